```python
import math
import jax, jax.numpy as jnp
from jax import lax
import numpy as np

D_MODEL = 1024
BATCH = 2
SEQ = 8192
DEPTH = 4

Q_BLOCK = 128
NEG_INF = -1e30
LN_EPS = 1e-5
RMS_EPS = 1e-6

MLA_HEADS = 4
MLA_Q_RANK = 256
MLA_KV_RANK = 128
MLA_NOPE = 64
MLA_ROPE = 32
MLA_V = 64
ROPE_THETA = 10000.0

SB_HEADS = 4
SB_DIM = 64

SW_Q_HEADS = 8
SW_KV_HEADS = 2
SW_DIM = 64
SW_WINDOW = 128

DF_HEADS = 4
DF_DIM = 32
DF_VDIM = 2 * DF_DIM

A_WIDTH = MLA_HEADS * MLA_V
B_WIDTH = SB_HEADS * SB_DIM
C_WIDTH = SW_Q_HEADS * SW_DIM
D_WIDTH = DF_HEADS * DF_VDIM
N_BRANCH = 4

IN_SPLITS = (MLA_Q_RANK, MLA_KV_RANK, MLA_ROPE, 3 * B_WIDTH, C_WIDTH, 2 * SW_KV_HEADS * SW_DIM, 2 * DF_HEADS * DF_DIM, 2 * DF_HEADS * DF_DIM, D_WIDTH)
IN_WIDTH = sum(IN_SPLITS)

N_EXPERTS = 32
N_GROUPS = 8
EXPERTS_PER_GROUP = N_EXPERTS // N_GROUPS
GROUP_SCORE_TOP = 2
TOP_K = 2
D_EXPERT = 256
MOE_BLOCK = 256

DN_ALPHA = (2 * DEPTH) ** 0.25
DN_BETA = (8 * DEPTH) ** -0.25

kernel_name = 'hybrid_mla_stickbreak_swa_diff_grouped_moe'


def layer_norm(x, g, b):
    xf = x.astype(jnp.float32)
    mu = xf.mean(-1, keepdims=True)
    var = jnp.square(xf - mu).mean(-1, keepdims=True)
    return ((xf - mu) * lax.rsqrt(var + LN_EPS) * g.astype(jnp.float32) + b.astype(jnp.float32)).astype(x.dtype)


def rms_norm(x, g):
    xf = x.astype(jnp.float32)
    return (xf * lax.rsqrt(jnp.mean(xf * xf, -1, keepdims=True) + RMS_EPS) * g.astype(jnp.float32)).astype(x.dtype)


def apply_rope(t, positions):
    half = t.shape[-1] // 2
    inv_freq = ROPE_THETA ** (-jnp.arange(half, dtype=jnp.float32) / half)
    ang = positions.astype(jnp.float32)[:, None] * inv_freq[None, :]
    cos = jnp.cos(ang)[None, :, None, :]
    sin = jnp.sin(ang)[None, :, None, :]
    tf = t.astype(jnp.float32)
    t1, t2 = tf[..., :half], tf[..., half:]
    return jnp.concatenate([t1 * cos - t2 * sin, t1 * sin + t2 * cos], axis=-1).astype(t.dtype)


def alibi_slopes(n_heads):
    return 2.0 ** (-8.0 * jnp.arange(1, n_heads + 1, dtype=jnp.float32) / n_heads)


def _to_blocks(t):
    b, s = t.shape[:2]
    return t.reshape(b, s // Q_BLOCK, Q_BLOCK, *t.shape[2:]).swapaxes(0, 1)


def _from_blocks(o):
    nq, b, qb = o.shape[:3]
    return o.swapaxes(0, 1).reshape(b, nq * qb, *o.shape[3:])


def mla_attention(q_nope, q_rope, k_nope, k_rope, v):
    s_len = q_nope.shape[1]
    scale = (MLA_NOPE + MLA_ROPE) ** -0.5
    kidx = jnp.arange(s_len)

    def block(args):
        qn, qr, i = args
        qidx = i * Q_BLOCK + jnp.arange(Q_BLOCK)
        sc = (jnp.einsum('bqhd,bkhd->bhqk', qn, k_nope, preferred_element_type=jnp.float32)
              + jnp.einsum('bqhd,bkd->bhqk', qr, k_rope, preferred_element_type=jnp.float32)) * scale
        sc = jnp.where(kidx[None, :] <= qidx[:, None], sc, NEG_INF)
        p = jax.nn.softmax(sc, axis=-1)
        return jnp.einsum('bhqk,bkhd->bqhd', p.astype(v.dtype), v)

    out = lax.map(block, (_to_blocks(q_nope), _to_blocks(q_rope), jnp.arange(s_len // Q_BLOCK)))
    return _from_blocks(out)


def stick_breaking_attention(q, k, v):
    s_len = q.shape[1]
    scale = SB_DIM ** -0.5
    kidx = jnp.arange(s_len)

    def block(args):
        qb, i = args
        qidx = i * Q_BLOCK + jnp.arange(Q_BLOCK)
        z = jnp.einsum('bqhd,bkhd->bhqk', qb, k, preferred_element_type=jnp.float32) * scale
        mask = kidx[None, :] < qidx[:, None]
        log_beta = jax.nn.log_sigmoid(z)
        log_1m_beta = jnp.where(mask, log_beta - z, 0.0)
        suffix = lax.cumsum(log_1m_beta, axis=3, reverse=True) - log_1m_beta
        a = jnp.where(mask, jnp.exp(log_beta + suffix), 0.0)
        return jnp.einsum('bhqk,bkhd->bqhd', a.astype(v.dtype), v)

    out = lax.map(block, (_to_blocks(q), jnp.arange(s_len // Q_BLOCK)))
    return _from_blocks(out)


def sliding_window_attention(q, k, v, sinks, slopes, positions):
    b, s_len, _, d = q.shape
    w = SW_WINDOW
    nb = s_len // w
    g = SW_Q_HEADS // SW_KV_HEADS
    qb = q.reshape(b, nb, w, SW_KV_HEADS, g, d)

    def band(t):
        tb = t.reshape(b, nb, w, SW_KV_HEADS, d)
        prev = jnp.pad(tb[:, :-1], ((0, 0), (1, 0), (0, 0), (0, 0), (0, 0)))
        return jnp.concatenate([prev, tb], axis=2)

    kb, vb = band(k), band(v)
    pq = positions.astype(jnp.float32).reshape(nb, w)
    pk = jnp.concatenate([jnp.pad(pq[:-1], ((1, 0), (0, 0))), pq], axis=1)
    dist = pq[:, :, None] - pk[:, None, :]
    qi = jnp.arange(w)[:, None]
    kj = jnp.arange(2 * w)[None, :]
    offset = qi + w - kj
    in_window = (offset >= 0) & (offset < SW_WINDOW)
    key_exists = (jnp.arange(nb)[:, None, None] > 0) | (kj[None] >= w)
    mask = (in_window[None] & key_exists)[None, :, None, None]
    sc = jnp.einsum('bnqhgd,bnkhd->bnhgqk', qb, kb, preferred_element_type=jnp.float32) * (d ** -0.5)
    bias = -slopes.reshape(SW_KV_HEADS, g)[None, None, :, :, None, None] * dist[None, :, None, None]
    sc = jnp.where(mask, sc + bias, NEG_INF)
    sink = jnp.broadcast_to(sinks.astype(jnp.float32).reshape(SW_KV_HEADS, g)[None, None, :, :, None, None], sc.shape[:-1] + (1,))
    p = jax.nn.softmax(jnp.concatenate([sc, sink], axis=-1), axis=-1)[..., :-1]
    out = jnp.einsum('bnhgqk,bnkhd->bnqhgd', p.astype(v.dtype), vb)
    return out.reshape(b, s_len, SW_Q_HEADS, d)


def diff_attention(q1, q2, k1, k2, v, lam, slopes, positions):
    s_len, d = q1.shape[1], q1.shape[3]
    scale = d ** -0.5
    kidx = jnp.arange(s_len)
    kpos = positions.astype(jnp.float32)

    def block(args):
        a1, a2, i = args
        qidx = i * Q_BLOCK + jnp.arange(Q_BLOCK)
        qpos = lax.dynamic_slice(kpos, (i * Q_BLOCK,), (Q_BLOCK,))
        causal = kidx[None, :] <= qidx[:, None]
        bias = -slopes[:, None, None] * (qpos[:, None] - kpos[None, :])[None]

        def attn_map(a, kk):
            sc = jnp.einsum('bqhd,bkhd->bhqk', a, kk, preferred_element_type=jnp.float32) * scale + bias
            return jax.nn.softmax(jnp.where(causal, sc, NEG_INF), axis=-1)

        p = attn_map(a1, k1) - lam * attn_map(a2, k2)
        return jnp.einsum('bhqk,bkhd->bqhd', p.astype(v.dtype), v)

    out = lax.map(block, (_to_blocks(q1), _to_blocks(q2), jnp.arange(s_len // Q_BLOCK)))
    return _from_blocks(out)


def token_mixers(x, positions, lam_init, w_in, q_norm, kv_norm, w_uq, w_ukv, sinks,
                 lq1, lk1, lq2, lk2, subln, w_gate, b_gate, w_br_a, w_br_b, w_br_c, w_br_d, w_out):
    b, s_len, _ = x.shape
    h = jnp.einsum('bsd,de->bse', x, w_in)
    split_at = [int(c) for c in np.cumsum(IN_SPLITS)[:-1]]
    c_q, c_kv, k_rope_raw, sb_qkv, sw_q, sw_kv, df_q, df_k, df_v = jnp.split(h, split_at, axis=-1)

    q = jnp.einsum('bsr,re->bse', rms_norm(c_q, q_norm), w_uq).reshape(b, s_len, MLA_HEADS, MLA_NOPE + MLA_ROPE)
    q_nope = q[..., :MLA_NOPE]
    q_rope = apply_rope(q[..., MLA_NOPE:], positions)
    kv = jnp.einsum('bsr,re->bse', rms_norm(c_kv, kv_norm), w_ukv).reshape(b, s_len, MLA_HEADS, MLA_NOPE + MLA_V)
    k_nope, v_a = kv[..., :MLA_NOPE], kv[..., MLA_NOPE:]
    k_rope = apply_rope(k_rope_raw[:, :, None, :], positions)[:, :, 0]
    o_a = mla_attention(q_nope, q_rope, k_nope, k_rope, v_a).reshape(b, s_len, A_WIDTH)

    sb = sb_qkv.reshape(b, s_len, 3, SB_HEADS, SB_DIM)
    o_b = stick_breaking_attention(sb[:, :, 0], sb[:, :, 1], sb[:, :, 2]).reshape(b, s_len, B_WIDTH)

    swkv = sw_kv.reshape(b, s_len, 2, SW_KV_HEADS, SW_DIM)
    o_c = sliding_window_attention(sw_q.reshape(b, s_len, SW_Q_HEADS, SW_DIM), swkv[:, :, 0], swkv[:, :, 1],
                                   sinks, alibi_slopes(SW_Q_HEADS), positions).reshape(b, s_len, C_WIDTH)

    dq = df_q.reshape(b, s_len, DF_HEADS, 2, DF_DIM)
    dk = df_k.reshape(b, s_len, DF_HEADS, 2, DF_DIM)
    f32 = jnp.float32
    lam = (jnp.exp(jnp.sum(lq1.astype(f32) * lk1.astype(f32)))
           - jnp.exp(jnp.sum(lq2.astype(f32) * lk2.astype(f32))) + lam_init)
    o = diff_attention(dq[..., 0, :], dq[..., 1, :], dk[..., 0, :], dk[..., 1, :],
                       df_v.reshape(b, s_len, DF_HEADS, DF_VDIM), lam, alibi_slopes(DF_HEADS), positions)
    o_d = (rms_norm(o, subln) * (1.0 - lam_init)).reshape(b, s_len, D_WIDTH)

    gates = jax.nn.sigmoid(jnp.einsum('bsd,de->bse', x, w_gate) + b_gate).reshape(b, s_len, N_BRANCH, D_MODEL)
    merged = (gates[:, :, 0] * jnp.einsum('bsw,wd->bsd', o_a, w_br_a)
              + gates[:, :, 1] * jnp.einsum('bsw,wd->bsd', o_b, w_br_b)
              + gates[:, :, 2] * jnp.einsum('bsw,wd->bsd', o_c, w_br_c)
              + gates[:, :, 3] * jnp.einsum('bsw,wd->bsd', o_d, w_br_d))
    return jnp.einsum('bsd,de->bse', merged, w_out)


def grouped_moe(x, w_router, router_bias, w1, w3, w2):
    b, s_len, d = x.shape
    xt = x.reshape(b * s_len, d)
    t = xt.shape[0]
    scores = jax.nn.sigmoid(jnp.dot(xt.astype(jnp.float32), w_router.astype(jnp.float32)))
    biased = (scores + router_bias.astype(jnp.float32)).reshape(t, N_GROUPS, EXPERTS_PER_GROUP)
    group_score = lax.top_k(biased, GROUP_SCORE_TOP)[0].sum(-1)
    g_sel = jnp.argmax(group_score, axis=-1).astype(jnp.int32)
    in_group = jnp.take_along_axis(biased, g_sel[:, None, None], axis=1)[:, 0]
    local = lax.top_k(in_group, TOP_K)[1]
    e_idx = (g_sel[:, None] * EXPERTS_PER_GROUP + local).astype(jnp.int32)
    gate = jnp.take_along_axis(scores, e_idx, axis=1)
    gate = gate / gate.sum(-1, keepdims=True)

    tk = t * TOP_K
    flat_e = e_idx.reshape(-1)
    order = jnp.argsort(flat_e)
    sorted_e = flat_e[order]
    tok = (order // TOP_K).astype(jnp.int32)
    counts = jnp.bincount(flat_e, length=N_EXPERTS)
    padded = (counts + MOE_BLOCK - 1) // MOE_BLOCK * MOE_BLOCK
    pad_end = jnp.cumsum(padded)
    pad_start = pad_end - padded
    start = jnp.cumsum(counts) - counts
    dest = (pad_start[sorted_e] + jnp.arange(tk) - start[sorted_e]).astype(jnp.int32)
    n_blocks = -(-tk // MOE_BLOCK) + N_EXPERTS
    rows = n_blocks * MOE_BLOCK
    row_tok = jnp.full((rows,), t, jnp.int32).at[dest].set(tok)
    x_pad = jnp.concatenate([xt, jnp.zeros((1, d), xt.dtype)], axis=0)
    xs = x_pad[row_tok].reshape(n_blocks, MOE_BLOCK, d)
    block_e = jnp.minimum(jnp.searchsorted(pad_end, jnp.arange(n_blocks) * MOE_BLOCK, side='right'), N_EXPERTS - 1)

    def expert_block(args):
        xb, e = args
        hid = jax.nn.silu(xb @ w1[e]) * (xb @ w3[e])
        return hid @ w2[e]

    ys = lax.map(expert_block, (xs, block_e)).reshape(rows, d)
    gate_sorted = gate.reshape(-1)[order].astype(x.dtype)
    y = jnp.zeros((t, d), x.dtype).at[tok].add(ys[dest] * gate_sorted[:, None])
    return y.reshape(b, s_len, d)


def setup_inputs(seed: int = 0) -> dict:
    key = jax.random.key(seed)
    ks = iter(jax.random.split(key, 40))

    def nrm(shape, scale):
        return jax.random.normal(next(ks), shape, jnp.float32) * scale

    def gain(shape):
        return 1.0 + 0.02 * jax.random.normal(next(ks), shape, jnp.float32)

    L = DEPTH
    return {
        'x': nrm((BATCH, SEQ, D_MODEL), 1.0),
        'positions': jnp.arange(SEQ, dtype=jnp.int32),
        'w_in': nrm((L, D_MODEL, IN_WIDTH), D_MODEL ** -0.5),
        'mla_q_norm': gain((L, MLA_Q_RANK)),
        'mla_kv_norm': gain((L, MLA_KV_RANK)),
        'w_uq': nrm((L, MLA_Q_RANK, MLA_HEADS * (MLA_NOPE + MLA_ROPE)), MLA_Q_RANK ** -0.5),
        'w_ukv': nrm((L, MLA_KV_RANK, MLA_HEADS * (MLA_NOPE + MLA_V)), MLA_KV_RANK ** -0.5),
        'sw_sinks': nrm((L, SW_Q_HEADS), 0.5),
        'df_lq1': nrm((L, DF_DIM), 0.1),
        'df_lk1': nrm((L, DF_DIM), 0.1),
        'df_lq2': nrm((L, DF_DIM), 0.1),
        'df_lk2': nrm((L, DF_DIM), 0.1),
        'df_subln': gain((L, DF_VDIM)),
        'w_gate': nrm((L, D_MODEL, N_BRANCH * D_MODEL), D_MODEL ** -0.5),
        'b_gate': nrm((L, N_BRANCH * D_MODEL), 0.01),
        'w_br_a': nrm((L, A_WIDTH, D_MODEL), DN_BETA * A_WIDTH ** -0.5),
        'w_br_b': nrm((L, B_WIDTH, D_MODEL), DN_BETA * B_WIDTH ** -0.5),
        'w_br_c': nrm((L, C_WIDTH, D_MODEL), DN_BETA * C_WIDTH ** -0.5),
        'w_br_d': nrm((L, D_WIDTH, D_MODEL), DN_BETA * D_WIDTH ** -0.5),
        'w_out': nrm((L, D_MODEL, D_MODEL), DN_BETA * D_MODEL ** -0.5),
        'ln1_g': gain((L, D_MODEL)),
        'ln1_b': nrm((L, D_MODEL), 0.01),
        'w_router': nrm((D_MODEL, N_EXPERTS), D_MODEL ** -0.5),
        'router_bias': nrm((N_EXPERTS,), 0.01),
        'moe_w1': nrm((L, N_EXPERTS, D_MODEL, D_EXPERT), D_MODEL ** -0.5),
        'moe_w3': nrm((L, N_EXPERTS, D_MODEL, D_EXPERT), DN_BETA * D_MODEL ** -0.5),
        'moe_w2': nrm((L, N_EXPERTS, D_EXPERT, D_MODEL), DN_BETA * D_EXPERT ** -0.5),
        'ln2_g': gain((L, D_MODEL)),
        'ln2_b': nrm((L, D_MODEL), 0.01),
    }


def reference(x, positions, w_in, mla_q_norm, mla_kv_norm, w_uq, w_ukv, sw_sinks,
              df_lq1, df_lk1, df_lq2, df_lk2, df_subln, w_gate, b_gate,
              w_br_a, w_br_b, w_br_c, w_br_d, w_out, ln1_g, ln1_b,
              w_router, router_bias, moe_w1, moe_w3, moe_w2, ln2_g, ln2_b):
    for l in range(DEPTH):
        lam_init = 0.8 - 0.6 * math.exp(-0.3 * l)
        mix = token_mixers(x, positions, lam_init, w_in[l], mla_q_norm[l], mla_kv_norm[l], w_uq[l], w_ukv[l],
                           sw_sinks[l], df_lq1[l], df_lk1[l], df_lq2[l], df_lk2[l], df_subln[l],
                           w_gate[l], b_gate[l], w_br_a[l], w_br_b[l], w_br_c[l], w_br_d[l], w_out[l])
        x = layer_norm(DN_ALPHA * x + mix, ln1_g[l], ln1_b[l])
        ffn = grouped_moe(x, w_router, router_bias, moe_w1[l], moe_w3[l], moe_w2[l])
        x = layer_norm(DN_ALPHA * x + ffn, ln2_g[l], ln2_b[l])
    return x
```

```python
import functools
import math

import numpy as np
import jax
import jax.numpy as jnp
from jax import lax
from jax.experimental import pallas as pl
from jax.experimental.pallas import tpu as pltpu

F32 = jnp.float32
BF16 = jnp.bfloat16

D_MODEL = 1024
NEG_INF = -1e30
LN_EPS = 1e-5
RMS_EPS = 1e-6

MLA_HEADS = 4
MLA_Q_RANK = 256
MLA_KV_RANK = 128
MLA_NOPE = 64
MLA_ROPE = 32
MLA_V = 64
MLA_QK = MLA_NOPE + MLA_ROPE
ROPE_THETA = 10000.0

SB_HEADS = 4
SB_DIM = 64

SW_Q_HEADS = 8
SW_KV_HEADS = 2
SW_DIM = 64
SW_WINDOW = 128
SW_GROUP = SW_Q_HEADS // SW_KV_HEADS

DF_HEADS = 4
DF_DIM = 32
DF_VDIM = 2 * DF_DIM

A_WIDTH = MLA_HEADS * MLA_V
B_WIDTH = SB_HEADS * SB_DIM
C_WIDTH = SW_Q_HEADS * SW_DIM
D_WIDTH = DF_HEADS * DF_VDIM
N_BRANCH = 4

IN_SPLITS = (MLA_Q_RANK, MLA_KV_RANK, MLA_ROPE, 3 * B_WIDTH, C_WIDTH, 2 * SW_KV_HEADS * SW_DIM,
             2 * DF_HEADS * DF_DIM, 2 * DF_HEADS * DF_DIM, D_WIDTH)

N_EXPERTS = 32
N_GROUPS = 8
EXPERTS_PER_GROUP = N_EXPERTS // N_GROUPS
D_EXPERT = 256

LANES = 128
SUBLANES = 8

OFF_CQ = 0
OFF_CKV = OFF_CQ + MLA_Q_RANK
OFF_KR = OFF_CKV + MLA_KV_RANK
OFF_KRS = OFF_KR + LANES
OFF_REST = OFF_KRS + LANES
REST_WIDTH = sum(IN_SPLITS[3:])
IN_PAD_WIDTH = OFF_REST + REST_WIDTH

TOKEN_BLOCK = 256
ATT_BLOCK = 256
MOE_ROWS = 256
DMA_CHUNK = 256
VMEM_LIMIT = 56 * 1024 * 1024

AUG_G0 = 0
AUG_GSEL = 4
AUG_RANK = 5


def _cparams(sem):
    return pltpu.CompilerParams(dimension_semantics=sem, vmem_limit_bytes=VMEM_LIMIT)


def _nt_dot(a, b, **kw):
    return lax.dot_general(a, b, (((1,), (1,)), ((), ())), preferred_element_type=F32, **kw)


def _const_spec(shape):
    nd = len(shape)
    return pl.BlockSpec(shape, lambda *_: (0,) * nd)


def _proj_kernel(x_ref, win_ref, qn_ref, kvn_ref, wuq_ref, wuqs_ref, wukv_ref,
                 cq_ref, sq_ref, ck_ref, sk_ref, rscale_ref,
                 hq_ref, kv_ref, kr_ref, hr_ref):
    xb = x_ref[...].astype(BF16)
    h = jnp.dot(xb, win_ref[...], preferred_element_type=F32)
    c_q = h[:, OFF_CQ:OFF_CQ + MLA_Q_RANK]
    c_kv = h[:, OFF_CKV:OFF_CKV + MLA_KV_RANK]
    nq = c_q * lax.rsqrt(jnp.mean(c_q * c_q, axis=-1, keepdims=True) + RMS_EPS) * qn_ref[...]
    nqb = nq.astype(BF16)
    q = jnp.dot(nqb, wuq_ref[...], preferred_element_type=F32)
    qs = jnp.dot(nqb, wuqs_ref[...], preferred_element_type=F32)
    hq_ref[...] = (q * cq_ref[...] + qs * sq_ref[...]).astype(hq_ref.dtype)
    nkv = c_kv * lax.rsqrt(jnp.mean(c_kv * c_kv, axis=-1, keepdims=True) + RMS_EPS) * kvn_ref[...]
    kv_ref[...] = jnp.dot(nkv.astype(BF16), wukv_ref[...], preferred_element_type=F32).astype(kv_ref.dtype)
    kr = h[:, OFF_KR:OFF_KR + LANES] * ck_ref[...] + h[:, OFF_KRS:OFF_KRS + LANES] * sk_ref[...]
    kr_ref[...] = kr.astype(kr_ref.dtype)
    hr_ref[...] = (h[:, OFF_REST:] * rscale_ref[...]).astype(hr_ref.dtype)


def _proj(x2, win, qn, kvn, wuq, wuqs, wukv, cq, sq, ck, sk, rscale):
    t = x2.shape[0]
    tm = TOKEN_BLOCK
    row = lambda w: pl.BlockSpec((tm, w), lambda i: (i, 0))
    qw = MLA_HEADS * MLA_QK
    kvw = MLA_HEADS * (MLA_NOPE + MLA_V)
    return pl.pallas_call(
        _proj_kernel,
        grid=(t // tm,),
        in_specs=[row(D_MODEL), _const_spec(win.shape), _const_spec(qn.shape), _const_spec(kvn.shape),
                  _const_spec(wuq.shape), _const_spec(wuqs.shape), _const_spec(wukv.shape),
                  row(qw), row(qw), row(LANES), row(LANES), _const_spec(rscale.shape)],
        out_specs=[row(qw), row(kvw), row(LANES), row(REST_WIDTH)],
        out_shape=[jax.ShapeDtypeStruct((t, qw), BF16), jax.ShapeDtypeStruct((t, kvw), BF16),
                   jax.ShapeDtypeStruct((t, LANES), BF16), jax.ShapeDtypeStruct((t, REST_WIDTH), BF16)],
        compiler_params=_cparams(("arbitrary",)),
        name="proj",
    )(x2, win, qn, kvn, wuq, wuqs, wukv, cq, sq, ck, sk, rscale)


def _causal_mask(n):
    row = lax.broadcasted_iota(jnp.int32, (n, n), 0)
    col = lax.broadcasted_iota(jnp.int32, (n, n), 1)
    return row, col


def _softmax_step(s, v, m, l, acc):
    m_new = jnp.maximum(m, jnp.max(s, axis=-1, keepdims=True))
    p = jnp.exp(s - m_new)
    alpha = jnp.exp(m - m_new)
    l_new = alpha * l + jnp.sum(p, axis=-1, keepdims=True)
    acc_new = alpha * acc + jnp.dot(p.astype(BF16), v, preferred_element_type=F32)
    return m_new, l_new, acc_new


def _mla_kernel(q_ref, k_ref, v_ref, o_ref, *, blk):
    qi = pl.program_id(2)
    q = q_ref[0, 0]

    def kv_block(ki):
        start = pl.multiple_of(ki * blk, blk)
        return k_ref[0, 0, pl.ds(start, blk), :], v_ref[0, 0, pl.ds(start, blk), :]

    def body(ki, carry):
        k, v = kv_block(ki)
        return _softmax_step(_nt_dot(q, k), v, *carry)

    init = (jnp.full((blk, 1), NEG_INF, F32), jnp.zeros((blk, 1), F32), jnp.zeros((blk, MLA_V), F32))
    carry = lax.fori_loop(0, qi, body, init)
    k, v = kv_block(qi)
    row, col = _causal_mask(blk)
    s = jnp.where(col <= row, _nt_dot(q, k), NEG_INF)
    m, l, acc = _softmax_step(s, v, *carry)
    o_ref[0, 0] = (acc / l).astype(o_ref.dtype)


def _head_spec(blk, d):
    return pl.BlockSpec((1, 1, blk, d), lambda b, h, i: (b, h, i, 0))


def _full_spec(s_len, d):
    return pl.BlockSpec((1, 1, s_len, d), lambda b, h, i: (b, h, 0, 0))


def _mla_attention(q, k, v):
    b, nh, s_len, _ = q.shape
    blk = ATT_BLOCK
    return pl.pallas_call(
        functools.partial(_mla_kernel, blk=blk),
        grid=(b, nh, s_len // blk),
        in_specs=[_head_spec(blk, MLA_QK), _full_spec(s_len, MLA_QK), _full_spec(s_len, MLA_V)],
        out_specs=_head_spec(blk, MLA_V),
        out_shape=jax.ShapeDtypeStruct((b, nh, s_len, MLA_V), BF16),
        compiler_params=_cparams(("arbitrary",) * 3),
        name="mla",
    )(q, k, v)


def _log_sigmoid(z):
    return jnp.minimum(z, 0.0) - jnp.log(1.0 + jnp.exp(-jnp.abs(z)))


def _suffix_sum(l1m, upper):
    hi = l1m.astype(BF16)
    lo = (l1m - hi.astype(F32)).astype(BF16)
    return (jnp.dot(hi, upper, preferred_element_type=F32) + jnp.dot(lo, upper, preferred_element_type=F32))


def _sb_kernel(q_ref, k_ref, v_ref, o_ref, *, blk):
    qi = pl.program_id(2)
    q = q_ref[0, 0]
    row, col = _causal_mask(blk)
    upper = (row > col).astype(BF16)

    def kv_block(ki):
        start = pl.multiple_of(ki * blk, blk)
        return k_ref[0, 0, pl.ds(start, blk), :], v_ref[0, 0, pl.ds(start, blk), :]

    k, v = kv_block(qi)
    z = _nt_dot(q, k)
    mask = col < row
    lb = _log_sigmoid(z)
    l1m = jnp.where(mask, lb - z, 0.0)
    suffix = _suffix_sum(l1m, upper)
    a = jnp.where(mask, jnp.exp(lb + suffix), 0.0)
    acc = jnp.dot(a.astype(BF16), v, preferred_element_type=F32)
    passed = suffix[:, 0:1] + l1m[:, 0:1]

    def body(j, carry):
        passed, acc = carry
        k, v = kv_block(qi - 1 - j)
        z = _nt_dot(q, k)
        lb = _log_sigmoid(z)
        l1m = lb - z
        suffix = _suffix_sum(l1m, upper)
        a = jnp.exp(lb + suffix + passed)
        acc = acc + jnp.dot(a.astype(BF16), v, preferred_element_type=F32)
        return passed + suffix[:, 0:1] + l1m[:, 0:1], acc

    _, acc = lax.fori_loop(0, qi, body, (passed, acc))
    o_ref[0, 0] = acc.astype(o_ref.dtype)


def _sb_attention(q, k, v):
    b, nh, s_len, d = q.shape
    blk = ATT_BLOCK
    return pl.pallas_call(
        functools.partial(_sb_kernel, blk=blk),
        grid=(b, nh, s_len // blk),
        in_specs=[_head_spec(blk, d), _full_spec(s_len, d), _full_spec(s_len, d)],
        out_specs=_head_spec(blk, d),
        out_shape=jax.ShapeDtypeStruct((b, nh, s_len, d), BF16),
        compiler_params=_cparams(("arbitrary",) * 3),
        name="sb",
    )(q, k, v)


def _swa_kernel(slopes_ref, sinks_ref, q_ref, kp_ref, kc_ref, vp_ref, vc_ref,
                pq_ref, pkp_ref, pkc_ref, o_ref):
    g = pl.program_id(1)
    i = pl.program_id(2)
    w = SW_WINDOW
    q = q_ref[0].reshape(SW_GROUP * w, SW_DIM)
    k = jnp.concatenate([kp_ref[0, 0], kc_ref[0, 0]], axis=0)
    v = jnp.concatenate([vp_ref[0, 0], vc_ref[0, 0]], axis=0)
    s = _nt_dot(q, k)
    pk = jnp.concatenate([pkp_ref[...], pkc_ref[...]], axis=1)
    dist = pq_ref[...] - pk
    qi = lax.broadcasted_iota(jnp.int32, (w, 2 * w), 0)
    kj = lax.broadcasted_iota(jnp.int32, (w, 2 * w), 1)
    mask = (kj > qi) & (kj <= qi + w) & ((kj >= w) | (i > 0))
    for j in range(SW_GROUP):
        head = g * SW_GROUP + j
        sj = jnp.where(mask, s[j * w:(j + 1) * w] - slopes_ref[head] * dist, NEG_INF)
        sink = sinks_ref[head]
        m = jnp.maximum(jnp.max(sj, axis=-1, keepdims=True), sink)
        p = jnp.exp(sj - m)
        denom = jnp.sum(p, axis=-1, keepdims=True) + jnp.exp(sink - m)
        o = jnp.dot(p.astype(BF16), v, preferred_element_type=F32) / denom
        o_ref[0, j] = o.astype(o_ref.dtype)


def _swa_attention(q, k, v, slopes, sinks, pos_col, pos_row):
    b, _, s_len, d = q.shape
    w = SW_WINDOW
    prev = lambda i: jnp.maximum(i - 1, 0)
    smem = pl.BlockSpec(memory_space=pltpu.SMEM)
    kv_prev = pl.BlockSpec((1, 1, w, d), lambda bb, g, i: (bb, g, prev(i), 0))
    kv_cur = pl.BlockSpec((1, 1, w, d), lambda bb, g, i: (bb, g, i, 0))
    return pl.pallas_call(
        _swa_kernel,
        grid=(b, SW_KV_HEADS, s_len // w),
        in_specs=[smem, smem,
                  pl.BlockSpec((1, SW_GROUP, w, d), lambda bb, g, i: (bb, g, i, 0)),
                  kv_prev, kv_cur, kv_prev, kv_cur,
                  pl.BlockSpec((w, 1), lambda bb, g, i: (i, 0)),
                  pl.BlockSpec((1, w), lambda bb, g, i: (0, prev(i))),
                  pl.BlockSpec((1, w), lambda bb, g, i: (0, i))],
        out_specs=pl.BlockSpec((1, SW_GROUP, w, d), lambda bb, g, i: (bb, g, i, 0)),
        out_shape=jax.ShapeDtypeStruct((b, SW_Q_HEADS, s_len, d), BF16),
        compiler_params=_cparams(("arbitrary",) * 3),
        name="swa",
    )(slopes, sinks, q, k, k, v, v, pos_col, pos_row, pos_row)


def _diff_kernel(slopes_ref, q_ref, k_ref, v_ref, pq_ref, pk_ref, lq1_ref, lk1_ref, lq2_ref, lk2_ref,
                 subln_ref, o_ref, *, blk, lam_init):
    h = pl.program_id(1)
    qi = pl.program_id(2)
    q = q_ref[0, 0]
    lane = lax.broadcasted_iota(jnp.int32, q.shape, 1)
    q1 = jnp.where(lane < DF_DIM, q, jnp.zeros_like(q))
    q2 = jnp.where(lane >= DF_DIM, q, jnp.zeros_like(q))
    slope = slopes_ref[h]
    qb = slope * pq_ref[...]

    def block(ki):
        start = pl.multiple_of(ki * blk, blk)
        k = k_ref[0, 0, pl.ds(start, blk), :]
        v = v_ref[0, 0, pl.ds(start, blk), :]
        bias = slope * pk_ref[:, pl.ds(start, blk)] - qb
        return _nt_dot(q1, k) + bias, _nt_dot(q2, k) + bias, v

    def body(ki, carry):
        s1, s2, v = block(ki)
        return _softmax_step(s1, v, *carry[:3]) + _softmax_step(s2, v, *carry[3:])

    one = (jnp.full((blk, 1), NEG_INF, F32), jnp.zeros((blk, 1), F32), jnp.zeros((blk, DF_VDIM), F32))
    carry = lax.fori_loop(0, qi, body, one + one)
    s1, s2, v = block(qi)
    row, col = _causal_mask(blk)
    causal = col <= row
    _, l1, acc1 = _softmax_step(jnp.where(causal, s1, NEG_INF), v, *carry[:3])
    _, l2, acc2 = _softmax_step(jnp.where(causal, s2, NEG_INF), v, *carry[3:])
    lam = (jnp.exp(jnp.sum(lq1_ref[...] * lk1_ref[...], keepdims=True))
           - jnp.exp(jnp.sum(lq2_ref[...] * lk2_ref[...], keepdims=True)) + lam_init)
    o = acc1 / l1 - lam * (acc2 / l2)
    o = o * lax.rsqrt(jnp.mean(o * o, axis=-1, keepdims=True) + RMS_EPS) * subln_ref[...] * (1.0 - lam_init)
    o_ref[0, 0] = o.astype(o_ref.dtype)


def _diff_attention(q, k, v, slopes, pos_col, pos_row, lq1, lk1, lq2, lk2, subln, lam_init):
    b, nh, s_len, d = q.shape
    blk = ATT_BLOCK
    smem = pl.BlockSpec(memory_space=pltpu.SMEM)
    vec = lambda n: pl.BlockSpec((1, n), lambda bb, h, i: (0, 0))
    return pl.pallas_call(
        functools.partial(_diff_kernel, blk=blk, lam_init=lam_init),
        grid=(b, nh, s_len // blk),
        in_specs=[smem, _head_spec(blk, d), _full_spec(s_len, d), _full_spec(s_len, DF_VDIM),
                  pl.BlockSpec((blk, 1), lambda bb, h, i: (i, 0)),
                  pl.BlockSpec((1, s_len), lambda bb, h, i: (0, 0)),
                  vec(DF_DIM), vec(DF_DIM), vec(DF_DIM), vec(DF_DIM), vec(DF_VDIM)],
        out_specs=_head_spec(blk, DF_VDIM),
        out_shape=jax.ShapeDtypeStruct((b, nh, s_len, DF_VDIM), BF16),
        compiler_params=_cparams(("arbitrary",) * 3),
        name="diff",
    )(slopes, q, k, v, pos_col, pos_row, lq1, lk1, lq2, lk2, subln)


def _layer_norm(v, g, b):
    mu = jnp.mean(v, axis=-1, keepdims=True)
    c = v - mu
    var = jnp.mean(c * c, axis=-1, keepdims=True)
    return c * lax.rsqrt(var + LN_EPS) * g + b


def _sigmoid(z):
    return 1.0 / (1.0 + jnp.exp(-z))


def _route(logits_t, bias_col):
    scores = _sigmoid(logits_t)
    biased = scores + bias_col
    ng = N_GROUPS
    sc = [scores[j * ng:(j + 1) * ng] for j in range(EXPERTS_PER_GROUP)]
    bs = [biased[j * ng:(j + 1) * ng] for j in range(EXPERTS_PER_GROUP)]
    gscore = None
    for a in range(EXPERTS_PER_GROUP):
        for c in range(a + 1, EXPERTS_PER_GROUP):
            pair = bs[a] + bs[c]
            gscore = pair if gscore is None else jnp.maximum(gscore, pair)
    gid = lax.broadcasted_iota(jnp.int32, gscore.shape, 0)
    best = jnp.max(gscore, axis=0, keepdims=True)
    gsel = jnp.min(jnp.where(gscore == best, gid, ng), axis=0, keepdims=True)
    onehot = gid == gsel
    pick = lambda a: jnp.sum(jnp.where(onehot, a, 0.0), axis=0, keepdims=True)
    b_in = [pick(a) for a in bs]
    s_in = [pick(a) for a in sc]
    chosen = []
    for j in range(EXPERTS_PER_GROUP):
        ahead = jnp.zeros_like(b_in[j])
        for c in range(EXPERTS_PER_GROUP):
            if c == j:
                continue
            beats = (b_in[c] > b_in[j]) | ((b_in[c] == b_in[j]) & (c < j))
            ahead = ahead + beats.astype(F32)
        chosen.append(jnp.where(ahead < 2.0, s_in[j], 0.0))
    total = chosen[0] + chosen[1] + chosen[2] + chosen[3]
    gates = [c / total for c in chosen]
    return gates, gsel, onehot


def _merge_kernel(x_ref, oa_ref, ob_ref, oc_ref, od_ref, wg_ref, bg_ref, wa_ref, wb_ref, wc_ref, wd_ref,
                  wo_ref, g_ref, b_ref, wr_ref, rb_ref, x1_ref, aug_ref, cnt_ref, count_sc, *, alpha):
    i = pl.program_id(0)
    tm = x_ref.shape[0]

    @pl.when(i == 0)
    def _():
        count_sc[...] = jnp.zeros_like(count_sc)

    x = x_ref[...]
    xb = x.astype(BF16)
    merged = None
    for n, (o_ref, w_ref) in enumerate(((oa_ref, wa_ref), (ob_ref, wb_ref), (oc_ref, wc_ref), (od_ref, wd_ref))):
        cols = slice(n * D_MODEL, (n + 1) * D_MODEL)
        gate = _sigmoid(jnp.dot(xb, wg_ref[:, cols], preferred_element_type=F32) + bg_ref[:, cols])
        term = gate * jnp.dot(o_ref[...], w_ref[...], preferred_element_type=F32)
        merged = term if merged is None else merged + term
    mix = jnp.dot(merged.astype(BF16), wo_ref[...], preferred_element_type=F32)
    y = _layer_norm(alpha * x + mix, g_ref[...], b_ref[...])
    x1_ref[...] = y

    logits_t = _nt_dot(wr_ref[...], y, precision=lax.Precision.HIGHEST)
    gates, gsel, onehot = _route(logits_t, rb_ref[...])
    before = (lax.broadcasted_iota(jnp.int32, (tm, tm), 0) < lax.broadcasted_iota(jnp.int32, (tm, tm), 1))
    prefix = jnp.dot(onehot.astype(BF16), before.astype(BF16), preferred_element_type=F32)
    seen = count_sc[...]
    rank = jnp.sum(jnp.where(onehot, prefix + seen[:, 0:1], 0.0), axis=0, keepdims=True)
    seen = seen + jnp.sum(onehot.astype(F32), axis=1, keepdims=True)
    count_sc[...] = seen
    cnt_ref[...] = seen
    rows = gates + [gsel.astype(F32), rank, jnp.zeros((LANES - 6, tm), F32)]
    aug_ref[...] = jnp.concatenate(rows, axis=0).T


def _merge(x2, oa, ob, oc, od, wg, bg, wa, wb, wc, wd, wo, g, b, wr_t, rb_col, alpha):
    t = x2.shape[0]
    tm = TOKEN_BLOCK
    row = lambda w: pl.BlockSpec((tm, w), lambda i: (i, 0))
    consts = (wg, bg, wa, wb, wc, wd, wo, g, b, wr_t, rb_col)
    return pl.pallas_call(
        functools.partial(_merge_kernel, alpha=alpha),
        grid=(t // tm,),
        in_specs=[row(D_MODEL), row(A_WIDTH), row(B_WIDTH), row(C_WIDTH), row(D_WIDTH)]
                 + [_const_spec(c.shape) for c in consts],
        out_specs=[row(D_MODEL), row(LANES), _const_spec((N_GROUPS, LANES))],
        out_shape=[jax.ShapeDtypeStruct((t, D_MODEL), F32), jax.ShapeDtypeStruct((t, LANES), F32),
                   jax.ShapeDtypeStruct((N_GROUPS, LANES), F32)],
        scratch_shapes=[pltpu.VMEM((N_GROUPS, LANES), F32)],
        compiler_params=_cparams(("arbitrary",)),
        name="merge",
    )(x2, oa, ob, oc, od, *consts)


def _row_moves(n_rows, start_one, wait_one):
    n_chunks = n_rows // DMA_CHUNK

    def issue(c):
        def one(r, _):
            start_one(c * DMA_CHUNK + r)
            return 0
        lax.fori_loop(0, DMA_CHUNK, one, 0)

    def drain():
        def one(r, _):
            wait_one()
            return 0
        lax.fori_loop(0, DMA_CHUNK, one, 0)

    issue(0)

    def wave(c, _):
        issue(c)
        drain()
        return 0

    lax.fori_loop(1, n_chunks, wave, 0)
    drain()


def _scatter_kernel(dest_ref, x_hbm, a_hbm, xz_hbm, az_hbm, xs_hbm, as_hbm, sems):
    del xz_hbm, az_hbm

    def x_copy(t, d):
        return pltpu.make_async_copy(x_hbm.at[t], xs_hbm.at[d], sems.at[0])

    def a_copy(t, d):
        return pltpu.make_async_copy(a_hbm.at[t], as_hbm.at[d], sems.at[1])

    def start_one(t):
        d = dest_ref[t]
        x_copy(t, d).start()
        a_copy(t, d).start()

    def wait_one():
        x_copy(0, 0).wait()
        a_copy(0, 0).wait()

    _row_moves(x_hbm.shape[0], start_one, wait_one)


def _scatter_rows(dest, x1, aug, n_rows):
    t = x1.shape[0]
    x3 = x1.reshape(t, D_MODEL // LANES, LANES)
    a3 = aug.reshape(t, 1, LANES)
    xz = jnp.zeros((n_rows,) + x3.shape[1:], x3.dtype)
    az = jnp.zeros((n_rows,) + a3.shape[1:], a3.dtype)
    anyspec = pl.BlockSpec(memory_space=pl.ANY)
    xs, augs = pl.pallas_call(
        _scatter_kernel,
        grid_spec=pltpu.PrefetchScalarGridSpec(
            num_scalar_prefetch=1, grid=(1,),
            in_specs=[anyspec] * 4, out_specs=[anyspec] * 2,
            scratch_shapes=[pltpu.SemaphoreType.DMA((2,))]),
        out_shape=[jax.ShapeDtypeStruct(xz.shape, xz.dtype), jax.ShapeDtypeStruct(az.shape, az.dtype)],
        input_output_aliases={3: 0, 4: 1},
        compiler_params=pltpu.CompilerParams(dimension_semantics=("arbitrary",), has_side_effects=True),
        name="scatter",
    )(dest, x3, a3, xz, az)
    return xs.reshape(n_rows, D_MODEL), augs.reshape(n_rows, LANES)


def _gather_kernel(dest_ref, src_hbm, out_hbm, sems):
    def copy(t, d):
        return pltpu.make_async_copy(src_hbm.at[d], out_hbm.at[t], sems.at[0])

    _row_moves(out_hbm.shape[0], lambda t: copy(t, dest_ref[t]).start(), lambda: copy(0, 0).wait())


def _gather_rows(dest, src, t):
    s3 = src.reshape(src.shape[0], D_MODEL // LANES, LANES)
    anyspec = pl.BlockSpec(memory_space=pl.ANY)
    out = pl.pallas_call(
        _gather_kernel,
        grid_spec=pltpu.PrefetchScalarGridSpec(
            num_scalar_prefetch=1, grid=(1,), in_specs=[anyspec], out_specs=anyspec,
            scratch_shapes=[pltpu.SemaphoreType.DMA((1,))]),
        out_shape=jax.ShapeDtypeStruct((t,) + s3.shape[1:], s3.dtype),
        compiler_params=pltpu.CompilerParams(dimension_semantics=("arbitrary",), has_side_effects=True),
        name="gather",
    )(dest, s3)
    return out.reshape(t, D_MODEL)


def _moe_kernel(bg_ref, nused_ref, xs_ref, aug_ref, w1_ref, w3_ref, w2_ref, g_ref, b_ref, o_ref, *, alpha):
    i = pl.program_id(0)

    @pl.when(i < nused_ref[0])
    def _():
        x = xs_ref[...]
        xb = x.astype(BF16)
        aug = aug_ref[...]
        hid = []
        for e in range(EXPERTS_PER_GROUP):
            h1 = jnp.dot(xb, w1_ref[e], preferred_element_type=F32)
            h3 = jnp.dot(xb, w3_ref[e], preferred_element_type=F32)
            gate = aug[:, AUG_G0 + e:AUG_G0 + e + 1]
            hid.append((h1 * _sigmoid(h1) * h3 * gate).astype(BF16))
        hid = jnp.concatenate(hid, axis=1)
        w2 = w2_ref[...].reshape(EXPERTS_PER_GROUP * D_EXPERT, D_MODEL)
        y = jnp.dot(hid, w2, preferred_element_type=F32)
        o_ref[...] = _layer_norm(alpha * x + y, g_ref[...], b_ref[...])

    @pl.when(i >= nused_ref[0])
    def _():
        o_ref[...] = jnp.zeros_like(o_ref)


def _moe(block_group, n_used, xs, augs, w1, w3, w2, g, b, alpha):
    n_rows = xs.shape[0]
    rb = MOE_ROWS
    epg = EXPERTS_PER_GROUP
    return pl.pallas_call(
        functools.partial(_moe_kernel, alpha=alpha),
        grid_spec=pltpu.PrefetchScalarGridSpec(
            num_scalar_prefetch=2, grid=(n_rows // rb,),
            in_specs=[pl.BlockSpec((rb, D_MODEL), lambda i, bg, nu: (i, 0)),
                      pl.BlockSpec((rb, LANES), lambda i, bg, nu: (i, 0)),
                      pl.BlockSpec((epg, D_MODEL, D_EXPERT), lambda i, bg, nu: (bg[i], 0, 0)),
                      pl.BlockSpec((epg, D_MODEL, D_EXPERT), lambda i, bg, nu: (bg[i], 0, 0)),
                      pl.BlockSpec((epg, D_EXPERT, D_MODEL), lambda i, bg, nu: (bg[i], 0, 0)),
                      pl.BlockSpec((1, D_MODEL), lambda i, bg, nu: (0, 0)),
                      pl.BlockSpec((1, D_MODEL), lambda i, bg, nu: (0, 0))],
            out_specs=pl.BlockSpec((rb, D_MODEL), lambda i, bg, nu: (i, 0))),
        out_shape=jax.ShapeDtypeStruct((n_rows, D_MODEL), F32),
        compiler_params=_cparams(("arbitrary",)),
        name="moe",
    )(block_group, n_used, xs, augs, w1, w3, w2, g, b)


def _alibi_slopes(n_heads):
    return 2.0 ** (-8.0 * jnp.arange(1, n_heads + 1, dtype=F32) / n_heads)


def _rope_tables(positions):
    half = MLA_ROPE // 2
    inv_freq = ROPE_THETA ** (-jnp.arange(half, dtype=F32) / half)
    ang = positions.astype(F32)[:, None] * inv_freq[None, :]
    cos = jnp.cos(ang)
    sin = jnp.sin(ang)
    return jnp.concatenate([cos, cos], axis=1), jnp.concatenate([sin, sin], axis=1)


def _rotate_half_columns(w):
    half = w.shape[-1] // 2
    return jnp.concatenate([-w[..., half:], w[..., :half]], axis=-1)


def _prep_w_in(w_in):
    split_at = [int(c) for c in np.cumsum(IN_SPLITS)[:-1]]
    parts = jnp.split(w_in, split_at, axis=-1)
    c_q, c_kv, k_rope = parts[0], parts[1], parts[2]
    pad = jnp.zeros((w_in.shape[0], LANES - MLA_ROPE), w_in.dtype)
    return jnp.concatenate([c_q, c_kv, k_rope, pad, _rotate_half_columns(k_rope), pad] + list(parts[3:]),
                           axis=-1).astype(BF16)


def _prep_w_uq(w_uq):
    w = w_uq.reshape(MLA_Q_RANK, MLA_HEADS, MLA_QK)
    rot = jnp.concatenate([jnp.zeros_like(w[..., :MLA_NOPE]), _rotate_half_columns(w[..., MLA_NOPE:])], axis=-1)
    return w_uq.astype(BF16), rot.reshape(w_uq.shape).astype(BF16)


def _heads(t2, b, s_len, nh, d):
    return t2.reshape(b, s_len, nh, d).transpose(0, 2, 1, 3)


def _unheads(t4):
    b, nh, s_len, d = t4.shape
    return t4.transpose(0, 2, 1, 3).reshape(b * s_len, nh * d)


def kernel(x, positions, w_in, mla_q_norm, mla_kv_norm, w_uq, w_ukv, sw_sinks, df_lq1, df_lk1, df_lq2, df_lk2,
           df_subln, w_gate, b_gate, w_br_a, w_br_b, w_br_c, w_br_d, w_out, ln1_g, ln1_b, w_router, router_bias,
           moe_w1, moe_w3, moe_w2, ln2_g, ln2_b):
    b, s_len, d = x.shape
    depth = w_in.shape[0]
    t = b * s_len
    alpha = (2 * depth) ** 0.25
    assert d == D_MODEL and t % TOKEN_BLOCK == 0 and s_len % ATT_BLOCK == 0 and t % DMA_CHUNK == 0
    assert ATT_BLOCK % SW_WINDOW == 0 and MOE_ROWS % SUBLANES == 0

    posf = positions.astype(F32)
    pos_col = posf[:, None]
    pos_row = posf[None, :]
    cos, sin = _rope_tables(positions)
    q_scale = MLA_QK ** -0.5
    ones_n = jnp.ones((s_len, MLA_NOPE), F32)
    cq = jnp.tile(jnp.concatenate([ones_n, cos], axis=1) * q_scale, (b, MLA_HEADS))
    sq = jnp.tile(jnp.concatenate([0.0 * ones_n, sin], axis=1) * q_scale, (b, MLA_HEADS))
    lane_pad = jnp.zeros((s_len, LANES - MLA_ROPE), F32)
    ck = jnp.tile(jnp.concatenate([cos, lane_pad], axis=1), (b, 1))
    sk = jnp.tile(jnp.concatenate([sin, lane_pad], axis=1), (b, 1))
    rscale = jnp.concatenate([
        jnp.full((B_WIDTH,), SB_DIM ** -0.5, F32), jnp.ones((2 * B_WIDTH,), F32),
        jnp.full((C_WIDTH,), SW_DIM ** -0.5, F32), jnp.ones((2 * SW_KV_HEADS * SW_DIM,), F32),
        jnp.full((2 * DF_HEADS * DF_DIM,), DF_DIM ** -0.5, F32), jnp.ones((2 * DF_HEADS * DF_DIM + D_WIDTH,), F32),
    ])[None, :]
    sw_slopes = _alibi_slopes(SW_Q_HEADS)
    df_slopes = _alibi_slopes(DF_HEADS)

    perm = np.arange(N_EXPERTS).reshape(N_GROUPS, EXPERTS_PER_GROUP).T.reshape(-1)
    wr_t = w_router.astype(F32).T[perm]
    rb_col = router_bias.astype(F32)[perm][:, None]

    n_rows = t + N_GROUPS * MOE_ROWS
    n_blocks = n_rows // MOE_ROWS

    x2 = x.reshape(t, d)
    for l in range(depth):
        lam_init = 0.8 - 0.6 * math.exp(-0.3 * l)
        wuq, wuqs = _prep_w_uq(w_uq[l])
        hq, kv, kr, hr = _proj(x2, _prep_w_in(w_in[l]), mla_q_norm[l][None, :], mla_kv_norm[l][None, :],
                               wuq, wuqs, w_ukv[l].astype(BF16), cq, sq, ck, sk, rscale)

        kv4 = kv.reshape(b, s_len, MLA_HEADS, MLA_NOPE + MLA_V)
        kr4 = jnp.broadcast_to(kr[:, :MLA_ROPE].reshape(b, s_len, 1, MLA_ROPE), (b, s_len, MLA_HEADS, MLA_ROPE))
        mla_q = _heads(hq, b, s_len, MLA_HEADS, MLA_QK)
        mla_k = jnp.concatenate([kv4[..., :MLA_NOPE], kr4], axis=-1).transpose(0, 2, 1, 3)
        mla_v = kv4[..., MLA_NOPE:].transpose(0, 2, 1, 3)
        o = 0
        sb_q = _heads(hr[:, o:o + B_WIDTH], b, s_len, SB_HEADS, SB_DIM); o += B_WIDTH
        sb_k = _heads(hr[:, o:o + B_WIDTH], b, s_len, SB_HEADS, SB_DIM); o += B_WIDTH
        sb_v = _heads(hr[:, o:o + B_WIDTH], b, s_len, SB_HEADS, SB_DIM); o += B_WIDTH
        sw_q = _heads(hr[:, o:o + C_WIDTH], b, s_len, SW_Q_HEADS, SW_DIM); o += C_WIDTH
        kvw = SW_KV_HEADS * SW_DIM
        sw_k = _heads(hr[:, o:o + kvw], b, s_len, SW_KV_HEADS, SW_DIM); o += kvw
        sw_v = _heads(hr[:, o:o + kvw], b, s_len, SW_KV_HEADS, SW_DIM); o += kvw
        dfw = 2 * DF_HEADS * DF_DIM
        df_q = _heads(hr[:, o:o + dfw], b, s_len, DF_HEADS, 2 * DF_DIM); o += dfw
        df_k = _heads(hr[:, o:o + dfw], b, s_len, DF_HEADS, 2 * DF_DIM); o += dfw
        df_v = _heads(hr[:, o:o + D_WIDTH], b, s_len, DF_HEADS, DF_VDIM)

        o_a = _unheads(_mla_attention(mla_q, mla_k, mla_v))
        o_b = _unheads(_sb_attention(sb_q, sb_k, sb_v))
        o_c = _unheads(_swa_attention(sw_q, sw_k, sw_v, sw_slopes, sw_sinks[l].astype(F32), pos_col, pos_row))
        o_d = _unheads(_diff_attention(df_q, df_k, df_v, df_slopes, pos_col, pos_row,
                                       df_lq1[l][None, :], df_lk1[l][None, :], df_lq2[l][None, :],
                                       df_lk2[l][None, :], df_subln[l][None, :], lam_init))

        x1, aug, cnt = _merge(x2, o_a, o_b, o_c, o_d, w_gate[l].astype(BF16), b_gate[l][None, :],
                              w_br_a[l].astype(BF16), w_br_b[l].astype(BF16), w_br_c[l].astype(BF16),
                              w_br_d[l].astype(BF16), w_out[l].astype(BF16), ln1_g[l][None, :], ln1_b[l][None, :],
                              wr_t, rb_col, alpha)

        gsel = aug[:, AUG_GSEL].astype(jnp.int32)
        rank = aug[:, AUG_RANK].astype(jnp.int32)
        counts = cnt[:, 0].astype(jnp.int32)
        padded = (counts + MOE_ROWS - 1) // MOE_ROWS * MOE_ROWS
        pad_end = jnp.cumsum(padded)
        dest = (pad_end - padded)[gsel] + rank
        block_group = jnp.minimum(
            jnp.searchsorted(pad_end, jnp.arange(n_blocks, dtype=jnp.int32) * MOE_ROWS, side='right'),
            N_GROUPS - 1).astype(jnp.int32)
        n_used = (pad_end[-1:] // MOE_ROWS).astype(jnp.int32)

        xs, augs = _scatter_rows(dest, x1, aug, n_rows)
        ys = _moe(block_group, n_used, xs, augs, moe_w1[l].astype(BF16), moe_w3[l].astype(BF16),
                  moe_w2[l].astype(BF16), ln2_g[l][None, :], ln2_b[l][None, :], alpha)
        x2 = _gather_rows(dest, ys, t)
    return x2.reshape(b, s_len, d)
```

```python
import functools
import math

import numpy as np
import jax
import jax.numpy as jnp
from jax import lax
from jax.experimental import pallas as pl
from jax.experimental.pallas import tpu as pltpu

F32 = jnp.float32
BF16 = jnp.bfloat16

D_MODEL = 1024
NEG_INF = -1e30
LN_EPS = 1e-5
RMS_EPS = 1e-6

MLA_HEADS = 4
MLA_Q_RANK = 256
MLA_KV_RANK = 128
MLA_NOPE = 64
MLA_ROPE = 32
MLA_V = 64
MLA_QK = MLA_NOPE + MLA_ROPE
ROPE_THETA = 10000.0

SB_HEADS = 4
SB_DIM = 64

SW_Q_HEADS = 8
SW_KV_HEADS = 2
SW_DIM = 64
SW_WINDOW = 128
SW_GROUP = SW_Q_HEADS // SW_KV_HEADS

DF_HEADS = 4
DF_DIM = 32
DF_VDIM = 2 * DF_DIM

A_WIDTH = MLA_HEADS * MLA_V
B_WIDTH = SB_HEADS * SB_DIM
C_WIDTH = SW_Q_HEADS * SW_DIM
D_WIDTH = DF_HEADS * DF_VDIM
N_BRANCH = 4

IN_SPLITS = (MLA_Q_RANK, MLA_KV_RANK, MLA_ROPE, 3 * B_WIDTH, C_WIDTH, 2 * SW_KV_HEADS * SW_DIM,
             2 * DF_HEADS * DF_DIM, 2 * DF_HEADS * DF_DIM, D_WIDTH)

N_EXPERTS = 32
N_GROUPS = 8
EXPERTS_PER_GROUP = N_EXPERTS // N_GROUPS
D_EXPERT = 256

LANES = 128

OFF_CQ = 0
OFF_CKV = OFF_CQ + MLA_Q_RANK
OFF_KR = OFF_CKV + MLA_KV_RANK
OFF_KRS = OFF_KR + LANES
OFF_REST = OFF_KRS + LANES
REST_WIDTH = sum(IN_SPLITS[3:])
IN_PAD_WIDTH = OFF_REST + REST_WIDTH

TOKEN_BLOCK = 256
ATT_BLOCK = 512
MOE_ROWS = 512
VMEM_LIMIT = 56 * 1024 * 1024


def _cparams(sem):
    return pltpu.CompilerParams(dimension_semantics=sem, vmem_limit_bytes=VMEM_LIMIT)


def _nt_dot(a, b, **kw):
    return lax.dot_general(a, b, (((1,), (1,)), ((), ())), preferred_element_type=F32, **kw)


def _const_spec(shape):
    nd = len(shape)
    return pl.BlockSpec(shape, lambda *_: (0,) * nd)


def _proj_kernel(x_ref, win_ref, qn_ref, kvn_ref, wuq_ref, wuqs_ref, wukv_ref,
                 cq_ref, sq_ref, ck_ref, sk_ref, rscale_ref,
                 hq_ref, kv_ref, kr_ref, hr_ref):
    xb = x_ref[...].astype(BF16)
    h = jnp.dot(xb, win_ref[...], preferred_element_type=F32)
    c_q = h[:, OFF_CQ:OFF_CQ + MLA_Q_RANK]
    c_kv = h[:, OFF_CKV:OFF_CKV + MLA_KV_RANK]
    nq = c_q * lax.rsqrt(jnp.mean(c_q * c_q, axis=-1, keepdims=True) + RMS_EPS) * qn_ref[...]
    nqb = nq.astype(BF16)
    q = jnp.dot(nqb, wuq_ref[...], preferred_element_type=F32)
    qs = jnp.dot(nqb, wuqs_ref[...], preferred_element_type=F32)
    hq_ref[...] = (q * cq_ref[...] + qs * sq_ref[...]).astype(hq_ref.dtype)
    nkv = c_kv * lax.rsqrt(jnp.mean(c_kv * c_kv, axis=-1, keepdims=True) + RMS_EPS) * kvn_ref[...]
    kv_ref[...] = jnp.dot(nkv.astype(BF16), wukv_ref[...], preferred_element_type=F32).astype(kv_ref.dtype)
    kr = h[:, OFF_KR:OFF_KR + LANES] * ck_ref[...] + h[:, OFF_KRS:OFF_KRS + LANES] * sk_ref[...]
    kr_ref[...] = kr.astype(kr_ref.dtype)
    hr_ref[...] = (h[:, OFF_REST:] * rscale_ref[...]).astype(hr_ref.dtype)


def _proj(x2, win, qn, kvn, wuq, wuqs, wukv, cq, sq, ck, sk, rscale):
    t = x2.shape[0]
    tm = TOKEN_BLOCK
    row = lambda w: pl.BlockSpec((tm, w), lambda i: (i, 0))
    qw = MLA_HEADS * MLA_QK
    kvw = MLA_HEADS * (MLA_NOPE + MLA_V)
    return pl.pallas_call(
        _proj_kernel,
        grid=(t // tm,),
        in_specs=[row(D_MODEL), _const_spec(win.shape), _const_spec(qn.shape), _const_spec(kvn.shape),
                  _const_spec(wuq.shape), _const_spec(wuqs.shape), _const_spec(wukv.shape),
                  row(qw), row(qw), row(LANES), row(LANES), _const_spec(rscale.shape)],
        out_specs=[row(qw), row(kvw), row(LANES), row(REST_WIDTH)],
        out_shape=[jax.ShapeDtypeStruct((t, qw), BF16), jax.ShapeDtypeStruct((t, kvw), BF16),
                   jax.ShapeDtypeStruct((t, LANES), BF16), jax.ShapeDtypeStruct((t, REST_WIDTH), BF16)],
        compiler_params=_cparams(("arbitrary",)),
        name="proj",
    )(x2, win, qn, kvn, wuq, wuqs, wukv, cq, sq, ck, sk, rscale)


def _causal_mask(n):
    row = lax.broadcasted_iota(jnp.int32, (n, n), 0)
    col = lax.broadcasted_iota(jnp.int32, (n, n), 1)
    return row, col


def _softmax_step(s, v, m, l, acc):
    m_new = jnp.maximum(m, jnp.max(s, axis=-1, keepdims=True))
    p = jnp.exp(s - m_new)
    alpha = jnp.exp(m - m_new)
    l_new = alpha * l + jnp.sum(p, axis=-1, keepdims=True)
    acc_new = alpha * acc + jnp.dot(p.astype(BF16), v, preferred_element_type=F32)
    return m_new, l_new, acc_new


def _mla_kernel(q_ref, k_ref, v_ref, o_ref, *, blk):
    qi = pl.program_id(2)
    q = q_ref[0, 0]

    def kv_block(ki):
        start = pl.multiple_of(ki * blk, blk)
        return k_ref[0, 0, pl.ds(start, blk), :], v_ref[0, 0, pl.ds(start, blk), :]

    def body(ki, carry):
        k, v = kv_block(ki)
        return _softmax_step(_nt_dot(q, k), v, *carry)

    init = (jnp.full((blk, 1), NEG_INF, F32), jnp.zeros((blk, 1), F32), jnp.zeros((blk, MLA_V), F32))
    carry = lax.fori_loop(0, qi, body, init)
    k, v = kv_block(qi)
    row, col = _causal_mask(blk)
    s = jnp.where(col <= row, _nt_dot(q, k), NEG_INF)
    m, l, acc = _softmax_step(s, v, *carry)
    o_ref[0, 0] = (acc / l).astype(o_ref.dtype)


def _head_spec(blk, d):
    return pl.BlockSpec((1, 1, blk, d), lambda b, h, i: (b, h, i, 0))


def _full_spec(s_len, d):
    return pl.BlockSpec((1, 1, s_len, d), lambda b, h, i: (b, h, 0, 0))


def _mla_attention(q, k, v):
    b, nh, s_len, _ = q.shape
    blk = ATT_BLOCK
    return pl.pallas_call(
        functools.partial(_mla_kernel, blk=blk),
        grid=(b, nh, s_len // blk),
        in_specs=[_head_spec(blk, MLA_QK), _full_spec(s_len, MLA_QK), _full_spec(s_len, MLA_V)],
        out_specs=_head_spec(blk, MLA_V),
        out_shape=jax.ShapeDtypeStruct((b, nh, s_len, MLA_V), BF16),
        compiler_params=_cparams(("arbitrary",) * 3),
        name="mla",
    )(q, k, v)


def _log_sigmoid(z):
    return jnp.minimum(z, 0.0) - jnp.log(1.0 + jnp.exp(-jnp.abs(z)))


def _suffix_sum(l1m, upper):
    hi = l1m.astype(BF16)
    lo = (l1m - hi.astype(F32)).astype(BF16)
    return (jnp.dot(hi, upper, preferred_element_type=F32) + jnp.dot(lo, upper, preferred_element_type=F32))


def _sb_kernel(q_ref, k_ref, v_ref, o_ref, *, blk):
    qi = pl.program_id(2)
    q = q_ref[0, 0]
    row, col = _causal_mask(blk)
    upper = (row > col).astype(BF16)

    def kv_block(ki):
        start = pl.multiple_of(ki * blk, blk)
        return k_ref[0, 0, pl.ds(start, blk), :], v_ref[0, 0, pl.ds(start, blk), :]

    k, v = kv_block(qi)
    z = _nt_dot(q, k)
    mask = col < row
    lb = _log_sigmoid(z)
    l1m = jnp.where(mask, lb - z, 0.0)
    suffix = _suffix_sum(l1m, upper)
    a = jnp.where(mask, jnp.exp(lb + suffix), 0.0)
    acc = jnp.dot(a.astype(BF16), v, preferred_element_type=F32)
    passed = suffix[:, 0:1] + l1m[:, 0:1]

    def body(j, carry):
        passed, acc = carry
        k, v = kv_block(qi - 1 - j)
        z = _nt_dot(q, k)
        lb = _log_sigmoid(z)
        l1m = lb - z
        suffix = _suffix_sum(l1m, upper)
        a = jnp.exp(lb + suffix + passed)
        acc = acc + jnp.dot(a.astype(BF16), v, preferred_element_type=F32)
        return passed + suffix[:, 0:1] + l1m[:, 0:1], acc

    _, acc = lax.fori_loop(0, qi, body, (passed, acc))
    o_ref[0, 0] = acc.astype(o_ref.dtype)


def _sb_attention(q, k, v):
    b, nh, s_len, d = q.shape
    blk = ATT_BLOCK
    return pl.pallas_call(
        functools.partial(_sb_kernel, blk=blk),
        grid=(b, nh, s_len // blk),
        in_specs=[_head_spec(blk, d), _full_spec(s_len, d), _full_spec(s_len, d)],
        out_specs=_head_spec(blk, d),
        out_shape=jax.ShapeDtypeStruct((b, nh, s_len, d), BF16),
        compiler_params=_cparams(("arbitrary",) * 3),
        name="sb",
    )(q, k, v)


def _swa_kernel(slopes_ref, sinks_ref, q_ref, kp_ref, kc_ref, vp_ref, vc_ref,
                pq_ref, pkp_ref, pkc_ref, o_ref):
    g = pl.program_id(1)
    i = pl.program_id(2)
    w = SW_WINDOW
    q = q_ref[0].reshape(SW_GROUP * w, SW_DIM)
    k = jnp.concatenate([kp_ref[0, 0], kc_ref[0, 0]], axis=0)
    v = jnp.concatenate([vp_ref[0, 0], vc_ref[0, 0]], axis=0)
    s = _nt_dot(q, k)
    pk = jnp.concatenate([pkp_ref[...], pkc_ref[...]], axis=1)
    dist = pq_ref[...] - pk
    qi = lax.broadcasted_iota(jnp.int32, (w, 2 * w), 0)
    kj = lax.broadcasted_iota(jnp.int32, (w, 2 * w), 1)
    mask = (kj > qi) & (kj <= qi + w) & ((kj >= w) | (i > 0))
    for j in range(SW_GROUP):
        head = g * SW_GROUP + j
        sj = jnp.where(mask, s[j * w:(j + 1) * w] - slopes_ref[head] * dist, NEG_INF)
        sink = sinks_ref[head]
        m = jnp.maximum(jnp.max(sj, axis=-1, keepdims=True), sink)
        p = jnp.exp(sj - m)
        denom = jnp.sum(p, axis=-1, keepdims=True) + jnp.exp(sink - m)
        o = jnp.dot(p.astype(BF16), v, preferred_element_type=F32) / denom
        o_ref[0, j] = o.astype(o_ref.dtype)


def _swa_attention(q, k, v, slopes, sinks, pos_col, pos_row):
    b, _, s_len, d = q.shape
    w = SW_WINDOW
    prev = lambda i: jnp.maximum(i - 1, 0)
    smem = pl.BlockSpec(memory_space=pltpu.SMEM)
    kv_prev = pl.BlockSpec((1, 1, w, d), lambda bb, g, i: (bb, g, prev(i), 0))
    kv_cur = pl.BlockSpec((1, 1, w, d), lambda bb, g, i: (bb, g, i, 0))
    return pl.pallas_call(
        _swa_kernel,
        grid=(b, SW_KV_HEADS, s_len // w),
        in_specs=[smem, smem,
                  pl.BlockSpec((1, SW_GROUP, w, d), lambda bb, g, i: (bb, g, i, 0)),
                  kv_prev, kv_cur, kv_prev, kv_cur,
                  pl.BlockSpec((w, 1), lambda bb, g, i: (i, 0)),
                  pl.BlockSpec((1, w), lambda bb, g, i: (0, prev(i))),
                  pl.BlockSpec((1, w), lambda bb, g, i: (0, i))],
        out_specs=pl.BlockSpec((1, SW_GROUP, w, d), lambda bb, g, i: (bb, g, i, 0)),
        out_shape=jax.ShapeDtypeStruct((b, SW_Q_HEADS, s_len, d), BF16),
        compiler_params=_cparams(("arbitrary",) * 3),
        name="swa",
    )(slopes, sinks, q, k, k, v, v, pos_col, pos_row, pos_row)


def _diff_kernel(slopes_ref, q_ref, k_ref, v_ref, pk_ref, lq1_ref, lk1_ref, lq2_ref, lk2_ref,
                 subln_ref, o_ref, *, blk, lam_init):
    h = pl.program_id(1)
    qi = pl.program_id(2)
    q = q_ref[0, 0]
    lane = lax.broadcasted_iota(jnp.int32, q.shape, 1)
    zero = jnp.zeros_like(q)
    qq = jnp.concatenate([jnp.where(lane < DF_DIM, q, zero), jnp.where(lane >= DF_DIM, q, zero)], axis=0)
    slope = slopes_ref[h]

    def block(ki):
        start = pl.multiple_of(ki * blk, blk)
        k = k_ref[0, 0, pl.ds(start, blk), :]
        v = v_ref[0, 0, pl.ds(start, blk), :]
        return _nt_dot(qq, k) + slope * pk_ref[:, pl.ds(start, blk)], v

    def body(ki, carry):
        s, v = block(ki)
        return _softmax_step(s, v, *carry)

    init = (jnp.full((2 * blk, 1), NEG_INF, F32), jnp.zeros((2 * blk, 1), F32),
            jnp.zeros((2 * blk, DF_VDIM), F32))
    carry = lax.fori_loop(0, qi, body, init)
    s, v = block(qi)
    row = lax.broadcasted_iota(jnp.int32, (2 * blk, blk), 0)
    col = lax.broadcasted_iota(jnp.int32, (2 * blk, blk), 1)
    causal = (col <= row) & ((row < blk) | (col <= row - blk))
    _, l, acc = _softmax_step(jnp.where(causal, s, NEG_INF), v, *carry)
    lam = (jnp.exp(jnp.sum(lq1_ref[...] * lk1_ref[...], keepdims=True))
           - jnp.exp(jnp.sum(lq2_ref[...] * lk2_ref[...], keepdims=True)) + lam_init)
    o = acc / l
    o = o[:blk] - lam * o[blk:]
    o = o * lax.rsqrt(jnp.mean(o * o, axis=-1, keepdims=True) + RMS_EPS) * subln_ref[...] * (1.0 - lam_init)
    o_ref[0, 0] = o.astype(o_ref.dtype)


def _diff_attention(q, k, v, slopes, pos_row, lq1, lk1, lq2, lk2, subln, lam_init):
    b, nh, s_len, d = q.shape
    blk = ATT_BLOCK
    smem = pl.BlockSpec(memory_space=pltpu.SMEM)
    vec = lambda n: pl.BlockSpec((1, n), lambda bb, h, i: (0, 0))
    return pl.pallas_call(
        functools.partial(_diff_kernel, blk=blk, lam_init=lam_init),
        grid=(b, nh, s_len // blk),
        in_specs=[smem, _head_spec(blk, d), _full_spec(s_len, d), _full_spec(s_len, DF_VDIM),
                  pl.BlockSpec((1, s_len), lambda bb, h, i: (0, 0)),
                  vec(DF_DIM), vec(DF_DIM), vec(DF_DIM), vec(DF_DIM), vec(DF_VDIM)],
        out_specs=_head_spec(blk, DF_VDIM),
        out_shape=jax.ShapeDtypeStruct((b, nh, s_len, DF_VDIM), BF16),
        compiler_params=_cparams(("arbitrary",) * 3),
        name="diff",
    )(slopes, q, k, v, pos_row, lq1, lk1, lq2, lk2, subln)


def _layer_norm(v, g, b):
    mu = jnp.mean(v, axis=-1, keepdims=True)
    c = v - mu
    var = jnp.mean(c * c, axis=-1, keepdims=True)
    return c * lax.rsqrt(var + LN_EPS) * g + b


def _sigmoid(z):
    return 1.0 / (1.0 + jnp.exp(-z))


def _route(logits_t, bias_col):
    scores = _sigmoid(logits_t)
    biased = scores + bias_col
    ng = N_GROUPS
    sc = [scores[j * ng:(j + 1) * ng] for j in range(EXPERTS_PER_GROUP)]
    bs = [biased[j * ng:(j + 1) * ng] for j in range(EXPERTS_PER_GROUP)]
    gscore = None
    for a in range(EXPERTS_PER_GROUP):
        for c in range(a + 1, EXPERTS_PER_GROUP):
            pair = bs[a] + bs[c]
            gscore = pair if gscore is None else jnp.maximum(gscore, pair)
    gid = lax.broadcasted_iota(jnp.int32, gscore.shape, 0)
    best = jnp.max(gscore, axis=0, keepdims=True)
    gsel = jnp.min(jnp.where(gscore == best, gid, ng), axis=0, keepdims=True)
    onehot = gid == gsel
    pick = lambda a: jnp.sum(jnp.where(onehot, a, 0.0), axis=0, keepdims=True)
    b_in = [pick(a) for a in bs]
    s_in = [pick(a) for a in sc]
    chosen = []
    for j in range(EXPERTS_PER_GROUP):
        ahead = jnp.zeros_like(b_in[j])
        for c in range(EXPERTS_PER_GROUP):
            if c == j:
                continue
            beats = (b_in[c] > b_in[j]) | ((b_in[c] == b_in[j]) & (c < j))
            ahead = ahead + beats.astype(F32)
        chosen.append(jnp.where(ahead < 2.0, s_in[j], 0.0))
    total = chosen[0] + chosen[1] + chosen[2] + chosen[3]
    gates = [c / total for c in chosen]
    return gates, onehot


def _merge_kernel(x_ref, oa_ref, ob_ref, oc_ref, od_ref, wg_ref, bg_ref, wa_ref, wb_ref, wc_ref, wd_ref,
                  wo_ref, g_ref, b_ref, wr_ref, rb_ref, x1_ref, gate_ref, *, alpha):
    tm = x_ref.shape[0]
    x = x_ref[...]
    xb = x.astype(BF16)
    merged = None
    for n, (o_ref, w_ref) in enumerate(((oa_ref, wa_ref), (ob_ref, wb_ref), (oc_ref, wc_ref), (od_ref, wd_ref))):
        cols = slice(n * D_MODEL, (n + 1) * D_MODEL)
        gate = _sigmoid(jnp.dot(xb, wg_ref[:, cols], preferred_element_type=F32) + bg_ref[:, cols])
        term = gate * jnp.dot(o_ref[...], w_ref[...], preferred_element_type=F32)
        merged = term if merged is None else merged + term
    mix = jnp.dot(merged.astype(BF16), wo_ref[...], preferred_element_type=F32)
    y = _layer_norm(alpha * x + mix, g_ref[...], b_ref[...])
    x1_ref[...] = y

    logits_t = _nt_dot(wr_ref[...], y, precision=lax.Precision.HIGHEST)
    gates, onehot = _route(logits_t, rb_ref[...])
    rows = [jnp.where(onehot, gate, 0.0) for gate in gates]
    rows.append(jnp.zeros((LANES - N_EXPERTS, tm), F32))
    gate_ref[...] = jnp.concatenate(rows, axis=0).T


def _merge(x2, oa, ob, oc, od, wg, bg, wa, wb, wc, wd, wo, g, b, wr_t, rb_col, alpha):
    t = x2.shape[0]
    tm = TOKEN_BLOCK
    row = lambda w: pl.BlockSpec((tm, w), lambda i: (i, 0))
    consts = (wg, bg, wa, wb, wc, wd, wo, g, b, wr_t, rb_col)
    return pl.pallas_call(
        functools.partial(_merge_kernel, alpha=alpha),
        grid=(t // tm,),
        in_specs=[row(D_MODEL), row(A_WIDTH), row(B_WIDTH), row(C_WIDTH), row(D_WIDTH)]
                 + [_const_spec(c.shape) for c in consts],
        out_specs=[row(D_MODEL), row(LANES)],
        out_shape=[jax.ShapeDtypeStruct((t, D_MODEL), F32), jax.ShapeDtypeStruct((t, LANES), F32)],
        compiler_params=_cparams(("arbitrary",)),
        name="merge",
    )(x2, oa, ob, oc, od, *consts)


def _moe_kernel(x_ref, gate_ref, w1_ref, w3_ref, w2_ref, g_ref, b_ref, o_ref, acc_ref, *, alpha):
    grp = pl.program_id(1)

    @pl.when(grp == 0)
    def _():
        acc_ref[...] = jnp.zeros_like(acc_ref)

    xb = x_ref[...].astype(BF16)
    gates = gate_ref[...]
    lane = lax.broadcasted_iota(jnp.int32, gates.shape, 1)
    hid = []
    for e in range(EXPERTS_PER_GROUP):
        h1 = jnp.dot(xb, w1_ref[e], preferred_element_type=F32)
        h3 = jnp.dot(xb, w3_ref[e], preferred_element_type=F32)
        gate = jnp.sum(jnp.where(lane == e * N_GROUPS + grp, gates, 0.0), axis=1, keepdims=True)
        hid.append((h1 * _sigmoid(h1) * h3 * gate).astype(BF16))
    w2 = w2_ref[...].reshape(EXPERTS_PER_GROUP * D_EXPERT, D_MODEL)
    acc_ref[...] += jnp.dot(jnp.concatenate(hid, axis=1), w2, preferred_element_type=F32)

    @pl.when(grp == N_GROUPS - 1)
    def _():
        o_ref[...] = _layer_norm(alpha * x_ref[...] + acc_ref[...], g_ref[...], b_ref[...])


def _moe(x1, gates, w1, w3, w2, g, b, alpha):
    t = x1.shape[0]
    tm = MOE_ROWS
    epg = EXPERTS_PER_GROUP
    return pl.pallas_call(
        functools.partial(_moe_kernel, alpha=alpha),
        grid=(t // tm, N_GROUPS),
        in_specs=[pl.BlockSpec((tm, D_MODEL), lambda i, e: (i, 0)),
                  pl.BlockSpec((tm, LANES), lambda i, e: (i, 0)),
                  pl.BlockSpec((epg, D_MODEL, D_EXPERT), lambda i, e: (e, 0, 0)),
                  pl.BlockSpec((epg, D_MODEL, D_EXPERT), lambda i, e: (e, 0, 0)),
                  pl.BlockSpec((epg, D_EXPERT, D_MODEL), lambda i, e: (e, 0, 0)),
                  pl.BlockSpec((1, D_MODEL), lambda i, e: (0, 0)),
                  pl.BlockSpec((1, D_MODEL), lambda i, e: (0, 0))],
        out_specs=pl.BlockSpec((tm, D_MODEL), lambda i, e: (i, 0)),
        out_shape=jax.ShapeDtypeStruct((t, D_MODEL), F32),
        scratch_shapes=[pltpu.VMEM((tm, D_MODEL), F32)],
        compiler_params=_cparams(("arbitrary", "arbitrary")),
        name="moe",
    )(x1, gates, w1, w3, w2, g, b)


def _alibi_slopes(n_heads):
    return 2.0 ** (-8.0 * jnp.arange(1, n_heads + 1, dtype=F32) / n_heads)


def _rope_tables(positions):
    half = MLA_ROPE // 2
    inv_freq = ROPE_THETA ** (-jnp.arange(half, dtype=F32) / half)
    ang = positions.astype(F32)[:, None] * inv_freq[None, :]
    cos = jnp.cos(ang)
    sin = jnp.sin(ang)
    return jnp.concatenate([cos, cos], axis=1), jnp.concatenate([sin, sin], axis=1)


def _rotate_half_columns(w):
    half = w.shape[-1] // 2
    return jnp.concatenate([-w[..., half:], w[..., :half]], axis=-1)


def _prep_w_in(w_in):
    split_at = [int(c) for c in np.cumsum(IN_SPLITS)[:-1]]
    parts = jnp.split(w_in, split_at, axis=-1)
    c_q, c_kv, k_rope = parts[0], parts[1], parts[2]
    pad = jnp.zeros((w_in.shape[0], LANES - MLA_ROPE), w_in.dtype)
    return jnp.concatenate([c_q, c_kv, k_rope, pad, _rotate_half_columns(k_rope), pad] + list(parts[3:]),
                           axis=-1).astype(BF16)


def _prep_w_uq(w_uq):
    w = w_uq.reshape(MLA_Q_RANK, MLA_HEADS, MLA_QK)
    rot = jnp.concatenate([jnp.zeros_like(w[..., :MLA_NOPE]), _rotate_half_columns(w[..., MLA_NOPE:])], axis=-1)
    return w_uq.astype(BF16), rot.reshape(w_uq.shape).astype(BF16)


def _heads(t2, b, s_len, nh, d):
    return t2.reshape(b, s_len, nh, d).transpose(0, 2, 1, 3)


def _unheads(t4):
    b, nh, s_len, d = t4.shape
    return t4.transpose(0, 2, 1, 3).reshape(b * s_len, nh * d)


def kernel(x, positions, w_in, mla_q_norm, mla_kv_norm, w_uq, w_ukv, sw_sinks, df_lq1, df_lk1, df_lq2, df_lk2,
           df_subln, w_gate, b_gate, w_br_a, w_br_b, w_br_c, w_br_d, w_out, ln1_g, ln1_b, w_router, router_bias,
           moe_w1, moe_w3, moe_w2, ln2_g, ln2_b):
    b, s_len, d = x.shape
    depth = w_in.shape[0]
    t = b * s_len
    alpha = (2 * depth) ** 0.25
    assert d == D_MODEL and t % TOKEN_BLOCK == 0 and s_len % ATT_BLOCK == 0 and t % MOE_ROWS == 0
    assert ATT_BLOCK % SW_WINDOW == 0

    posf = positions.astype(F32)
    pos_col = posf[:, None]
    pos_row = posf[None, :]
    cos, sin = _rope_tables(positions)
    q_scale = MLA_QK ** -0.5
    ones_n = jnp.ones((s_len, MLA_NOPE), F32)
    cq = jnp.tile(jnp.concatenate([ones_n, cos], axis=1) * q_scale, (b, MLA_HEADS))
    sq = jnp.tile(jnp.concatenate([0.0 * ones_n, sin], axis=1) * q_scale, (b, MLA_HEADS))
    lane_pad = jnp.zeros((s_len, LANES - MLA_ROPE), F32)
    ck = jnp.tile(jnp.concatenate([cos, lane_pad], axis=1), (b, 1))
    sk = jnp.tile(jnp.concatenate([sin, lane_pad], axis=1), (b, 1))
    rscale = jnp.concatenate([
        jnp.full((B_WIDTH,), SB_DIM ** -0.5, F32), jnp.ones((2 * B_WIDTH,), F32),
        jnp.full((C_WIDTH,), SW_DIM ** -0.5, F32), jnp.ones((2 * SW_KV_HEADS * SW_DIM,), F32),
        jnp.full((2 * DF_HEADS * DF_DIM,), DF_DIM ** -0.5, F32), jnp.ones((2 * DF_HEADS * DF_DIM + D_WIDTH,), F32),
    ])[None, :]
    sw_slopes = _alibi_slopes(SW_Q_HEADS)
    df_slopes = _alibi_slopes(DF_HEADS)

    perm = np.arange(N_EXPERTS).reshape(N_GROUPS, EXPERTS_PER_GROUP).T.reshape(-1)
    wr_t = w_router.astype(F32).T[perm]
    rb_col = router_bias.astype(F32)[perm][:, None]

    x2 = x.reshape(t, d)
    for l in range(depth):
        lam_init = 0.8 - 0.6 * math.exp(-0.3 * l)
        wuq, wuqs = _prep_w_uq(w_uq[l])
        hq, kv, kr, hr = _proj(x2, _prep_w_in(w_in[l]), mla_q_norm[l][None, :], mla_kv_norm[l][None, :],
                               wuq, wuqs, w_ukv[l].astype(BF16), cq, sq, ck, sk, rscale)

        kv4 = kv.reshape(b, s_len, MLA_HEADS, MLA_NOPE + MLA_V)
        kr4 = jnp.broadcast_to(kr[:, :MLA_ROPE].reshape(b, s_len, 1, MLA_ROPE), (b, s_len, MLA_HEADS, MLA_ROPE))
        mla_q = _heads(hq, b, s_len, MLA_HEADS, MLA_QK)
        mla_k = jnp.concatenate([kv4[..., :MLA_NOPE], kr4], axis=-1).transpose(0, 2, 1, 3)
        mla_v = kv4[..., MLA_NOPE:].transpose(0, 2, 1, 3)
        o = 0
        sb_q = _heads(hr[:, o:o + B_WIDTH], b, s_len, SB_HEADS, SB_DIM); o += B_WIDTH
        sb_k = _heads(hr[:, o:o + B_WIDTH], b, s_len, SB_HEADS, SB_DIM); o += B_WIDTH
        sb_v = _heads(hr[:, o:o + B_WIDTH], b, s_len, SB_HEADS, SB_DIM); o += B_WIDTH
        sw_q = _heads(hr[:, o:o + C_WIDTH], b, s_len, SW_Q_HEADS, SW_DIM); o += C_WIDTH
        kvw = SW_KV_HEADS * SW_DIM
        sw_k = _heads(hr[:, o:o + kvw], b, s_len, SW_KV_HEADS, SW_DIM); o += kvw
        sw_v = _heads(hr[:, o:o + kvw], b, s_len, SW_KV_HEADS, SW_DIM); o += kvw
        dfw = 2 * DF_HEADS * DF_DIM
        df_q = _heads(hr[:, o:o + dfw], b, s_len, DF_HEADS, 2 * DF_DIM); o += dfw
        df_k = _heads(hr[:, o:o + dfw], b, s_len, DF_HEADS, 2 * DF_DIM); o += dfw
        df_v = _heads(hr[:, o:o + D_WIDTH], b, s_len, DF_HEADS, DF_VDIM)

        o_a = _unheads(_mla_attention(mla_q, mla_k, mla_v))
        o_b = _unheads(_sb_attention(sb_q, sb_k, sb_v))
        o_c = _unheads(_swa_attention(sw_q, sw_k, sw_v, sw_slopes, sw_sinks[l].astype(F32), pos_col, pos_row))
        o_d = _unheads(_diff_attention(df_q, df_k, df_v, df_slopes, pos_row,
                                       df_lq1[l][None, :], df_lk1[l][None, :], df_lq2[l][None, :],
                                       df_lk2[l][None, :], df_subln[l][None, :], lam_init))

        x1, gates = _merge(x2, o_a, o_b, o_c, o_d, w_gate[l].astype(BF16), b_gate[l][None, :],
                           w_br_a[l].astype(BF16), w_br_b[l].astype(BF16), w_br_c[l].astype(BF16),
                           w_br_d[l].astype(BF16), w_out[l].astype(BF16), ln1_g[l][None, :], ln1_b[l][None, :],
                           wr_t, rb_col, alpha)
        x2 = _moe(x1, gates, moe_w1[l].astype(BF16), moe_w3[l].astype(BF16), moe_w2[l].astype(BF16),
                  ln2_g[l][None, :], ln2_b[l][None, :], alpha)
    return x2.reshape(b, s_len, d)
```

```python
import functools
import math

import numpy as np
import jax
import jax.numpy as jnp
from jax import lax
from jax.experimental import pallas as pl
from jax.experimental.pallas import tpu as pltpu

F32 = jnp.float32
BF16 = jnp.bfloat16

D_MODEL = 1024
NEG_INF = -1e30
LN_EPS = 1e-5
RMS_EPS = 1e-6

MLA_HEADS = 4
MLA_Q_RANK = 256
MLA_KV_RANK = 128
MLA_NOPE = 64
MLA_ROPE = 32
MLA_V = 64
MLA_QK = MLA_NOPE + MLA_ROPE
ROPE_THETA = 10000.0

SB_HEADS = 4
SB_DIM = 64

SW_Q_HEADS = 8
SW_KV_HEADS = 2
SW_DIM = 64
SW_WINDOW = 128
SW_GROUP = SW_Q_HEADS // SW_KV_HEADS

DF_HEADS = 4
DF_DIM = 32
DF_VDIM = 2 * DF_DIM

A_WIDTH = MLA_HEADS * MLA_V
B_WIDTH = SB_HEADS * SB_DIM
C_WIDTH = SW_Q_HEADS * SW_DIM
D_WIDTH = DF_HEADS * DF_VDIM
N_BRANCH = 4

IN_SPLITS = (MLA_Q_RANK, MLA_KV_RANK, MLA_ROPE, 3 * B_WIDTH, C_WIDTH, 2 * SW_KV_HEADS * SW_DIM,
             2 * DF_HEADS * DF_DIM, 2 * DF_HEADS * DF_DIM, D_WIDTH)

N_EXPERTS = 32
N_GROUPS = 8
EXPERTS_PER_GROUP = N_EXPERTS // N_GROUPS
D_EXPERT = 256

LANES = 128

OFF_CQ = 0
OFF_CKV = OFF_CQ + MLA_Q_RANK
OFF_KR = OFF_CKV + MLA_KV_RANK
OFF_KRS = OFF_KR + LANES
OFF_REST = OFF_KRS + LANES
REST_WIDTH = sum(IN_SPLITS[3:])
IN_PAD_WIDTH = OFF_REST + REST_WIDTH

TOKEN_BLOCK = 256
ATT_BLOCK = 512
SUFFIX_BLOCK = 256
EXP_UNDERFLOW = -104.0
MOE_ROWS = 512
VMEM_LIMIT = 56 * 1024 * 1024


def _cparams(sem):
    return pltpu.CompilerParams(dimension_semantics=sem, vmem_limit_bytes=VMEM_LIMIT)


def _nt_dot(a, b, **kw):
    return lax.dot_general(a, b, (((1,), (1,)), ((), ())), preferred_element_type=F32, **kw)


def _const_spec(shape):
    nd = len(shape)
    return pl.BlockSpec(shape, lambda *_: (0,) * nd)


def _proj_kernel(x_ref, win_ref, qn_ref, kvn_ref, wuq_ref, wuqs_ref, wukv_ref,
                 cq_ref, sq_ref, ck_ref, sk_ref, rscale_ref,
                 hq_ref, kv_ref, kr_ref, hr_ref):
    xb = x_ref[...].astype(BF16)
    h = jnp.dot(xb, win_ref[...], preferred_element_type=F32)
    c_q = h[:, OFF_CQ:OFF_CQ + MLA_Q_RANK]
    c_kv = h[:, OFF_CKV:OFF_CKV + MLA_KV_RANK]
    nq = c_q * lax.rsqrt(jnp.mean(c_q * c_q, axis=-1, keepdims=True) + RMS_EPS) * qn_ref[...]
    nqb = nq.astype(BF16)
    q = jnp.dot(nqb, wuq_ref[...], preferred_element_type=F32)
    qs = jnp.dot(nqb, wuqs_ref[...], preferred_element_type=F32)
    hq_ref[...] = (q * cq_ref[...] + qs * sq_ref[...]).astype(hq_ref.dtype)
    nkv = c_kv * lax.rsqrt(jnp.mean(c_kv * c_kv, axis=-1, keepdims=True) + RMS_EPS) * kvn_ref[...]
    kv_ref[...] = jnp.dot(nkv.astype(BF16), wukv_ref[...], preferred_element_type=F32).astype(kv_ref.dtype)
    kr = h[:, OFF_KR:OFF_KR + LANES] * ck_ref[...] + h[:, OFF_KRS:OFF_KRS + LANES] * sk_ref[...]
    kr_ref[...] = kr.astype(kr_ref.dtype)
    hr_ref[...] = (h[:, OFF_REST:] * rscale_ref[...]).astype(hr_ref.dtype)


def _proj(x2, win, qn, kvn, wuq, wuqs, wukv, cq, sq, ck, sk, rscale):
    t = x2.shape[0]
    tm = TOKEN_BLOCK
    row = lambda w: pl.BlockSpec((tm, w), lambda i: (i, 0))
    qw = MLA_HEADS * MLA_QK
    kvw = MLA_HEADS * (MLA_NOPE + MLA_V)
    return pl.pallas_call(
        _proj_kernel,
        grid=(t // tm,),
        in_specs=[row(D_MODEL), _const_spec(win.shape), _const_spec(qn.shape), _const_spec(kvn.shape),
                  _const_spec(wuq.shape), _const_spec(wuqs.shape), _const_spec(wukv.shape),
                  row(qw), row(qw), row(LANES), row(LANES), _const_spec(rscale.shape)],
        out_specs=[row(qw), row(kvw), row(LANES), row(REST_WIDTH)],
        out_shape=[jax.ShapeDtypeStruct((t, qw), BF16), jax.ShapeDtypeStruct((t, kvw), BF16),
                   jax.ShapeDtypeStruct((t, LANES), BF16), jax.ShapeDtypeStruct((t, REST_WIDTH), BF16)],
        compiler_params=_cparams(("arbitrary",)),
        name="proj",
    )(x2, win, qn, kvn, wuq, wuqs, wukv, cq, sq, ck, sk, rscale)


def _causal_mask(n):
    row = lax.broadcasted_iota(jnp.int32, (n, n), 0)
    col = lax.broadcasted_iota(jnp.int32, (n, n), 1)
    return row, col


def _softmax_step(s, v, m, l, acc):
    m_new = jnp.maximum(m, jnp.max(s, axis=-1, keepdims=True))
    p = jnp.exp(s - m_new)
    alpha = jnp.exp(m - m_new)
    l_new = alpha * l + jnp.sum(p, axis=-1, keepdims=True)
    acc_new = alpha * acc + jnp.dot(p.astype(BF16), v, preferred_element_type=F32)
    return m_new, l_new, acc_new


def _mla_kernel(q_ref, k_ref, v_ref, o_ref, *, blk):
    qi = pl.program_id(2)
    q = q_ref[0, 0]

    def kv_block(ki):
        start = pl.multiple_of(ki * blk, blk)
        return k_ref[0, 0, pl.ds(start, blk), :], v_ref[0, 0, pl.ds(start, blk), :]

    def body(ki, carry):
        k, v = kv_block(ki)
        return _softmax_step(_nt_dot(q, k), v, *carry)

    init = (jnp.full((blk, 1), NEG_INF, F32), jnp.zeros((blk, 1), F32), jnp.zeros((blk, MLA_V), F32))
    carry = lax.fori_loop(0, qi, body, init)
    k, v = kv_block(qi)
    row, col = _causal_mask(blk)
    s = jnp.where(col <= row, _nt_dot(q, k), NEG_INF)
    m, l, acc = _softmax_step(s, v, *carry)
    o_ref[0, 0] = (acc / l).astype(o_ref.dtype)


def _head_spec(blk, d):
    return pl.BlockSpec((1, 1, blk, d), lambda b, h, i: (b, h, i, 0))


def _full_spec(s_len, d):
    return pl.BlockSpec((1, 1, s_len, d), lambda b, h, i: (b, h, 0, 0))


def _mla_attention(q, k, v):
    b, nh, s_len, _ = q.shape
    blk = ATT_BLOCK
    return pl.pallas_call(
        functools.partial(_mla_kernel, blk=blk),
        grid=(b, nh, s_len // blk),
        in_specs=[_head_spec(blk, MLA_QK), _full_spec(s_len, MLA_QK), _full_spec(s_len, MLA_V)],
        out_specs=_head_spec(blk, MLA_V),
        out_shape=jax.ShapeDtypeStruct((b, nh, s_len, MLA_V), BF16),
        compiler_params=_cparams(("arbitrary",) * 3),
        name="mla",
    )(q, k, v)


def _log_sigmoid(z):
    return jnp.minimum(z, 0.0) - jnp.log(1.0 + jnp.exp(-jnp.abs(z)))


def _suffix_sum(l1m, upper):
    sub = upper.shape[0]
    n_sub = l1m.shape[1] // sub
    pieces = [None] * n_sub
    right = None
    for c in reversed(range(n_sub)):
        chunk = l1m[:, c * sub:(c + 1) * sub]
        hi = chunk.astype(BF16)
        lo = (chunk - hi.astype(F32)).astype(BF16)
        inner = jnp.dot(hi, upper, preferred_element_type=F32) + jnp.dot(lo, upper, preferred_element_type=F32)
        total = inner[:, 0:1] + chunk[:, 0:1]
        if right is None:
            pieces[c], right = inner, total
        else:
            pieces[c], right = inner + right, right + total
    return jnp.concatenate(pieces, axis=1), right


def _sb_kernel(q_ref, k_ref, v_ref, o_ref, *, blk):
    qi = pl.program_id(2)
    q = q_ref[0, 0]
    srow, scol = _causal_mask(SUFFIX_BLOCK)
    upper = (srow > scol).astype(BF16)

    def kv_block(ki):
        start = pl.multiple_of(ki * blk, blk)
        return k_ref[0, 0, pl.ds(start, blk), :], v_ref[0, 0, pl.ds(start, blk), :]

    row, col = _causal_mask(blk)
    k, v = kv_block(qi)
    z = _nt_dot(q, k)
    mask = col < row
    lb = _log_sigmoid(z)
    l1m = jnp.where(mask, lb - z, 0.0)
    suffix, passed = _suffix_sum(l1m, upper)
    a = jnp.where(mask, jnp.exp(lb + suffix), 0.0)
    acc = jnp.dot(a.astype(BF16), v, preferred_element_type=F32)

    def live(carry):
        j, passed, _ = carry
        return (j < qi) & (jnp.max(passed) >= EXP_UNDERFLOW)

    def body(carry):
        j, passed, acc = carry
        k, v = kv_block(qi - 1 - j)
        z = _nt_dot(q, k)
        lb = _log_sigmoid(z)
        l1m = lb - z
        suffix, total = _suffix_sum(l1m, upper)
        a = jnp.exp(lb + suffix + passed)
        acc = acc + jnp.dot(a.astype(BF16), v, preferred_element_type=F32)
        return j + 1, passed + total, acc

    _, _, acc = lax.while_loop(live, body, (jnp.int32(0), passed, acc))
    o_ref[0, 0] = acc.astype(o_ref.dtype)


def _sb_attention(q, k, v):
    b, nh, s_len, d = q.shape
    blk = ATT_BLOCK
    return pl.pallas_call(
        functools.partial(_sb_kernel, blk=blk),
        grid=(b, nh, s_len // blk),
        in_specs=[_head_spec(blk, d), _full_spec(s_len, d), _full_spec(s_len, d)],
        out_specs=_head_spec(blk, d),
        out_shape=jax.ShapeDtypeStruct((b, nh, s_len, d), BF16),
        compiler_params=_cparams(("arbitrary",) * 3),
        name="sb",
    )(q, k, v)


def _swa_kernel(slopes_ref, sinks_ref, q_ref, kp_ref, kc_ref, vp_ref, vc_ref,
                pq_ref, pkp_ref, pkc_ref, o_ref):
    g = pl.program_id(1)
    i = pl.program_id(2)
    w = SW_WINDOW
    q = q_ref[0].reshape(SW_GROUP * w, SW_DIM)
    k = jnp.concatenate([kp_ref[0, 0], kc_ref[0, 0]], axis=0)
    v = jnp.concatenate([vp_ref[0, 0], vc_ref[0, 0]], axis=0)
    s = _nt_dot(q, k)
    pk = jnp.concatenate([pkp_ref[...], pkc_ref[...]], axis=1)
    dist = pq_ref[...] - pk
    qi = lax.broadcasted_iota(jnp.int32, (w, 2 * w), 0)
    kj = lax.broadcasted_iota(jnp.int32, (w, 2 * w), 1)
    mask = (kj > qi) & (kj <= qi + w) & ((kj >= w) | (i > 0))
    for j in range(SW_GROUP):
        head = g * SW_GROUP + j
        sj = jnp.where(mask, s[j * w:(j + 1) * w] - slopes_ref[head] * dist, NEG_INF)
        sink = sinks_ref[head]
        m = jnp.maximum(jnp.max(sj, axis=-1, keepdims=True), sink)
        p = jnp.exp(sj - m)
        denom = jnp.sum(p, axis=-1, keepdims=True) + jnp.exp(sink - m)
        o = jnp.dot(p.astype(BF16), v, preferred_element_type=F32) / denom
        o_ref[0, j] = o.astype(o_ref.dtype)


def _swa_attention(q, k, v, slopes, sinks, pos_col, pos_row):
    b, _, s_len, d = q.shape
    w = SW_WINDOW
    prev = lambda i: jnp.maximum(i - 1, 0)
    smem = pl.BlockSpec(memory_space=pltpu.SMEM)
    kv_prev = pl.BlockSpec((1, 1, w, d), lambda bb, g, i: (bb, g, prev(i), 0))
    kv_cur = pl.BlockSpec((1, 1, w, d), lambda bb, g, i: (bb, g, i, 0))
    return pl.pallas_call(
        _swa_kernel,
        grid=(b, SW_KV_HEADS, s_len // w),
        in_specs=[smem, smem,
                  pl.BlockSpec((1, SW_GROUP, w, d), lambda bb, g, i: (bb, g, i, 0)),
                  kv_prev, kv_cur, kv_prev, kv_cur,
                  pl.BlockSpec((w, 1), lambda bb, g, i: (i, 0)),
                  pl.BlockSpec((1, w), lambda bb, g, i: (0, prev(i))),
                  pl.BlockSpec((1, w), lambda bb, g, i: (0, i))],
        out_specs=pl.BlockSpec((1, SW_GROUP, w, d), lambda bb, g, i: (bb, g, i, 0)),
        out_shape=jax.ShapeDtypeStruct((b, SW_Q_HEADS, s_len, d), BF16),
        compiler_params=_cparams(("arbitrary",) * 3),
        name="swa",
    )(slopes, sinks, q, k, k, v, v, pos_col, pos_row, pos_row)


def _diff_kernel(slopes_ref, q_ref, k_ref, v_ref, pk_ref, lq1_ref, lk1_ref, lq2_ref, lk2_ref,
                 subln_ref, o_ref, *, blk, lam_init):
    h = pl.program_id(1)
    qi = pl.program_id(2)
    q = q_ref[0, 0]
    lane = lax.broadcasted_iota(jnp.int32, q.shape, 1)
    zero = jnp.zeros_like(q)
    qq = jnp.concatenate([jnp.where(lane < DF_DIM, q, zero), jnp.where(lane >= DF_DIM, q, zero)], axis=0)
    slope = slopes_ref[h]

    def block(ki):
        start = pl.multiple_of(ki * blk, blk)
        k = k_ref[0, 0, pl.ds(start, blk), :]
        v = v_ref[0, 0, pl.ds(start, blk), :]
        return _nt_dot(qq, k) + slope * pk_ref[:, pl.ds(start, blk)], v

    def body(ki, carry):
        s, v = block(ki)
        return _softmax_step(s, v, *carry)

    init = (jnp.full((2 * blk, 1), NEG_INF, F32), jnp.zeros((2 * blk, 1), F32),
            jnp.zeros((2 * blk, DF_VDIM), F32))
    carry = lax.fori_loop(0, qi, body, init)
    s, v = block(qi)
    row = lax.broadcasted_iota(jnp.int32, (2 * blk, blk), 0)
    col = lax.broadcasted_iota(jnp.int32, (2 * blk, blk), 1)
    causal = (col <= row) & ((row < blk) | (col <= row - blk))
    _, l, acc = _softmax_step(jnp.where(causal, s, NEG_INF), v, *carry)
    lam = (jnp.exp(jnp.sum(lq1_ref[...] * lk1_ref[...], keepdims=True))
           - jnp.exp(jnp.sum(lq2_ref[...] * lk2_ref[...], keepdims=True)) + lam_init)
    o = acc / l
    o = o[:blk] - lam * o[blk:]
    o = o * lax.rsqrt(jnp.mean(o * o, axis=-1, keepdims=True) + RMS_EPS) * subln_ref[...] * (1.0 - lam_init)
    o_ref[0, 0] = o.astype(o_ref.dtype)


def _diff_attention(q, k, v, slopes, pos_row, lq1, lk1, lq2, lk2, subln, lam_init):
    b, nh, s_len, d = q.shape
    blk = ATT_BLOCK
    smem = pl.BlockSpec(memory_space=pltpu.SMEM)
    vec = lambda n: pl.BlockSpec((1, n), lambda bb, h, i: (0, 0))
    return pl.pallas_call(
        functools.partial(_diff_kernel, blk=blk, lam_init=lam_init),
        grid=(b, nh, s_len // blk),
        in_specs=[smem, _head_spec(blk, d), _full_spec(s_len, d), _full_spec(s_len, DF_VDIM),
                  pl.BlockSpec((1, s_len), lambda bb, h, i: (0, 0)),
                  vec(DF_DIM), vec(DF_DIM), vec(DF_DIM), vec(DF_DIM), vec(DF_VDIM)],
        out_specs=_head_spec(blk, DF_VDIM),
        out_shape=jax.ShapeDtypeStruct((b, nh, s_len, DF_VDIM), BF16),
        compiler_params=_cparams(("arbitrary",) * 3),
        name="diff",
    )(slopes, q, k, v, pos_row, lq1, lk1, lq2, lk2, subln)


def _layer_norm(v, g, b):
    mu = jnp.mean(v, axis=-1, keepdims=True)
    c = v - mu
    var = jnp.mean(c * c, axis=-1, keepdims=True)
    return c * lax.rsqrt(var + LN_EPS) * g + b


def _sigmoid(z):
    return 1.0 / (1.0 + jnp.exp(-z))


def _route(logits_t, bias_col):
    scores = _sigmoid(logits_t)
    biased = scores + bias_col
    ng = N_GROUPS
    sc = [scores[j * ng:(j + 1) * ng] for j in range(EXPERTS_PER_GROUP)]
    bs = [biased[j * ng:(j + 1) * ng] for j in range(EXPERTS_PER_GROUP)]
    gscore = None
    for a in range(EXPERTS_PER_GROUP):
        for c in range(a + 1, EXPERTS_PER_GROUP):
            pair = bs[a] + bs[c]
            gscore = pair if gscore is None else jnp.maximum(gscore, pair)
    gid = lax.broadcasted_iota(jnp.int32, gscore.shape, 0)
    best = jnp.max(gscore, axis=0, keepdims=True)
    gsel = jnp.min(jnp.where(gscore == best, gid, ng), axis=0, keepdims=True)
    onehot = gid == gsel
    pick = lambda a: jnp.sum(jnp.where(onehot, a, 0.0), axis=0, keepdims=True)
    b_in = [pick(a) for a in bs]
    s_in = [pick(a) for a in sc]
    chosen = []
    for j in range(EXPERTS_PER_GROUP):
        ahead = jnp.zeros_like(b_in[j])
        for c in range(EXPERTS_PER_GROUP):
            if c == j:
                continue
            beats = (b_in[c] > b_in[j]) | ((b_in[c] == b_in[j]) & (c < j))
            ahead = ahead + beats.astype(F32)
        chosen.append(jnp.where(ahead < 2.0, s_in[j], 0.0))
    total = chosen[0] + chosen[1] + chosen[2] + chosen[3]
    gates = [c / total for c in chosen]
    return gates, onehot


def _merge_kernel(x_ref, oa_ref, ob_ref, oc_ref, od_ref, wg_ref, bg_ref, wa_ref, wb_ref, wc_ref, wd_ref,
                  wo_ref, g_ref, b_ref, wr_ref, rb_ref, x1_ref, gate_ref, *, alpha):
    tm = x_ref.shape[0]
    x = x_ref[...]
    xb = x.astype(BF16)
    merged = None
    for n, (o_ref, w_ref) in enumerate(((oa_ref, wa_ref), (ob_ref, wb_ref), (oc_ref, wc_ref), (od_ref, wd_ref))):
        cols = slice(n * D_MODEL, (n + 1) * D_MODEL)
        gate = _sigmoid(jnp.dot(xb, wg_ref[:, cols], preferred_element_type=F32) + bg_ref[:, cols])
        term = gate * jnp.dot(o_ref[...], w_ref[...], preferred_element_type=F32)
        merged = term if merged is None else merged + term
    mix = jnp.dot(merged.astype(BF16), wo_ref[...], preferred_element_type=F32)
    y = _layer_norm(alpha * x + mix, g_ref[...], b_ref[...])
    x1_ref[...] = y

    logits_t = _nt_dot(wr_ref[...], y, precision=lax.Precision.HIGHEST)
    gates, onehot = _route(logits_t, rb_ref[...])
    rows = [jnp.where(onehot, gate, 0.0) for gate in gates]
    rows.append(jnp.zeros((LANES - N_EXPERTS, tm), F32))
    gate_ref[...] = jnp.concatenate(rows, axis=0).T


def _merge(x2, oa, ob, oc, od, wg, bg, wa, wb, wc, wd, wo, g, b, wr_t, rb_col, alpha):
    t = x2.shape[0]
    tm = TOKEN_BLOCK
    row = lambda w: pl.BlockSpec((tm, w), lambda i: (i, 0))
    consts = (wg, bg, wa, wb, wc, wd, wo, g, b, wr_t, rb_col)
    return pl.pallas_call(
        functools.partial(_merge_kernel, alpha=alpha),
        grid=(t // tm,),
        in_specs=[row(D_MODEL), row(A_WIDTH), row(B_WIDTH), row(C_WIDTH), row(D_WIDTH)]
                 + [_const_spec(c.shape) for c in consts],
        out_specs=[row(D_MODEL), row(LANES)],
        out_shape=[jax.ShapeDtypeStruct((t, D_MODEL), F32), jax.ShapeDtypeStruct((t, LANES), F32)],
        compiler_params=_cparams(("arbitrary",)),
        name="merge",
    )(x2, oa, ob, oc, od, *consts)


def _moe_kernel(x_ref, gate_ref, w1_ref, w3_ref, w2_ref, g_ref, b_ref, o_ref, acc_ref, *, alpha):
    grp = pl.program_id(1)

    @pl.when(grp == 0)
    def _():
        acc_ref[...] = jnp.zeros_like(acc_ref)

    xb = x_ref[...].astype(BF16)
    gates = gate_ref[...]
    lane = lax.broadcasted_iota(jnp.int32, gates.shape, 1)
    hid = []
    for e in range(EXPERTS_PER_GROUP):
        h1 = jnp.dot(xb, w1_ref[e], preferred_element_type=F32)
        h3 = jnp.dot(xb, w3_ref[e], preferred_element_type=F32)
        gate = jnp.sum(jnp.where(lane == e * N_GROUPS + grp, gates, 0.0), axis=1, keepdims=True)
        hid.append((h1 * _sigmoid(h1) * h3 * gate).astype(BF16))
    w2 = w2_ref[...].reshape(EXPERTS_PER_GROUP * D_EXPERT, D_MODEL)
    acc_ref[...] += jnp.dot(jnp.concatenate(hid, axis=1), w2, preferred_element_type=F32)

    @pl.when(grp == N_GROUPS - 1)
    def _():
        o_ref[...] = _layer_norm(alpha * x_ref[...] + acc_ref[...], g_ref[...], b_ref[...])


def _moe(x1, gates, w1, w3, w2, g, b, alpha):
    t = x1.shape[0]
    tm = MOE_ROWS
    epg = EXPERTS_PER_GROUP
    return pl.pallas_call(
        functools.partial(_moe_kernel, alpha=alpha),
        grid=(t // tm, N_GROUPS),
        in_specs=[pl.BlockSpec((tm, D_MODEL), lambda i, e: (i, 0)),
                  pl.BlockSpec((tm, LANES), lambda i, e: (i, 0)),
                  pl.BlockSpec((epg, D_MODEL, D_EXPERT), lambda i, e: (e, 0, 0)),
                  pl.BlockSpec((epg, D_MODEL, D_EXPERT), lambda i, e: (e, 0, 0)),
                  pl.BlockSpec((epg, D_EXPERT, D_MODEL), lambda i, e: (e, 0, 0)),
                  pl.BlockSpec((1, D_MODEL), lambda i, e: (0, 0)),
                  pl.BlockSpec((1, D_MODEL), lambda i, e: (0, 0))],
        out_specs=pl.BlockSpec((tm, D_MODEL), lambda i, e: (i, 0)),
        out_shape=jax.ShapeDtypeStruct((t, D_MODEL), F32),
        scratch_shapes=[pltpu.VMEM((tm, D_MODEL), F32)],
        compiler_params=_cparams(("arbitrary", "arbitrary")),
        name="moe",
    )(x1, gates, w1, w3, w2, g, b)


def _alibi_slopes(n_heads):
    return 2.0 ** (-8.0 * jnp.arange(1, n_heads + 1, dtype=F32) / n_heads)


def _rope_tables(positions):
    half = MLA_ROPE // 2
    inv_freq = ROPE_THETA ** (-jnp.arange(half, dtype=F32) / half)
    ang = positions.astype(F32)[:, None] * inv_freq[None, :]
    cos = jnp.cos(ang)
    sin = jnp.sin(ang)
    return jnp.concatenate([cos, cos], axis=1), jnp.concatenate([sin, sin], axis=1)


def _rotate_half_columns(w):
    half = w.shape[-1] // 2
    return jnp.concatenate([-w[..., half:], w[..., :half]], axis=-1)


def _prep_w_in(w_in):
    split_at = [int(c) for c in np.cumsum(IN_SPLITS)[:-1]]
    parts = jnp.split(w_in, split_at, axis=-1)
    c_q, c_kv, k_rope = parts[0], parts[1], parts[2]
    pad = jnp.zeros((w_in.shape[0], LANES - MLA_ROPE), w_in.dtype)
    return jnp.concatenate([c_q, c_kv, k_rope, pad, _rotate_half_columns(k_rope), pad] + list(parts[3:]),
                           axis=-1).astype(BF16)


def _prep_w_uq(w_uq):
    w = w_uq.reshape(MLA_Q_RANK, MLA_HEADS, MLA_QK)
    rot = jnp.concatenate([jnp.zeros_like(w[..., :MLA_NOPE]), _rotate_half_columns(w[..., MLA_NOPE:])], axis=-1)
    return w_uq.astype(BF16), rot.reshape(w_uq.shape).astype(BF16)


def _heads(t2, b, s_len, nh, d):
    return t2.reshape(b, s_len, nh, d).transpose(0, 2, 1, 3)


def _unheads(t4):
    b, nh, s_len, d = t4.shape
    return t4.transpose(0, 2, 1, 3).reshape(b * s_len, nh * d)


def kernel(x, positions, w_in, mla_q_norm, mla_kv_norm, w_uq, w_ukv, sw_sinks, df_lq1, df_lk1, df_lq2, df_lk2,
           df_subln, w_gate, b_gate, w_br_a, w_br_b, w_br_c, w_br_d, w_out, ln1_g, ln1_b, w_router, router_bias,
           moe_w1, moe_w3, moe_w2, ln2_g, ln2_b):
    b, s_len, d = x.shape
    depth = w_in.shape[0]
    t = b * s_len
    alpha = (2 * depth) ** 0.25
    assert d == D_MODEL and t % TOKEN_BLOCK == 0 and s_len % ATT_BLOCK == 0 and t % MOE_ROWS == 0
    assert ATT_BLOCK % SW_WINDOW == 0

    posf = positions.astype(F32)
    pos_col = posf[:, None]
    pos_row = posf[None, :]
    cos, sin = _rope_tables(positions)
    q_scale = MLA_QK ** -0.5
    ones_n = jnp.ones((s_len, MLA_NOPE), F32)
    cq = jnp.tile(jnp.concatenate([ones_n, cos], axis=1) * q_scale, (b, MLA_HEADS))
    sq = jnp.tile(jnp.concatenate([0.0 * ones_n, sin], axis=1) * q_scale, (b, MLA_HEADS))
    lane_pad = jnp.zeros((s_len, LANES - MLA_ROPE), F32)
    ck = jnp.tile(jnp.concatenate([cos, lane_pad], axis=1), (b, 1))
    sk = jnp.tile(jnp.concatenate([sin, lane_pad], axis=1), (b, 1))
    rscale = jnp.concatenate([
        jnp.full((B_WIDTH,), SB_DIM ** -0.5, F32), jnp.ones((2 * B_WIDTH,), F32),
        jnp.full((C_WIDTH,), SW_DIM ** -0.5, F32), jnp.ones((2 * SW_KV_HEADS * SW_DIM,), F32),
        jnp.full((2 * DF_HEADS * DF_DIM,), DF_DIM ** -0.5, F32), jnp.ones((2 * DF_HEADS * DF_DIM + D_WIDTH,), F32),
    ])[None, :]
    sw_slopes = _alibi_slopes(SW_Q_HEADS)
    df_slopes = _alibi_slopes(DF_HEADS)

    perm = np.arange(N_EXPERTS).reshape(N_GROUPS, EXPERTS_PER_GROUP).T.reshape(-1)
    wr_t = w_router.astype(F32).T[perm]
    rb_col = router_bias.astype(F32)[perm][:, None]

    x2 = x.reshape(t, d)
    for l in range(depth):
        lam_init = 0.8 - 0.6 * math.exp(-0.3 * l)
        wuq, wuqs = _prep_w_uq(w_uq[l])
        hq, kv, kr, hr = _proj(x2, _prep_w_in(w_in[l]), mla_q_norm[l][None, :], mla_kv_norm[l][None, :],
                               wuq, wuqs, w_ukv[l].astype(BF16), cq, sq, ck, sk, rscale)

        kv4 = kv.reshape(b, s_len, MLA_HEADS, MLA_NOPE + MLA_V)
        kr4 = jnp.broadcast_to(kr[:, :MLA_ROPE].reshape(b, s_len, 1, MLA_ROPE), (b, s_len, MLA_HEADS, MLA_ROPE))
        mla_q = _heads(hq, b, s_len, MLA_HEADS, MLA_QK)
        mla_k = jnp.concatenate([kv4[..., :MLA_NOPE], kr4], axis=-1).transpose(0, 2, 1, 3)
        mla_v = kv4[..., MLA_NOPE:].transpose(0, 2, 1, 3)
        o = 0
        sb_q = _heads(hr[:, o:o + B_WIDTH], b, s_len, SB_HEADS, SB_DIM); o += B_WIDTH
        sb_k = _heads(hr[:, o:o + B_WIDTH], b, s_len, SB_HEADS, SB_DIM); o += B_WIDTH
        sb_v = _heads(hr[:, o:o + B_WIDTH], b, s_len, SB_HEADS, SB_DIM); o += B_WIDTH
        sw_q = _heads(hr[:, o:o + C_WIDTH], b, s_len, SW_Q_HEADS, SW_DIM); o += C_WIDTH
        kvw = SW_KV_HEADS * SW_DIM
        sw_k = _heads(hr[:, o:o + kvw], b, s_len, SW_KV_HEADS, SW_DIM); o += kvw
        sw_v = _heads(hr[:, o:o + kvw], b, s_len, SW_KV_HEADS, SW_DIM); o += kvw
        dfw = 2 * DF_HEADS * DF_DIM
        df_q = _heads(hr[:, o:o + dfw], b, s_len, DF_HEADS, 2 * DF_DIM); o += dfw
        df_k = _heads(hr[:, o:o + dfw], b, s_len, DF_HEADS, 2 * DF_DIM); o += dfw
        df_v = _heads(hr[:, o:o + D_WIDTH], b, s_len, DF_HEADS, DF_VDIM)

        o_a = _unheads(_mla_attention(mla_q, mla_k, mla_v))
        o_b = _unheads(_sb_attention(sb_q, sb_k, sb_v))
        o_c = _unheads(_swa_attention(sw_q, sw_k, sw_v, sw_slopes, sw_sinks[l].astype(F32), pos_col, pos_row))
        o_d = _unheads(_diff_attention(df_q, df_k, df_v, df_slopes, pos_row,
                                       df_lq1[l][None, :], df_lk1[l][None, :], df_lq2[l][None, :],
                                       df_lk2[l][None, :], df_subln[l][None, :], lam_init))

        x1, gates = _merge(x2, o_a, o_b, o_c, o_d, w_gate[l].astype(BF16), b_gate[l][None, :],
                           w_br_a[l].astype(BF16), w_br_b[l].astype(BF16), w_br_c[l].astype(BF16),
                           w_br_d[l].astype(BF16), w_out[l].astype(BF16), ln1_g[l][None, :], ln1_b[l][None, :],
                           wr_t, rb_col, alpha)
        x2 = _moe(x1, gates, moe_w1[l].astype(BF16), moe_w3[l].astype(BF16), moe_w2[l].astype(BF16),
                  ln2_g[l][None, :], ln2_b[l][None, :], alpha)
    return x2.reshape(b, s_len, d)
```

```python
import functools
import math

import numpy as np
import jax
import jax.numpy as jnp
from jax import lax
from jax.experimental import pallas as pl
from jax.experimental.pallas import tpu as pltpu

F32 = jnp.float32
BF16 = jnp.bfloat16

D_MODEL = 1024
NEG_INF = -1e30
LN_EPS = 1e-5
RMS_EPS = 1e-6

MLA_HEADS = 4
MLA_Q_RANK = 256
MLA_KV_RANK = 128
MLA_NOPE = 64
MLA_ROPE = 32
MLA_V = 64
MLA_QK = MLA_NOPE + MLA_ROPE
ROPE_THETA = 10000.0

SB_HEADS = 4
SB_DIM = 64

SW_Q_HEADS = 8
SW_KV_HEADS = 2
SW_DIM = 64
SW_WINDOW = 128
SW_GROUP = SW_Q_HEADS // SW_KV_HEADS

DF_HEADS = 4
DF_DIM = 32
DF_VDIM = 2 * DF_DIM

A_WIDTH = MLA_HEADS * MLA_V
B_WIDTH = SB_HEADS * SB_DIM
C_WIDTH = SW_Q_HEADS * SW_DIM
D_WIDTH = DF_HEADS * DF_VDIM
N_BRANCH = 4

IN_SPLITS = (MLA_Q_RANK, MLA_KV_RANK, MLA_ROPE, 3 * B_WIDTH, C_WIDTH, 2 * SW_KV_HEADS * SW_DIM,
             2 * DF_HEADS * DF_DIM, 2 * DF_HEADS * DF_DIM, D_WIDTH)

N_EXPERTS = 32
N_GROUPS = 8
EXPERTS_PER_GROUP = N_EXPERTS // N_GROUPS
D_EXPERT = 256

LANES = 128
HEAD_WIDTH = 64

OFF_CQ = 0
OFF_CKV = OFF_CQ + MLA_Q_RANK
OFF_KR = OFF_CKV + MLA_KV_RANK
OFF_KRS = OFF_KR + LANES
OFF_REST = OFF_KRS + LANES
REST_WIDTH = sum(IN_SPLITS[3:])
IN_PAD_WIDTH = OFF_REST + REST_WIDTH

TOKEN_BLOCK = 512
ATT_BLOCK = 512
SUFFIX_BLOCK = 256
EXP_UNDERFLOW = -104.0
MOE_ROWS = 1024
VMEM_LIMIT = 56 * 1024 * 1024


def _cparams(sem):
    return pltpu.CompilerParams(dimension_semantics=sem, vmem_limit_bytes=VMEM_LIMIT)


def _nt_dot(a, b, **kw):
    return lax.dot_general(a, b, (((1,), (1,)), ((), ())), preferred_element_type=F32, **kw)


def _const_spec(shape):
    nd = len(shape)
    return pl.BlockSpec(shape, lambda *_: (0,) * nd)


def _split_heads(val, out_ref, n_heads, width):
    for h in range(n_heads):
        tile = val[:, (h // 2) * LANES:(h // 2 + 1) * LANES]
        if h % 2:
            tile = pltpu.roll(tile, LANES - width, axis=1)
        out_ref[0, h] = tile[:, :width].astype(out_ref.dtype)


def _proj_kernel(x_ref, win_ref, qn_ref, kvn_ref, wuq_ref, wuqs_ref, wuk_ref, wuv_ref,
                 cq_ref, sq_ref, ck_ref, sk_ref, rscale_ref,
                 mq_ref, mk_ref, mv_ref, sbq_ref, sbk_ref, sbv_ref, swq_ref, swk_ref, swv_ref,
                 dfq_ref, dfk_ref, dfv_ref):
    xb = x_ref[0].astype(BF16)
    h = jnp.dot(xb, win_ref[...], preferred_element_type=F32)
    c_q = h[:, OFF_CQ:OFF_CQ + MLA_Q_RANK]
    c_kv = h[:, OFF_CKV:OFF_CKV + MLA_KV_RANK]
    nq = c_q * lax.rsqrt(jnp.mean(c_q * c_q, axis=-1, keepdims=True) + RMS_EPS) * qn_ref[...]
    nqb = nq.astype(BF16)
    q = jnp.dot(nqb, wuq_ref[...], preferred_element_type=F32)
    qs = jnp.dot(nqb, wuqs_ref[...], preferred_element_type=F32)
    q = q * cq_ref[...] + qs * sq_ref[...]
    nkv = c_kv * lax.rsqrt(jnp.mean(c_kv * c_kv, axis=-1, keepdims=True) + RMS_EPS) * kvn_ref[...]
    nkvb = nkv.astype(BF16)
    kn = jnp.dot(nkvb, wuk_ref[...], preferred_element_type=F32)
    vv = jnp.dot(nkvb, wuv_ref[...], preferred_element_type=F32)
    kr = h[:, OFF_KR:OFF_KR + LANES] * ck_ref[...] + h[:, OFF_KRS:OFF_KRS + LANES] * sk_ref[...]
    for hd in range(MLA_HEADS):
        lanes = slice(hd * LANES, (hd + 1) * LANES)
        mq_ref[0, hd] = q[:, lanes][:, :MLA_QK].astype(mq_ref.dtype)
        mk_ref[0, hd] = (kn[:, lanes] + kr)[:, :MLA_QK].astype(mk_ref.dtype)
        mv_ref[0, hd] = vv[:, lanes][:, :MLA_V].astype(mv_ref.dtype)
    rest = h[:, OFF_REST:] * rscale_ref[...]
    o = 0
    for ref, nh in ((sbq_ref, SB_HEADS), (sbk_ref, SB_HEADS), (sbv_ref, SB_HEADS),
                    (swq_ref, SW_Q_HEADS), (swk_ref, SW_KV_HEADS), (swv_ref, SW_KV_HEADS),
                    (dfq_ref, DF_HEADS), (dfk_ref, DF_HEADS), (dfv_ref, DF_HEADS)):
        _split_heads(rest[:, o:o + nh * HEAD_WIDTH], ref, nh, HEAD_WIDTH)
        o += nh * HEAD_WIDTH


def _proj(x, win, qn, kvn, wuq, wuqs, wuk, wuv, cq, sq, ck, sk, rscale):
    b, s_len, _ = x.shape
    tm = TOKEN_BLOCK
    tab = lambda w: pl.BlockSpec((tm, w), lambda bb, i: (i, 0))
    heads = lambda nh, d: pl.BlockSpec((1, nh, tm, d), lambda bb, i: (bb, 0, i, 0))
    shape = lambda nh, d: jax.ShapeDtypeStruct((b, nh, s_len, d), BF16)
    outs = [(MLA_HEADS, MLA_QK), (MLA_HEADS, MLA_QK), (MLA_HEADS, MLA_V)] + [(SB_HEADS, SB_DIM)] * 3 + [
        (SW_Q_HEADS, SW_DIM), (SW_KV_HEADS, SW_DIM), (SW_KV_HEADS, SW_DIM)] + [(DF_HEADS, HEAD_WIDTH)] * 3
    consts = (win, qn, kvn, wuq, wuqs, wuk, wuv)
    return pl.pallas_call(
        _proj_kernel,
        grid=(b, s_len // tm),
        in_specs=[pl.BlockSpec((1, tm, D_MODEL), lambda bb, i: (bb, i, 0))]
                 + [_const_spec(c.shape) for c in consts]
                 + [tab(cq.shape[1]), tab(sq.shape[1]), tab(LANES), tab(LANES), _const_spec(rscale.shape)],
        out_specs=[heads(nh, d) for nh, d in outs],
        out_shape=[shape(nh, d) for nh, d in outs],
        compiler_params=_cparams(("arbitrary", "arbitrary")),
        name="proj",
    )(x, *consts, cq, sq, ck, sk, rscale)


def _causal_mask(n):
    row = lax.broadcasted_iota(jnp.int32, (n, n), 0)
    col = lax.broadcasted_iota(jnp.int32, (n, n), 1)
    return row, col


def _softmax_step(s, v, m, l, acc):
    m_new = jnp.maximum(m, jnp.max(s, axis=-1, keepdims=True))
    p = jnp.exp(s - m_new)
    alpha = jnp.exp(m - m_new)
    l_new = alpha * l + jnp.sum(p, axis=-1, keepdims=True)
    acc_new = alpha * acc + jnp.dot(p.astype(BF16), v, preferred_element_type=F32)
    return m_new, l_new, acc_new


def _mla_kernel(q_ref, k_ref, v_ref, o_ref, *, blk):
    qi = pl.program_id(2)
    q = q_ref[0, 0]

    def kv_block(ki):
        start = pl.multiple_of(ki * blk, blk)
        return k_ref[0, 0, pl.ds(start, blk), :], v_ref[0, 0, pl.ds(start, blk), :]

    def body(ki, carry):
        k, v = kv_block(ki)
        return _softmax_step(_nt_dot(q, k), v, *carry)

    init = (jnp.full((blk, 1), NEG_INF, F32), jnp.zeros((blk, 1), F32), jnp.zeros((blk, MLA_V), F32))
    carry = lax.fori_loop(0, qi, body, init)
    k, v = kv_block(qi)
    row, col = _causal_mask(blk)
    s = jnp.where(col <= row, _nt_dot(q, k), NEG_INF)
    m, l, acc = _softmax_step(s, v, *carry)
    o_ref[0, 0] = (acc / l).astype(o_ref.dtype)


def _head_spec(blk, d):
    return pl.BlockSpec((1, 1, blk, d), lambda b, h, i: (b, h, i, 0))


def _full_spec(s_len, d):
    return pl.BlockSpec((1, 1, s_len, d), lambda b, h, i: (b, h, 0, 0))


def _mla_attention(q, k, v):
    b, nh, s_len, _ = q.shape
    blk = ATT_BLOCK
    return pl.pallas_call(
        functools.partial(_mla_kernel, blk=blk),
        grid=(b, nh, s_len // blk),
        in_specs=[_head_spec(blk, MLA_QK), _full_spec(s_len, MLA_QK), _full_spec(s_len, MLA_V)],
        out_specs=_head_spec(blk, MLA_V),
        out_shape=jax.ShapeDtypeStruct((b, nh, s_len, MLA_V), BF16),
        compiler_params=_cparams(("arbitrary",) * 3),
        name="mla",
    )(q, k, v)


def _log_sigmoid(z):
    return jnp.minimum(z, 0.0) - jnp.log(1.0 + jnp.exp(-jnp.abs(z)))


def _suffix_sum(l1m, upper):
    sub = upper.shape[0]
    n_sub = l1m.shape[1] // sub
    pieces = [None] * n_sub
    right = None
    for c in reversed(range(n_sub)):
        chunk = l1m[:, c * sub:(c + 1) * sub]
        hi = chunk.astype(BF16)
        lo = (chunk - hi.astype(F32)).astype(BF16)
        inner = jnp.dot(hi, upper, preferred_element_type=F32) + jnp.dot(lo, upper, preferred_element_type=F32)
        total = inner[:, 0:1] + chunk[:, 0:1]
        if right is None:
            pieces[c], right = inner, total
        else:
            pieces[c], right = inner + right, right + total
    return jnp.concatenate(pieces, axis=1), right


def _sb_kernel(q_ref, k_ref, v_ref, o_ref, *, blk):
    qi = pl.program_id(2)
    q = q_ref[0, 0]
    srow, scol = _causal_mask(SUFFIX_BLOCK)
    upper = (srow > scol).astype(BF16)

    def kv_block(ki):
        start = pl.multiple_of(ki * blk, blk)
        return k_ref[0, 0, pl.ds(start, blk), :], v_ref[0, 0, pl.ds(start, blk), :]

    row, col = _causal_mask(blk)
    k, v = kv_block(qi)
    z = _nt_dot(q, k)
    mask = col < row
    lb = _log_sigmoid(z)
    l1m = jnp.where(mask, lb - z, 0.0)
    suffix, passed = _suffix_sum(l1m, upper)
    a = jnp.where(mask, jnp.exp(lb + suffix), 0.0)
    acc = jnp.dot(a.astype(BF16), v, preferred_element_type=F32)

    def live(carry):
        j, passed, _ = carry
        return (j < qi) & (jnp.max(passed) >= EXP_UNDERFLOW)

    def body(carry):
        j, passed, acc = carry
        k, v = kv_block(qi - 1 - j)
        z = _nt_dot(q, k)
        lb = _log_sigmoid(z)
        l1m = lb - z
        suffix, total = _suffix_sum(l1m, upper)
        a = jnp.exp(lb + suffix + passed)
        acc = acc + jnp.dot(a.astype(BF16), v, preferred_element_type=F32)
        return j + 1, passed + total, acc

    _, _, acc = lax.while_loop(live, body, (jnp.int32(0), passed, acc))
    o_ref[0, 0] = acc.astype(o_ref.dtype)


def _sb_attention(q, k, v):
    b, nh, s_len, d = q.shape
    blk = ATT_BLOCK
    return pl.pallas_call(
        functools.partial(_sb_kernel, blk=blk),
        grid=(b, nh, s_len // blk),
        in_specs=[_head_spec(blk, d), _full_spec(s_len, d), _full_spec(s_len, d)],
        out_specs=_head_spec(blk, d),
        out_shape=jax.ShapeDtypeStruct((b, nh, s_len, d), BF16),
        compiler_params=_cparams(("arbitrary",) * 3),
        name="sb",
    )(q, k, v)


def _swa_kernel(slopes_ref, sinks_ref, q_ref, kp_ref, kc_ref, vp_ref, vc_ref,
                pq_ref, pkp_ref, pkc_ref, o_ref):
    g = pl.program_id(1)
    i = pl.program_id(2)
    w = SW_WINDOW
    q = q_ref[0].reshape(SW_GROUP * w, SW_DIM)
    k = jnp.concatenate([kp_ref[0, 0], kc_ref[0, 0]], axis=0)
    v = jnp.concatenate([vp_ref[0, 0], vc_ref[0, 0]], axis=0)
    s = _nt_dot(q, k)
    pk = jnp.concatenate([pkp_ref[...], pkc_ref[...]], axis=1)
    dist = pq_ref[...] - pk
    qi = lax.broadcasted_iota(jnp.int32, (w, 2 * w), 0)
    kj = lax.broadcasted_iota(jnp.int32, (w, 2 * w), 1)
    mask = (kj > qi) & (kj <= qi + w) & ((kj >= w) | (i > 0))
    for j in range(SW_GROUP):
        head = g * SW_GROUP + j
        sj = jnp.where(mask, s[j * w:(j + 1) * w] - slopes_ref[head] * dist, NEG_INF)
        sink = sinks_ref[head]
        m = jnp.maximum(jnp.max(sj, axis=-1, keepdims=True), sink)
        p = jnp.exp(sj - m)
        denom = jnp.sum(p, axis=-1, keepdims=True) + jnp.exp(sink - m)
        o = jnp.dot(p.astype(BF16), v, preferred_element_type=F32) / denom
        o_ref[0, j] = o.astype(o_ref.dtype)


def _swa_attention(q, k, v, slopes, sinks, pos_col, pos_row):
    b, _, s_len, d = q.shape
    w = SW_WINDOW
    prev = lambda i: jnp.maximum(i - 1, 0)
    smem = pl.BlockSpec(memory_space=pltpu.SMEM)
    kv_prev = pl.BlockSpec((1, 1, w, d), lambda bb, g, i: (bb, g, prev(i), 0))
    kv_cur = pl.BlockSpec((1, 1, w, d), lambda bb, g, i: (bb, g, i, 0))
    return pl.pallas_call(
        _swa_kernel,
        grid=(b, SW_KV_HEADS, s_len // w),
        in_specs=[smem, smem,
                  pl.BlockSpec((1, SW_GROUP, w, d), lambda bb, g, i: (bb, g, i, 0)),
                  kv_prev, kv_cur, kv_prev, kv_cur,
                  pl.BlockSpec((w, 1), lambda bb, g, i: (i, 0)),
                  pl.BlockSpec((1, w), lambda bb, g, i: (0, prev(i))),
                  pl.BlockSpec((1, w), lambda bb, g, i: (0, i))],
        out_specs=pl.BlockSpec((1, SW_GROUP, w, d), lambda bb, g, i: (bb, g, i, 0)),
        out_shape=jax.ShapeDtypeStruct((b, SW_Q_HEADS, s_len, d), BF16),
        compiler_params=_cparams(("arbitrary",) * 3),
        name="swa",
    )(slopes, sinks, q, k, k, v, v, pos_col, pos_row, pos_row)


def _diff_kernel(slopes_ref, q_ref, k_ref, v_ref, pk_ref, lq1_ref, lk1_ref, lq2_ref, lk2_ref,
                 subln_ref, o_ref, *, blk, lam_init):
    h = pl.program_id(1)
    qi = pl.program_id(2)
    q = q_ref[0, 0]
    lane = lax.broadcasted_iota(jnp.int32, q.shape, 1)
    zero = jnp.zeros_like(q)
    qq = jnp.concatenate([jnp.where(lane < DF_DIM, q, zero), jnp.where(lane >= DF_DIM, q, zero)], axis=0)
    slope = slopes_ref[h]

    def block(ki):
        start = pl.multiple_of(ki * blk, blk)
        k = k_ref[0, 0, pl.ds(start, blk), :]
        v = v_ref[0, 0, pl.ds(start, blk), :]
        return _nt_dot(qq, k) + slope * pk_ref[:, pl.ds(start, blk)], v

    def body(ki, carry):
        s, v = block(ki)
        return _softmax_step(s, v, *carry)

    init = (jnp.full((2 * blk, 1), NEG_INF, F32), jnp.zeros((2 * blk, 1), F32),
            jnp.zeros((2 * blk, DF_VDIM), F32))
    carry = lax.fori_loop(0, qi, body, init)
    s, v = block(qi)
    row = lax.broadcasted_iota(jnp.int32, (2 * blk, blk), 0)
    col = lax.broadcasted_iota(jnp.int32, (2 * blk, blk), 1)
    causal = (col <= row) & ((row < blk) | (col <= row - blk))
    _, l, acc = _softmax_step(jnp.where(causal, s, NEG_INF), v, *carry)
    lam = (jnp.exp(jnp.sum(lq1_ref[...] * lk1_ref[...], keepdims=True))
           - jnp.exp(jnp.sum(lq2_ref[...] * lk2_ref[...], keepdims=True)) + lam_init)
    o = acc / l
    o = o[:blk] - lam * o[blk:]
    o = o * lax.rsqrt(jnp.mean(o * o, axis=-1, keepdims=True) + RMS_EPS) * subln_ref[...] * (1.0 - lam_init)
    o_ref[0, 0] = o.astype(o_ref.dtype)


def _diff_attention(q, k, v, slopes, pos_row, lq1, lk1, lq2, lk2, subln, lam_init):
    b, nh, s_len, d = q.shape
    blk = ATT_BLOCK
    smem = pl.BlockSpec(memory_space=pltpu.SMEM)
    vec = lambda n: pl.BlockSpec((1, n), lambda bb, h, i: (0, 0))
    return pl.pallas_call(
        functools.partial(_diff_kernel, blk=blk, lam_init=lam_init),
        grid=(b, nh, s_len // blk),
        in_specs=[smem, _head_spec(blk, d), _full_spec(s_len, d), _full_spec(s_len, DF_VDIM),
                  pl.BlockSpec((1, s_len), lambda bb, h, i: (0, 0)),
                  vec(DF_DIM), vec(DF_DIM), vec(DF_DIM), vec(DF_DIM), vec(DF_VDIM)],
        out_specs=_head_spec(blk, DF_VDIM),
        out_shape=jax.ShapeDtypeStruct((b, nh, s_len, DF_VDIM), BF16),
        compiler_params=_cparams(("arbitrary",) * 3),
        name="diff",
    )(slopes, q, k, v, pos_row, lq1, lk1, lq2, lk2, subln)


def _layer_norm(v, g, b):
    mu = jnp.mean(v, axis=-1, keepdims=True)
    c = v - mu
    var = jnp.mean(c * c, axis=-1, keepdims=True)
    return c * lax.rsqrt(var + LN_EPS) * g + b


def _sigmoid(z):
    return 1.0 / (1.0 + jnp.exp(-z))


def _route(logits_t, bias_col):
    scores = _sigmoid(logits_t)
    biased = scores + bias_col
    ng = N_GROUPS
    sc = [scores[j * ng:(j + 1) * ng] for j in range(EXPERTS_PER_GROUP)]
    bs = [biased[j * ng:(j + 1) * ng] for j in range(EXPERTS_PER_GROUP)]
    gscore = None
    for a in range(EXPERTS_PER_GROUP):
        for c in range(a + 1, EXPERTS_PER_GROUP):
            pair = bs[a] + bs[c]
            gscore = pair if gscore is None else jnp.maximum(gscore, pair)
    gid = lax.broadcasted_iota(jnp.int32, gscore.shape, 0)
    best = jnp.max(gscore, axis=0, keepdims=True)
    gsel = jnp.min(jnp.where(gscore == best, gid, ng), axis=0, keepdims=True)
    onehot = gid == gsel
    pick = lambda a: jnp.sum(jnp.where(onehot, a, 0.0), axis=0, keepdims=True)
    b_in = [pick(a) for a in bs]
    s_in = [pick(a) for a in sc]
    chosen = []
    for j in range(EXPERTS_PER_GROUP):
        ahead = jnp.zeros_like(b_in[j])
        for c in range(EXPERTS_PER_GROUP):
            if c == j:
                continue
            beats = (b_in[c] > b_in[j]) | ((b_in[c] == b_in[j]) & (c < j))
            ahead = ahead + beats.astype(F32)
        chosen.append(jnp.where(ahead < 2.0, s_in[j], 0.0))
    total = chosen[0] + chosen[1] + chosen[2] + chosen[3]
    gates = [c / total for c in chosen]
    return gates, onehot


def _merge_kernel(x_ref, oa_ref, ob_ref, oc_ref, od_ref, wg_ref, bg_ref, wa_ref, wb_ref, wc_ref, wd_ref,
                  wo_ref, g_ref, b_ref, wr_ref, rb_ref, x1_ref, gate_ref, *, alpha):
    tm = x_ref.shape[1]
    x = x_ref[0]
    xb = x.astype(BF16)
    merged = None
    for n, (o_ref, w_ref) in enumerate(((oa_ref, wa_ref), (ob_ref, wb_ref), (oc_ref, wc_ref), (od_ref, wd_ref))):
        cols = slice(n * D_MODEL, (n + 1) * D_MODEL)
        gate = _sigmoid(jnp.dot(xb, wg_ref[:, cols], preferred_element_type=F32) + bg_ref[:, cols])
        branch = None
        for hd in range(o_ref.shape[1]):
            part = jnp.dot(o_ref[0, hd], w_ref[hd * HEAD_WIDTH:(hd + 1) * HEAD_WIDTH, :],
                           preferred_element_type=F32)
            branch = part if branch is None else branch + part
        term = gate * branch
        merged = term if merged is None else merged + term
    mix = jnp.dot(merged.astype(BF16), wo_ref[...], preferred_element_type=F32)
    y = _layer_norm(alpha * x + mix, g_ref[...], b_ref[...])
    x1_ref[0] = y

    logits_t = _nt_dot(wr_ref[...], y, precision=lax.Precision.HIGHEST)
    gates, onehot = _route(logits_t, rb_ref[...])
    rows = [jnp.where(onehot, gate, 0.0) for gate in gates]
    rows.append(jnp.zeros((LANES - N_EXPERTS, tm), F32))
    gate_ref[0] = jnp.concatenate(rows, axis=0).T


def _merge(x, oa, ob, oc, od, wg, bg, wa, wb, wc, wd, wo, g, b, wr_t, rb_col, alpha):
    bsz, s_len, _ = x.shape
    tm = TOKEN_BLOCK
    row = lambda w: pl.BlockSpec((1, tm, w), lambda bb, i: (bb, i, 0))
    heads = lambda o: pl.BlockSpec((1, o.shape[1], tm, o.shape[3]), lambda bb, i: (bb, 0, i, 0))
    consts = (wg, bg, wa, wb, wc, wd, wo, g, b, wr_t, rb_col)
    return pl.pallas_call(
        functools.partial(_merge_kernel, alpha=alpha),
        grid=(bsz, s_len // tm),
        in_specs=[row(D_MODEL), heads(oa), heads(ob), heads(oc), heads(od)]
                 + [_const_spec(c.shape) for c in consts],
        out_specs=[row(D_MODEL), row(LANES)],
        out_shape=[jax.ShapeDtypeStruct((bsz, s_len, D_MODEL), F32),
                   jax.ShapeDtypeStruct((bsz, s_len, LANES), F32)],
        compiler_params=_cparams(("arbitrary", "arbitrary")),
        name="merge",
    )(x, oa, ob, oc, od, *consts)


def _moe_kernel(x_ref, gate_ref, w1_ref, w3_ref, w2_ref, g_ref, b_ref, o_ref, acc_ref, *, alpha):
    grp = pl.program_id(1)

    @pl.when(grp == 0)
    def _():
        acc_ref[...] = jnp.zeros_like(acc_ref)

    xb = x_ref[...].astype(BF16)
    gates = gate_ref[...]
    lane = lax.broadcasted_iota(jnp.int32, gates.shape, 1)
    hid = []
    for e in range(EXPERTS_PER_GROUP):
        h1 = jnp.dot(xb, w1_ref[e], preferred_element_type=F32)
        h3 = jnp.dot(xb, w3_ref[e], preferred_element_type=F32)
        gate = jnp.sum(jnp.where(lane == e * N_GROUPS + grp, gates, 0.0), axis=1, keepdims=True)
        hid.append((h1 * _sigmoid(h1) * h3 * gate).astype(BF16))
    w2 = w2_ref[...].reshape(EXPERTS_PER_GROUP * D_EXPERT, D_MODEL)
    acc_ref[...] += jnp.dot(jnp.concatenate(hid, axis=1), w2, preferred_element_type=F32)

    @pl.when(grp == N_GROUPS - 1)
    def _():
        o_ref[...] = _layer_norm(alpha * x_ref[...] + acc_ref[...], g_ref[...], b_ref[...])


def _moe(x1, gates, w1, w3, w2, g, b, alpha):
    t = x1.shape[0]
    tm = MOE_ROWS
    epg = EXPERTS_PER_GROUP
    return pl.pallas_call(
        functools.partial(_moe_kernel, alpha=alpha),
        grid=(t // tm, N_GROUPS),
        in_specs=[pl.BlockSpec((tm, D_MODEL), lambda i, e: (i, 0)),
                  pl.BlockSpec((tm, LANES), lambda i, e: (i, 0)),
                  pl.BlockSpec((epg, D_MODEL, D_EXPERT), lambda i, e: (e, 0, 0)),
                  pl.BlockSpec((epg, D_MODEL, D_EXPERT), lambda i, e: (e, 0, 0)),
                  pl.BlockSpec((epg, D_EXPERT, D_MODEL), lambda i, e: (e, 0, 0)),
                  pl.BlockSpec((1, D_MODEL), lambda i, e: (0, 0)),
                  pl.BlockSpec((1, D_MODEL), lambda i, e: (0, 0))],
        out_specs=pl.BlockSpec((tm, D_MODEL), lambda i, e: (i, 0)),
        out_shape=jax.ShapeDtypeStruct((t, D_MODEL), F32),
        scratch_shapes=[pltpu.VMEM((tm, D_MODEL), F32)],
        compiler_params=_cparams(("arbitrary", "arbitrary")),
        name="moe",
    )(x1, gates, w1, w3, w2, g, b)


def _alibi_slopes(n_heads):
    return 2.0 ** (-8.0 * jnp.arange(1, n_heads + 1, dtype=F32) / n_heads)


def _rope_tables(positions):
    half = MLA_ROPE // 2
    inv_freq = ROPE_THETA ** (-jnp.arange(half, dtype=F32) / half)
    ang = positions.astype(F32)[:, None] * inv_freq[None, :]
    cos = jnp.cos(ang)
    sin = jnp.sin(ang)
    return jnp.concatenate([cos, cos], axis=1), jnp.concatenate([sin, sin], axis=1)


def _rotate_half_columns(w):
    half = w.shape[-1] // 2
    return jnp.concatenate([-w[..., half:], w[..., :half]], axis=-1)


def _prep_w_in(w_in):
    split_at = [int(c) for c in np.cumsum(IN_SPLITS)[:-1]]
    parts = jnp.split(w_in, split_at, axis=-1)
    c_q, c_kv, k_rope = parts[0], parts[1], parts[2]
    before = jnp.zeros((w_in.shape[0], MLA_NOPE), w_in.dtype)
    after = jnp.zeros((w_in.shape[0], LANES - MLA_QK), w_in.dtype)
    return jnp.concatenate([c_q, c_kv, before, k_rope, after, before, _rotate_half_columns(k_rope), after]
                           + list(parts[3:]), axis=-1).astype(BF16)


def _pad_heads(w, n_heads, width):
    w3 = w.reshape(w.shape[0], n_heads, width)
    return jnp.pad(w3, ((0, 0), (0, 0), (0, LANES - width))).reshape(w.shape[0], n_heads * LANES)


def _prep_w_uq(w_uq):
    w = w_uq.reshape(MLA_Q_RANK, MLA_HEADS, MLA_QK)
    rot = jnp.concatenate([jnp.zeros_like(w[..., :MLA_NOPE]), _rotate_half_columns(w[..., MLA_NOPE:])], axis=-1)
    return (_pad_heads(w_uq, MLA_HEADS, MLA_QK).astype(BF16),
            _pad_heads(rot.reshape(w_uq.shape), MLA_HEADS, MLA_QK).astype(BF16))


def _prep_w_ukv(w_ukv):
    w = w_ukv.reshape(MLA_KV_RANK, MLA_HEADS, MLA_NOPE + MLA_V)
    wk = w[..., :MLA_NOPE].reshape(MLA_KV_RANK, MLA_HEADS * MLA_NOPE)
    wv = w[..., MLA_NOPE:].reshape(MLA_KV_RANK, MLA_HEADS * MLA_V)
    return _pad_heads(wk, MLA_HEADS, MLA_NOPE).astype(BF16), _pad_heads(wv, MLA_HEADS, MLA_V).astype(BF16)


def kernel(x, positions, w_in, mla_q_norm, mla_kv_norm, w_uq, w_ukv, sw_sinks, df_lq1, df_lk1, df_lq2, df_lk2,
           df_subln, w_gate, b_gate, w_br_a, w_br_b, w_br_c, w_br_d, w_out, ln1_g, ln1_b, w_router, router_bias,
           moe_w1, moe_w3, moe_w2, ln2_g, ln2_b):
    b, s_len, d = x.shape
    depth = w_in.shape[0]
    t = b * s_len
    alpha = (2 * depth) ** 0.25
    assert d == D_MODEL and s_len % TOKEN_BLOCK == 0 and s_len % ATT_BLOCK == 0 and t % MOE_ROWS == 0
    assert SB_DIM == SW_DIM == DF_VDIM == 2 * DF_DIM == HEAD_WIDTH == LANES // 2
    assert ATT_BLOCK % SW_WINDOW == 0

    posf = positions.astype(F32)
    pos_col = posf[:, None]
    pos_row = posf[None, :]
    cos, sin = _rope_tables(positions)
    q_scale = MLA_QK ** -0.5
    ones_n = jnp.ones((s_len, MLA_NOPE), F32)
    tail = jnp.zeros((s_len, LANES - MLA_QK), F32)
    cq = jnp.tile(jnp.concatenate([ones_n * q_scale, cos * q_scale, tail], axis=1), (1, MLA_HEADS))
    sq = jnp.tile(jnp.concatenate([0.0 * ones_n, sin * q_scale, tail], axis=1), (1, MLA_HEADS))
    ck = jnp.concatenate([0.0 * ones_n, cos, tail], axis=1)
    sk = jnp.concatenate([0.0 * ones_n, sin, tail], axis=1)
    rscale = jnp.concatenate([
        jnp.full((B_WIDTH,), SB_DIM ** -0.5, F32), jnp.ones((2 * B_WIDTH,), F32),
        jnp.full((C_WIDTH,), SW_DIM ** -0.5, F32), jnp.ones((2 * SW_KV_HEADS * SW_DIM,), F32),
        jnp.full((2 * DF_HEADS * DF_DIM,), DF_DIM ** -0.5, F32), jnp.ones((2 * DF_HEADS * DF_DIM + D_WIDTH,), F32),
    ])[None, :]
    sw_slopes = _alibi_slopes(SW_Q_HEADS)
    df_slopes = _alibi_slopes(DF_HEADS)

    perm = np.arange(N_EXPERTS).reshape(N_GROUPS, EXPERTS_PER_GROUP).T.reshape(-1)
    wr_t = w_router.astype(F32).T[perm]
    rb_col = router_bias.astype(F32)[perm][:, None]

    x3 = x
    for l in range(depth):
        lam_init = 0.8 - 0.6 * math.exp(-0.3 * l)
        wuq, wuqs = _prep_w_uq(w_uq[l])
        wuk, wuv = _prep_w_ukv(w_ukv[l])
        (mla_q, mla_k, mla_v, sb_q, sb_k, sb_v, sw_q, sw_k, sw_v, df_q, df_k, df_v) = _proj(
            x3, _prep_w_in(w_in[l]), mla_q_norm[l][None, :], mla_kv_norm[l][None, :],
            wuq, wuqs, wuk, wuv, cq, sq, ck, sk, rscale)

        o_a = _mla_attention(mla_q, mla_k, mla_v)
        o_b = _sb_attention(sb_q, sb_k, sb_v)
        o_c = _swa_attention(sw_q, sw_k, sw_v, sw_slopes, sw_sinks[l].astype(F32), pos_col, pos_row)
        o_d = _diff_attention(df_q, df_k, df_v, df_slopes, pos_row,
                              df_lq1[l][None, :], df_lk1[l][None, :], df_lq2[l][None, :],
                              df_lk2[l][None, :], df_subln[l][None, :], lam_init)

        x1, gates = _merge(x3, o_a, o_b, o_c, o_d, w_gate[l].astype(BF16), b_gate[l][None, :],
                           w_br_a[l].astype(BF16), w_br_b[l].astype(BF16), w_br_c[l].astype(BF16),
                           w_br_d[l].astype(BF16), w_out[l].astype(BF16), ln1_g[l][None, :], ln1_b[l][None, :],
                           wr_t, rb_col, alpha)
        x2 = _moe(x1.reshape(t, d), gates.reshape(t, LANES), moe_w1[l].astype(BF16), moe_w3[l].astype(BF16),
                  moe_w2[l].astype(BF16), ln2_g[l][None, :], ln2_b[l][None, :], alpha)
        x3 = x2.reshape(b, s_len, d)
    return x3
```

```python
import functools
import math

import numpy as np
import jax
import jax.numpy as jnp
from jax import lax
from jax.experimental import pallas as pl
from jax.experimental.pallas import tpu as pltpu

F32 = jnp.float32
BF16 = jnp.bfloat16

D_MODEL = 1024
NEG_INF = -1e30
LN_EPS = 1e-5
RMS_EPS = 1e-6

MLA_HEADS = 4
MLA_Q_RANK = 256
MLA_KV_RANK = 128
MLA_NOPE = 64
MLA_ROPE = 32
MLA_V = 64
MLA_QK = MLA_NOPE + MLA_ROPE
ROPE_THETA = 10000.0

SB_HEADS = 4
SB_DIM = 64

SW_Q_HEADS = 8
SW_KV_HEADS = 2
SW_DIM = 64
SW_WINDOW = 128
SW_GROUP = SW_Q_HEADS // SW_KV_HEADS

DF_HEADS = 4
DF_DIM = 32
DF_VDIM = 2 * DF_DIM

A_WIDTH = MLA_HEADS * MLA_V
B_WIDTH = SB_HEADS * SB_DIM
C_WIDTH = SW_Q_HEADS * SW_DIM
D_WIDTH = DF_HEADS * DF_VDIM
N_BRANCH = 4

IN_SPLITS = (MLA_Q_RANK, MLA_KV_RANK, MLA_ROPE, 3 * B_WIDTH, C_WIDTH, 2 * SW_KV_HEADS * SW_DIM,
             2 * DF_HEADS * DF_DIM, 2 * DF_HEADS * DF_DIM, D_WIDTH)

N_EXPERTS = 32
N_GROUPS = 8
EXPERTS_PER_GROUP = N_EXPERTS // N_GROUPS
D_EXPERT = 256

LANES = 128
HEAD_WIDTH = 64

OFF_CQ = 0
OFF_CKV = OFF_CQ + MLA_Q_RANK
OFF_KR = OFF_CKV + MLA_KV_RANK
OFF_KRS = OFF_KR + LANES
OFF_REST = OFF_KRS + LANES
REST_WIDTH = sum(IN_SPLITS[3:])
IN_PAD_WIDTH = OFF_REST + REST_WIDTH

TOKEN_BLOCK = 512
ATT_BLOCK = 512
SUFFIX_BLOCK = 256
EXP_UNDERFLOW = -104.0
MOE_ROWS = 1024
VMEM_LIMIT = 56 * 1024 * 1024


def _cparams(sem):
    return pltpu.CompilerParams(dimension_semantics=sem, vmem_limit_bytes=VMEM_LIMIT)


def _nt_dot(a, b, **kw):
    return lax.dot_general(a, b, (((1,), (1,)), ((), ())), preferred_element_type=F32, **kw)


def _const_spec(shape):
    nd = len(shape)
    return pl.BlockSpec(shape, lambda *_: (0,) * nd)


def _split_heads(val, out_ref, n_heads, width, ones_lane=None):
    for h in range(n_heads):
        tile = val[:, (h // 2) * LANES:(h // 2 + 1) * LANES]
        if h % 2:
            tile = pltpu.roll(tile, LANES - width, axis=1)
        if ones_lane is None:
            out_ref[0, h] = tile[:, :width].astype(out_ref.dtype)
        else:
            keep = lax.broadcasted_iota(jnp.int32, tile.shape, 1) < width
            out_ref[0, h] = (jnp.where(keep, tile, 0.0) + ones_lane).astype(out_ref.dtype)


def _proj_kernel(x_ref, win_ref, qn_ref, kvn_ref, wuq_ref, wuqs_ref, wuk_ref, wuv_ref,
                 cq_ref, sq_ref, ck_ref, sk_ref, rscale_ref,
                 mq_ref, mk_ref, mv_ref, sbq_ref, sbk_ref, sbv_ref, swq_ref, swk_ref, swv_ref,
                 dfq_ref, dfk_ref, dfv_ref):
    xb = x_ref[0].astype(BF16)
    h = jnp.dot(xb, win_ref[...], preferred_element_type=F32)
    c_q = h[:, OFF_CQ:OFF_CQ + MLA_Q_RANK]
    c_kv = h[:, OFF_CKV:OFF_CKV + MLA_KV_RANK]
    nq = c_q * lax.rsqrt(jnp.mean(c_q * c_q, axis=-1, keepdims=True) + RMS_EPS) * qn_ref[...]
    nqb = nq.astype(BF16)
    q = jnp.dot(nqb, wuq_ref[...], preferred_element_type=F32)
    qs = jnp.dot(nqb, wuqs_ref[...], preferred_element_type=F32)
    q = q * cq_ref[...] + qs * sq_ref[...]
    nkv = c_kv * lax.rsqrt(jnp.mean(c_kv * c_kv, axis=-1, keepdims=True) + RMS_EPS) * kvn_ref[...]
    nkvb = nkv.astype(BF16)
    kn = jnp.dot(nkvb, wuk_ref[...], preferred_element_type=F32)
    vv = jnp.dot(nkvb, wuv_ref[...], preferred_element_type=F32)
    kr = h[:, OFF_KR:OFF_KR + LANES] * ck_ref[...] + h[:, OFF_KRS:OFF_KRS + LANES] * sk_ref[...]
    ones_lane = (lax.broadcasted_iota(jnp.int32, (1, LANES), 1) == HEAD_WIDTH).astype(F32)
    for hd in range(MLA_HEADS):
        lanes = slice(hd * LANES, (hd + 1) * LANES)
        mq_ref[0, hd] = q[:, lanes][:, :MLA_QK].astype(mq_ref.dtype)
        mk_ref[0, hd] = (kn[:, lanes] + kr)[:, :MLA_QK].astype(mk_ref.dtype)
        mv_ref[0, hd] = (vv[:, lanes] + ones_lane).astype(mv_ref.dtype)
    rest = h[:, OFF_REST:] * rscale_ref[...]
    o = 0
    for ref, nh in ((sbq_ref, SB_HEADS), (sbk_ref, SB_HEADS), (sbv_ref, SB_HEADS),
                    (swq_ref, SW_Q_HEADS), (swk_ref, SW_KV_HEADS), (swv_ref, SW_KV_HEADS),
                    (dfq_ref, DF_HEADS), (dfk_ref, DF_HEADS), (dfv_ref, DF_HEADS)):
        _split_heads(rest[:, o:o + nh * HEAD_WIDTH], ref, nh, HEAD_WIDTH,
                     ones_lane if ref is dfv_ref else None)
        o += nh * HEAD_WIDTH


def _proj(x, win, qn, kvn, wuq, wuqs, wuk, wuv, cq, sq, ck, sk, rscale):
    b, s_len, _ = x.shape
    tm = TOKEN_BLOCK
    tab = lambda w: pl.BlockSpec((tm, w), lambda bb, i: (i, 0))
    heads = lambda nh, d: pl.BlockSpec((1, nh, tm, d), lambda bb, i: (bb, 0, i, 0))
    shape = lambda nh, d: jax.ShapeDtypeStruct((b, nh, s_len, d), BF16)
    outs = [(MLA_HEADS, MLA_QK), (MLA_HEADS, MLA_QK), (MLA_HEADS, LANES)] + [(SB_HEADS, SB_DIM)] * 3 + [
        (SW_Q_HEADS, SW_DIM), (SW_KV_HEADS, SW_DIM), (SW_KV_HEADS, SW_DIM)] + [
        (DF_HEADS, HEAD_WIDTH), (DF_HEADS, HEAD_WIDTH), (DF_HEADS, LANES)]
    consts = (win, qn, kvn, wuq, wuqs, wuk, wuv)
    return pl.pallas_call(
        _proj_kernel,
        grid=(b, s_len // tm),
        in_specs=[pl.BlockSpec((1, tm, D_MODEL), lambda bb, i: (bb, i, 0))]
                 + [_const_spec(c.shape) for c in consts]
                 + [tab(cq.shape[1]), tab(sq.shape[1]), tab(LANES), tab(LANES), _const_spec(rscale.shape)],
        out_specs=[heads(nh, d) for nh, d in outs],
        out_shape=[shape(nh, d) for nh, d in outs],
        compiler_params=_cparams(("arbitrary", "arbitrary")),
        name="proj",
    )(x, *consts, cq, sq, ck, sk, rscale)


def _causal_mask(n):
    row = lax.broadcasted_iota(jnp.int32, (n, n), 0)
    col = lax.broadcasted_iota(jnp.int32, (n, n), 1)
    return row, col


def _softmax_step(s, v_ones, m, acc):
    m_new = jnp.maximum(m, jnp.max(s, axis=-1, keepdims=True))
    p = jnp.exp((s - m_new).astype(BF16))
    alpha = jnp.exp(m - m_new)
    return m_new, alpha * acc + jnp.dot(p, v_ones, preferred_element_type=F32)


def _normalised(acc, width):
    return acc[:, :width] / acc[:, width:width + 1]


def _mla_kernel(q_ref, k_ref, v_ref, o_ref, *, blk):
    qi = pl.program_id(2)
    q = q_ref[0, 0]

    def kv_block(ki):
        start = pl.multiple_of(ki * blk, blk)
        return k_ref[0, 0, pl.ds(start, blk), :], v_ref[0, 0, pl.ds(start, blk), :]

    def body(ki, carry):
        k, v = kv_block(ki)
        return _softmax_step(_nt_dot(q, k), v, *carry)

    init = (jnp.full((blk, 1), NEG_INF, F32), jnp.zeros((blk, LANES), F32))
    carry = lax.fori_loop(0, qi, body, init)
    k, v = kv_block(qi)
    row, col = _causal_mask(blk)
    s = jnp.where(col <= row, _nt_dot(q, k), NEG_INF)
    _, acc = _softmax_step(s, v, *carry)
    o_ref[0, 0] = _normalised(acc, MLA_V).astype(o_ref.dtype)


def _head_spec(blk, d):
    return pl.BlockSpec((1, 1, blk, d), lambda b, h, i: (b, h, i, 0))


def _full_spec(s_len, d):
    return pl.BlockSpec((1, 1, s_len, d), lambda b, h, i: (b, h, 0, 0))


def _mla_attention(q, k, v):
    b, nh, s_len, _ = q.shape
    blk = ATT_BLOCK
    return pl.pallas_call(
        functools.partial(_mla_kernel, blk=blk),
        grid=(b, nh, s_len // blk),
        in_specs=[_head_spec(blk, MLA_QK), _full_spec(s_len, MLA_QK), _full_spec(s_len, LANES)],
        out_specs=_head_spec(blk, MLA_V),
        out_shape=jax.ShapeDtypeStruct((b, nh, s_len, MLA_V), BF16),
        compiler_params=_cparams(("arbitrary",) * 3),
        name="mla",
    )(q, k, v)


def _log_sigmoid(z):
    return jnp.minimum(z, 0.0) - jnp.log(1.0 + jnp.exp(-jnp.abs(z)))


def _suffix_sum(l1m, upper):
    sub = upper.shape[0]
    n_sub = l1m.shape[1] // sub
    pieces = [None] * n_sub
    right = None
    for c in reversed(range(n_sub)):
        chunk = l1m[:, c * sub:(c + 1) * sub]
        hi = chunk.astype(BF16)
        lo = (chunk - hi.astype(F32)).astype(BF16)
        inner = jnp.dot(hi, upper, preferred_element_type=F32) + jnp.dot(lo, upper, preferred_element_type=F32)
        total = inner[:, 0:1] + chunk[:, 0:1]
        if right is None:
            pieces[c], right = inner, total
        else:
            pieces[c], right = inner + right, right + total
    return jnp.concatenate(pieces, axis=1), right


def _sb_kernel(q_ref, k_ref, v_ref, o_ref, *, blk):
    qi = pl.program_id(2)
    q = q_ref[0, 0]
    srow, scol = _causal_mask(SUFFIX_BLOCK)
    upper = (srow > scol).astype(BF16)

    def kv_block(ki):
        start = pl.multiple_of(ki * blk, blk)
        return k_ref[0, 0, pl.ds(start, blk), :], v_ref[0, 0, pl.ds(start, blk), :]

    row, col = _causal_mask(blk)
    k, v = kv_block(qi)
    z = _nt_dot(q, k)
    mask = col < row
    lb = _log_sigmoid(z)
    l1m = jnp.where(mask, lb - z, 0.0)
    suffix, passed = _suffix_sum(l1m, upper)
    a = jnp.where(mask, jnp.exp(lb + suffix), 0.0)
    acc = jnp.dot(a.astype(BF16), v, preferred_element_type=F32)

    def live(carry):
        j, passed, _ = carry
        return (j < qi) & (jnp.max(passed) >= EXP_UNDERFLOW)

    def body(carry):
        j, passed, acc = carry
        k, v = kv_block(qi - 1 - j)
        z = _nt_dot(q, k)
        lb = _log_sigmoid(z)
        l1m = lb - z
        suffix, total = _suffix_sum(l1m, upper)
        a = jnp.exp(lb + suffix + passed)
        acc = acc + jnp.dot(a.astype(BF16), v, preferred_element_type=F32)
        return j + 1, passed + total, acc

    _, _, acc = lax.while_loop(live, body, (jnp.int32(0), passed, acc))
    o_ref[0, 0] = acc.astype(o_ref.dtype)


def _sb_attention(q, k, v):
    b, nh, s_len, d = q.shape
    blk = ATT_BLOCK
    return pl.pallas_call(
        functools.partial(_sb_kernel, blk=blk),
        grid=(b, nh, s_len // blk),
        in_specs=[_head_spec(blk, d), _full_spec(s_len, d), _full_spec(s_len, d)],
        out_specs=_head_spec(blk, d),
        out_shape=jax.ShapeDtypeStruct((b, nh, s_len, d), BF16),
        compiler_params=_cparams(("arbitrary",) * 3),
        name="sb",
    )(q, k, v)


def _swa_kernel(slopes_ref, sinks_ref, q_ref, kp_ref, kc_ref, vp_ref, vc_ref,
                pq_ref, pkp_ref, pkc_ref, o_ref):
    g = pl.program_id(1)
    i = pl.program_id(2)
    w = SW_WINDOW
    q = q_ref[0].reshape(SW_GROUP * w, SW_DIM)
    k = jnp.concatenate([kp_ref[0, 0], kc_ref[0, 0]], axis=0)
    v = jnp.concatenate([vp_ref[0, 0], vc_ref[0, 0]], axis=0)
    s = _nt_dot(q, k)
    pk = jnp.concatenate([pkp_ref[...], pkc_ref[...]], axis=1)
    dist = pq_ref[...] - pk
    qi = lax.broadcasted_iota(jnp.int32, (w, 2 * w), 0)
    kj = lax.broadcasted_iota(jnp.int32, (w, 2 * w), 1)
    mask = (kj > qi) & (kj <= qi + w) & ((kj >= w) | (i > 0))
    for j in range(SW_GROUP):
        head = g * SW_GROUP + j
        sj = jnp.where(mask, s[j * w:(j + 1) * w] - slopes_ref[head] * dist, NEG_INF)
        sink = sinks_ref[head]
        m = jnp.maximum(jnp.max(sj, axis=-1, keepdims=True), sink)
        p = jnp.exp(sj - m)
        denom = jnp.sum(p, axis=-1, keepdims=True) + jnp.exp(sink - m)
        o = jnp.dot(p.astype(BF16), v, preferred_element_type=F32) / denom
        o_ref[0, j] = o.astype(o_ref.dtype)


def _swa_attention(q, k, v, slopes, sinks, pos_col, pos_row):
    b, _, s_len, d = q.shape
    w = SW_WINDOW
    prev = lambda i: jnp.maximum(i - 1, 0)
    smem = pl.BlockSpec(memory_space=pltpu.SMEM)
    kv_prev = pl.BlockSpec((1, 1, w, d), lambda bb, g, i: (bb, g, prev(i), 0))
    kv_cur = pl.BlockSpec((1, 1, w, d), lambda bb, g, i: (bb, g, i, 0))
    return pl.pallas_call(
        _swa_kernel,
        grid=(b, SW_KV_HEADS, s_len // w),
        in_specs=[smem, smem,
                  pl.BlockSpec((1, SW_GROUP, w, d), lambda bb, g, i: (bb, g, i, 0)),
                  kv_prev, kv_cur, kv_prev, kv_cur,
                  pl.BlockSpec((w, 1), lambda bb, g, i: (i, 0)),
                  pl.BlockSpec((1, w), lambda bb, g, i: (0, prev(i))),
                  pl.BlockSpec((1, w), lambda bb, g, i: (0, i))],
        out_specs=pl.BlockSpec((1, SW_GROUP, w, d), lambda bb, g, i: (bb, g, i, 0)),
        out_shape=jax.ShapeDtypeStruct((b, SW_Q_HEADS, s_len, d), BF16),
        compiler_params=_cparams(("arbitrary",) * 3),
        name="swa",
    )(slopes, sinks, q, k, k, v, v, pos_col, pos_row, pos_row)


def _diff_kernel(slopes_ref, q_ref, k_ref, v_ref, pk_ref, lq1_ref, lk1_ref, lq2_ref, lk2_ref,
                 subln_ref, o_ref, *, blk, lam_init):
    h = pl.program_id(1)
    qi = pl.program_id(2)
    q = q_ref[0, 0]
    lane = lax.broadcasted_iota(jnp.int32, q.shape, 1)
    zero = jnp.zeros_like(q)
    qq = jnp.concatenate([jnp.where(lane < DF_DIM, q, zero), jnp.where(lane >= DF_DIM, q, zero)], axis=0)
    slope = slopes_ref[h]

    def block(ki):
        start = pl.multiple_of(ki * blk, blk)
        k = k_ref[0, 0, pl.ds(start, blk), :]
        v = v_ref[0, 0, pl.ds(start, blk), :]
        return _nt_dot(qq, k) + slope * pk_ref[:, pl.ds(start, blk)], v

    def body(ki, carry):
        s, v = block(ki)
        return _softmax_step(s, v, *carry)

    init = (jnp.full((2 * blk, 1), NEG_INF, F32), jnp.zeros((2 * blk, LANES), F32))
    carry = lax.fori_loop(0, qi, body, init)
    s, v = block(qi)
    row = lax.broadcasted_iota(jnp.int32, (2 * blk, blk), 0)
    col = lax.broadcasted_iota(jnp.int32, (2 * blk, blk), 1)
    causal = (col <= row) & ((row < blk) | (col <= row - blk))
    _, acc = _softmax_step(jnp.where(causal, s, NEG_INF), v, *carry)
    lam = (jnp.exp(jnp.sum(lq1_ref[...] * lk1_ref[...], keepdims=True))
           - jnp.exp(jnp.sum(lq2_ref[...] * lk2_ref[...], keepdims=True)) + lam_init)
    o = _normalised(acc, DF_VDIM)
    o = o[:blk] - lam * o[blk:]
    o = o * lax.rsqrt(jnp.mean(o * o, axis=-1, keepdims=True) + RMS_EPS) * subln_ref[...] * (1.0 - lam_init)
    o_ref[0, 0] = o.astype(o_ref.dtype)


def _diff_attention(q, k, v, slopes, pos_row, lq1, lk1, lq2, lk2, subln, lam_init):
    b, nh, s_len, d = q.shape
    blk = ATT_BLOCK
    smem = pl.BlockSpec(memory_space=pltpu.SMEM)
    vec = lambda n: pl.BlockSpec((1, n), lambda bb, h, i: (0, 0))
    return pl.pallas_call(
        functools.partial(_diff_kernel, blk=blk, lam_init=lam_init),
        grid=(b, nh, s_len // blk),
        in_specs=[smem, _head_spec(blk, d), _full_spec(s_len, d), _full_spec(s_len, LANES),
                  pl.BlockSpec((1, s_len), lambda bb, h, i: (0, 0)),
                  vec(DF_DIM), vec(DF_DIM), vec(DF_DIM), vec(DF_DIM), vec(DF_VDIM)],
        out_specs=_head_spec(blk, DF_VDIM),
        out_shape=jax.ShapeDtypeStruct((b, nh, s_len, DF_VDIM), BF16),
        compiler_params=_cparams(("arbitrary",) * 3),
        name="diff",
    )(slopes, q, k, v, pos_row, lq1, lk1, lq2, lk2, subln)


def _layer_norm(v, g, b):
    mu = jnp.mean(v, axis=-1, keepdims=True)
    c = v - mu
    var = jnp.mean(c * c, axis=-1, keepdims=True)
    return c * lax.rsqrt(var + LN_EPS) * g + b


def _sigmoid(z):
    return 1.0 / (1.0 + jnp.exp(-z))


def _route(logits_t, bias_col):
    scores = _sigmoid(logits_t)
    biased = scores + bias_col
    ng = N_GROUPS
    sc = [scores[j * ng:(j + 1) * ng] for j in range(EXPERTS_PER_GROUP)]
    bs = [biased[j * ng:(j + 1) * ng] for j in range(EXPERTS_PER_GROUP)]
    gscore = None
    for a in range(EXPERTS_PER_GROUP):
        for c in range(a + 1, EXPERTS_PER_GROUP):
            pair = bs[a] + bs[c]
            gscore = pair if gscore is None else jnp.maximum(gscore, pair)
    gid = lax.broadcasted_iota(jnp.int32, gscore.shape, 0)
    best = jnp.max(gscore, axis=0, keepdims=True)
    gsel = jnp.min(jnp.where(gscore == best, gid, ng), axis=0, keepdims=True)
    onehot = gid == gsel
    pick = lambda a: jnp.sum(jnp.where(onehot, a, 0.0), axis=0, keepdims=True)
    b_in = [pick(a) for a in bs]
    s_in = [pick(a) for a in sc]
    chosen = []
    for j in range(EXPERTS_PER_GROUP):
        ahead = jnp.zeros_like(b_in[j])
        for c in range(EXPERTS_PER_GROUP):
            if c == j:
                continue
            beats = (b_in[c] > b_in[j]) | ((b_in[c] == b_in[j]) & (c < j))
            ahead = ahead + beats.astype(F32)
        chosen.append(jnp.where(ahead < 2.0, s_in[j], 0.0))
    total = chosen[0] + chosen[1] + chosen[2] + chosen[3]
    gates = [c / total for c in chosen]
    return gates, onehot


def _merge_kernel(x_ref, oa_ref, ob_ref, oc_ref, od_ref, wg_ref, bg_ref, wa_ref, wb_ref, wc_ref, wd_ref,
                  wo_ref, g_ref, b_ref, wr_ref, rb_ref, x1_ref, gate_ref, *, alpha):
    tm = x_ref.shape[1]
    x = x_ref[0]
    xb = x.astype(BF16)
    merged = None
    for n, (o_ref, w_ref) in enumerate(((oa_ref, wa_ref), (ob_ref, wb_ref), (oc_ref, wc_ref), (od_ref, wd_ref))):
        cols = slice(n * D_MODEL, (n + 1) * D_MODEL)
        gate = _sigmoid(jnp.dot(xb, wg_ref[:, cols], preferred_element_type=F32) + bg_ref[:, cols])
        heads = jnp.concatenate([o_ref[0, hd] for hd in range(o_ref.shape[1])], axis=1)
        term = gate * jnp.dot(heads, w_ref[...], preferred_element_type=F32)
        merged = term if merged is None else merged + term
    mix = jnp.dot(merged.astype(BF16), wo_ref[...], preferred_element_type=F32)
    y = _layer_norm(alpha * x + mix, g_ref[...], b_ref[...])
    x1_ref[0] = y

    logits_t = _nt_dot(wr_ref[...], y, precision=lax.Precision.HIGHEST)
    gates, onehot = _route(logits_t, rb_ref[...])
    rows = [jnp.where(onehot, gate, 0.0) for gate in gates]
    rows.append(jnp.zeros((LANES - N_EXPERTS, tm), F32))
    gate_ref[0] = jnp.concatenate(rows, axis=0).T


def _merge(x, oa, ob, oc, od, wg, bg, wa, wb, wc, wd, wo, g, b, wr_t, rb_col, alpha):
    bsz, s_len, _ = x.shape
    tm = TOKEN_BLOCK
    row = lambda w: pl.BlockSpec((1, tm, w), lambda bb, i: (bb, i, 0))
    heads = lambda o: pl.BlockSpec((1, o.shape[1], tm, o.shape[3]), lambda bb, i: (bb, 0, i, 0))
    consts = (wg, bg, wa, wb, wc, wd, wo, g, b, wr_t, rb_col)
    return pl.pallas_call(
        functools.partial(_merge_kernel, alpha=alpha),
        grid=(bsz, s_len // tm),
        in_specs=[row(D_MODEL), heads(oa), heads(ob), heads(oc), heads(od)]
                 + [_const_spec(c.shape) for c in consts],
        out_specs=[row(D_MODEL), row(LANES)],
        out_shape=[jax.ShapeDtypeStruct((bsz, s_len, D_MODEL), F32),
                   jax.ShapeDtypeStruct((bsz, s_len, LANES), F32)],
        compiler_params=_cparams(("arbitrary", "arbitrary")),
        name="merge",
    )(x, oa, ob, oc, od, *consts)


def _moe_kernel(x_ref, gate_ref, w1_ref, w3_ref, w2_ref, g_ref, b_ref, o_ref, acc_ref, *, alpha):
    grp = pl.program_id(1)

    @pl.when(grp == 0)
    def _():
        acc_ref[...] = jnp.zeros_like(acc_ref)

    xb = x_ref[...].astype(BF16)
    gates = gate_ref[...]
    lane = lax.broadcasted_iota(jnp.int32, gates.shape, 1)
    hid = []
    for e in range(EXPERTS_PER_GROUP):
        h1 = jnp.dot(xb, w1_ref[e], preferred_element_type=F32)
        h3 = jnp.dot(xb, w3_ref[e], preferred_element_type=F32)
        gate = jnp.sum(jnp.where(lane == e * N_GROUPS + grp, gates, 0.0), axis=1, keepdims=True)
        hid.append((h1 * _sigmoid(h1) * h3 * gate).astype(BF16))
    w2 = w2_ref[...].reshape(EXPERTS_PER_GROUP * D_EXPERT, D_MODEL)
    acc_ref[...] += jnp.dot(jnp.concatenate(hid, axis=1), w2, preferred_element_type=F32)

    @pl.when(grp == N_GROUPS - 1)
    def _():
        o_ref[...] = _layer_norm(alpha * x_ref[...] + acc_ref[...], g_ref[...], b_ref[...])


def _moe(x1, gates, w1, w3, w2, g, b, alpha):
    t = x1.shape[0]
    tm = MOE_ROWS
    epg = EXPERTS_PER_GROUP
    return pl.pallas_call(
        functools.partial(_moe_kernel, alpha=alpha),
        grid=(t // tm, N_GROUPS),
        in_specs=[pl.BlockSpec((tm, D_MODEL), lambda i, e: (i, 0)),
                  pl.BlockSpec((tm, LANES), lambda i, e: (i, 0)),
                  pl.BlockSpec((epg, D_MODEL, D_EXPERT), lambda i, e: (e, 0, 0)),
                  pl.BlockSpec((epg, D_MODEL, D_EXPERT), lambda i, e: (e, 0, 0)),
                  pl.BlockSpec((epg, D_EXPERT, D_MODEL), lambda i, e: (e, 0, 0)),
                  pl.BlockSpec((1, D_MODEL), lambda i, e: (0, 0)),
                  pl.BlockSpec((1, D_MODEL), lambda i, e: (0, 0))],
        out_specs=pl.BlockSpec((tm, D_MODEL), lambda i, e: (i, 0)),
        out_shape=jax.ShapeDtypeStruct((t, D_MODEL), F32),
        scratch_shapes=[pltpu.VMEM((tm, D_MODEL), F32)],
        compiler_params=_cparams(("arbitrary", "arbitrary")),
        name="moe",
    )(x1, gates, w1, w3, w2, g, b)


def _alibi_slopes(n_heads):
    return 2.0 ** (-8.0 * jnp.arange(1, n_heads + 1, dtype=F32) / n_heads)


def _rope_tables(positions):
    half = MLA_ROPE // 2
    inv_freq = ROPE_THETA ** (-jnp.arange(half, dtype=F32) / half)
    ang = positions.astype(F32)[:, None] * inv_freq[None, :]
    cos = jnp.cos(ang)
    sin = jnp.sin(ang)
    return jnp.concatenate([cos, cos], axis=1), jnp.concatenate([sin, sin], axis=1)


def _rotate_half_columns(w):
    half = w.shape[-1] // 2
    return jnp.concatenate([-w[..., half:], w[..., :half]], axis=-1)


def _prep_w_in(w_in):
    split_at = [int(c) for c in np.cumsum(IN_SPLITS)[:-1]]
    parts = jnp.split(w_in, split_at, axis=-1)
    c_q, c_kv, k_rope = parts[0], parts[1], parts[2]
    before = jnp.zeros((w_in.shape[0], MLA_NOPE), w_in.dtype)
    after = jnp.zeros((w_in.shape[0], LANES - MLA_QK), w_in.dtype)
    return jnp.concatenate([c_q, c_kv, before, k_rope, after, before, _rotate_half_columns(k_rope), after]
                           + list(parts[3:]), axis=-1).astype(BF16)


def _pad_heads(w, n_heads, width):
    w3 = w.reshape(w.shape[0], n_heads, width)
    return jnp.pad(w3, ((0, 0), (0, 0), (0, LANES - width))).reshape(w.shape[0], n_heads * LANES)


def _prep_w_uq(w_uq):
    w = w_uq.reshape(MLA_Q_RANK, MLA_HEADS, MLA_QK)
    rot = jnp.concatenate([jnp.zeros_like(w[..., :MLA_NOPE]), _rotate_half_columns(w[..., MLA_NOPE:])], axis=-1)
    return (_pad_heads(w_uq, MLA_HEADS, MLA_QK).astype(BF16),
            _pad_heads(rot.reshape(w_uq.shape), MLA_HEADS, MLA_QK).astype(BF16))


def _prep_w_ukv(w_ukv):
    w = w_ukv.reshape(MLA_KV_RANK, MLA_HEADS, MLA_NOPE + MLA_V)
    wk = w[..., :MLA_NOPE].reshape(MLA_KV_RANK, MLA_HEADS * MLA_NOPE)
    wv = w[..., MLA_NOPE:].reshape(MLA_KV_RANK, MLA_HEADS * MLA_V)
    return _pad_heads(wk, MLA_HEADS, MLA_NOPE).astype(BF16), _pad_heads(wv, MLA_HEADS, MLA_V).astype(BF16)


def kernel(x, positions, w_in, mla_q_norm, mla_kv_norm, w_uq, w_ukv, sw_sinks, df_lq1, df_lk1, df_lq2, df_lk2,
           df_subln, w_gate, b_gate, w_br_a, w_br_b, w_br_c, w_br_d, w_out, ln1_g, ln1_b, w_router, router_bias,
           moe_w1, moe_w3, moe_w2, ln2_g, ln2_b):
    b, s_len, d = x.shape
    depth = w_in.shape[0]
    t = b * s_len
    alpha = (2 * depth) ** 0.25
    assert d == D_MODEL and s_len % TOKEN_BLOCK == 0 and s_len % ATT_BLOCK == 0 and t % MOE_ROWS == 0
    assert SB_DIM == SW_DIM == DF_VDIM == 2 * DF_DIM == HEAD_WIDTH == LANES // 2
    assert ATT_BLOCK % SW_WINDOW == 0

    posf = positions.astype(F32)
    pos_col = posf[:, None]
    pos_row = posf[None, :]
    cos, sin = _rope_tables(positions)
    q_scale = MLA_QK ** -0.5
    ones_n = jnp.ones((s_len, MLA_NOPE), F32)
    tail = jnp.zeros((s_len, LANES - MLA_QK), F32)
    cq = jnp.tile(jnp.concatenate([ones_n * q_scale, cos * q_scale, tail], axis=1), (1, MLA_HEADS))
    sq = jnp.tile(jnp.concatenate([0.0 * ones_n, sin * q_scale, tail], axis=1), (1, MLA_HEADS))
    ck = jnp.concatenate([0.0 * ones_n, cos, tail], axis=1)
    sk = jnp.concatenate([0.0 * ones_n, sin, tail], axis=1)
    rscale = jnp.concatenate([
        jnp.full((B_WIDTH,), SB_DIM ** -0.5, F32), jnp.ones((2 * B_WIDTH,), F32),
        jnp.full((C_WIDTH,), SW_DIM ** -0.5, F32), jnp.ones((2 * SW_KV_HEADS * SW_DIM,), F32),
        jnp.full((2 * DF_HEADS * DF_DIM,), DF_DIM ** -0.5, F32), jnp.ones((2 * DF_HEADS * DF_DIM + D_WIDTH,), F32),
    ])[None, :]
    sw_slopes = _alibi_slopes(SW_Q_HEADS)
    df_slopes = _alibi_slopes(DF_HEADS)

    perm = np.arange(N_EXPERTS).reshape(N_GROUPS, EXPERTS_PER_GROUP).T.reshape(-1)
    wr_t = w_router.astype(F32).T[perm]
    rb_col = router_bias.astype(F32)[perm][:, None]

    x3 = x
    for l in range(depth):
        lam_init = 0.8 - 0.6 * math.exp(-0.3 * l)
        wuq, wuqs = _prep_w_uq(w_uq[l])
        wuk, wuv = _prep_w_ukv(w_ukv[l])
        (mla_q, mla_k, mla_v, sb_q, sb_k, sb_v, sw_q, sw_k, sw_v, df_q, df_k, df_v) = _proj(
            x3, _prep_w_in(w_in[l]), mla_q_norm[l][None, :], mla_kv_norm[l][None, :],
            wuq, wuqs, wuk, wuv, cq, sq, ck, sk, rscale)

        o_a = _mla_attention(mla_q, mla_k, mla_v)
        o_b = _sb_attention(sb_q, sb_k, sb_v)
        o_c = _swa_attention(sw_q, sw_k, sw_v, sw_slopes, sw_sinks[l].astype(F32), pos_col, pos_row)
        o_d = _diff_attention(df_q, df_k, df_v, df_slopes, pos_row,
                              df_lq1[l][None, :], df_lk1[l][None, :], df_lq2[l][None, :],
                              df_lk2[l][None, :], df_subln[l][None, :], lam_init)

        x1, gates = _merge(x3, o_a, o_b, o_c, o_d, w_gate[l].astype(BF16), b_gate[l][None, :],
                           w_br_a[l].astype(BF16), w_br_b[l].astype(BF16), w_br_c[l].astype(BF16),
                           w_br_d[l].astype(BF16), w_out[l].astype(BF16), ln1_g[l][None, :], ln1_b[l][None, :],
                           wr_t, rb_col, alpha)
        x2 = _moe(x1.reshape(t, d), gates.reshape(t, LANES), moe_w1[l].astype(BF16), moe_w3[l].astype(BF16),
                  moe_w2[l].astype(BF16), ln2_g[l][None, :], ln2_b[l][None, :], alpha)
        x3 = x2.reshape(b, s_len, d)
    return x3
```

```python
import functools
import math

import numpy as np
import jax
import jax.numpy as jnp
from jax import lax
from jax.experimental import pallas as pl
from jax.experimental.pallas import tpu as pltpu

F32 = jnp.float32
BF16 = jnp.bfloat16

D_MODEL = 1024
NEG_INF = -1e30
LN_EPS = 1e-5
RMS_EPS = 1e-6

MLA_HEADS = 4
MLA_Q_RANK = 256
MLA_KV_RANK = 128
MLA_NOPE = 64
MLA_ROPE = 32
MLA_V = 64
MLA_QK = MLA_NOPE + MLA_ROPE
ROPE_THETA = 10000.0

SB_HEADS = 4
SB_DIM = 64

SW_Q_HEADS = 8
SW_KV_HEADS = 2
SW_DIM = 64
SW_WINDOW = 128
SW_GROUP = SW_Q_HEADS // SW_KV_HEADS

DF_HEADS = 4
DF_DIM = 32
DF_VDIM = 2 * DF_DIM

A_WIDTH = MLA_HEADS * MLA_V
B_WIDTH = SB_HEADS * SB_DIM
C_WIDTH = SW_Q_HEADS * SW_DIM
D_WIDTH = DF_HEADS * DF_VDIM
N_BRANCH = 4

IN_SPLITS = (MLA_Q_RANK, MLA_KV_RANK, MLA_ROPE, 3 * B_WIDTH, C_WIDTH, 2 * SW_KV_HEADS * SW_DIM,
             2 * DF_HEADS * DF_DIM, 2 * DF_HEADS * DF_DIM, D_WIDTH)

N_EXPERTS = 32
N_GROUPS = 8
EXPERTS_PER_GROUP = N_EXPERTS // N_GROUPS
D_EXPERT = 256

LANES = 128
HEAD_WIDTH = 64

OFF_CQ = 0
OFF_CKV = OFF_CQ + MLA_Q_RANK
OFF_KR = OFF_CKV + MLA_KV_RANK
OFF_KRS = OFF_KR + LANES
OFF_REST = OFF_KRS + LANES
REST_WIDTH = sum(IN_SPLITS[3:])
IN_PAD_WIDTH = OFF_REST + REST_WIDTH

TOKEN_BLOCK = 512
ATT_BLOCK = 512
SUFFIX_BLOCK = 256
EXP_UNDERFLOW = -104.0
MOE_ROWS = 1024
VMEM_LIMIT = 56 * 1024 * 1024


def _cparams(sem):
    return pltpu.CompilerParams(dimension_semantics=sem, vmem_limit_bytes=VMEM_LIMIT)


def _nt_dot(a, b, **kw):
    return lax.dot_general(a, b, (((1,), (1,)), ((), ())), preferred_element_type=F32, **kw)


def _const_spec(shape):
    nd = len(shape)
    return pl.BlockSpec(shape, lambda *_: (0,) * nd)


def _split_heads(val, out_ref, n_heads, width, ones_lane=None):
    for h in range(n_heads):
        tile = val[:, (h // 2) * LANES:(h // 2 + 1) * LANES]
        if h % 2:
            tile = pltpu.roll(tile, LANES - width, axis=1)
        if ones_lane is None:
            out_ref[0, h] = tile[:, :width].astype(out_ref.dtype)
        else:
            keep = lax.broadcasted_iota(jnp.int32, tile.shape, 1) < width
            out_ref[0, h] = (jnp.where(keep, tile, 0.0) + ones_lane).astype(out_ref.dtype)


def _proj_kernel(x_ref, win_ref, qn_ref, kvn_ref, wuq_ref, wuqs_ref, wuk_ref, wuv_ref,
                 cq_ref, sq_ref, ck_ref, sk_ref, rscale_ref,
                 mq_ref, mk_ref, mv_ref, sbq_ref, sbk_ref, sbv_ref, swq_ref, swk_ref, swv_ref,
                 dfq_ref, dfk_ref, dfv_ref):
    xb = x_ref[0].astype(BF16)
    h = jnp.dot(xb, win_ref[...], preferred_element_type=F32)
    c_q = h[:, OFF_CQ:OFF_CQ + MLA_Q_RANK]
    c_kv = h[:, OFF_CKV:OFF_CKV + MLA_KV_RANK]
    nq = c_q * lax.rsqrt(jnp.mean(c_q * c_q, axis=-1, keepdims=True) + RMS_EPS) * qn_ref[...]
    nqb = nq.astype(BF16)
    q = jnp.dot(nqb, wuq_ref[...], preferred_element_type=F32)
    qs = jnp.dot(nqb, wuqs_ref[...], preferred_element_type=F32)
    q = q * cq_ref[...] + qs * sq_ref[...]
    nkv = c_kv * lax.rsqrt(jnp.mean(c_kv * c_kv, axis=-1, keepdims=True) + RMS_EPS) * kvn_ref[...]
    nkvb = nkv.astype(BF16)
    kn = jnp.dot(nkvb, wuk_ref[...], preferred_element_type=F32)
    vv = jnp.dot(nkvb, wuv_ref[...], preferred_element_type=F32)
    kr = h[:, OFF_KR:OFF_KR + LANES] * ck_ref[...] + h[:, OFF_KRS:OFF_KRS + LANES] * sk_ref[...]
    ones_lane = (lax.broadcasted_iota(jnp.int32, (1, LANES), 1) == HEAD_WIDTH).astype(F32)
    for hd in range(MLA_HEADS):
        lanes = slice(hd * LANES, (hd + 1) * LANES)
        mq_ref[0, hd] = q[:, lanes][:, :MLA_QK].astype(mq_ref.dtype)
        mk_ref[0, hd] = (kn[:, lanes] + kr)[:, :MLA_QK].astype(mk_ref.dtype)
        mv_ref[0, hd] = (vv[:, lanes] + ones_lane).astype(mv_ref.dtype)
    rest = h[:, OFF_REST:] * rscale_ref[...]
    o = 0
    for ref, nh in ((sbq_ref, SB_HEADS), (sbk_ref, SB_HEADS), (sbv_ref, SB_HEADS),
                    (swq_ref, SW_Q_HEADS), (swk_ref, SW_KV_HEADS), (swv_ref, SW_KV_HEADS),
                    (dfq_ref, DF_HEADS), (dfk_ref, DF_HEADS), (dfv_ref, DF_HEADS)):
        _split_heads(rest[:, o:o + nh * HEAD_WIDTH], ref, nh, HEAD_WIDTH,
                     ones_lane if ref is dfv_ref else None)
        o += nh * HEAD_WIDTH


def _proj(x, win, qn, kvn, wuq, wuqs, wuk, wuv, cq, sq, ck, sk, rscale):
    b, s_len, _ = x.shape
    tm = TOKEN_BLOCK
    tab = lambda w: pl.BlockSpec((tm, w), lambda bb, i: (i, 0))
    heads = lambda nh, d: pl.BlockSpec((1, nh, tm, d), lambda bb, i: (bb, 0, i, 0))
    shape = lambda nh, d: jax.ShapeDtypeStruct((b, nh, s_len, d), BF16)
    outs = [(MLA_HEADS, MLA_QK), (MLA_HEADS, MLA_QK), (MLA_HEADS, LANES)] + [(SB_HEADS, SB_DIM)] * 3 + [
        (SW_Q_HEADS, SW_DIM), (SW_KV_HEADS, SW_DIM), (SW_KV_HEADS, SW_DIM)] + [
        (DF_HEADS, HEAD_WIDTH), (DF_HEADS, HEAD_WIDTH), (DF_HEADS, LANES)]
    consts = (win, qn, kvn, wuq, wuqs, wuk, wuv)
    return pl.pallas_call(
        _proj_kernel,
        grid=(b, s_len // tm),
        in_specs=[pl.BlockSpec((1, tm, D_MODEL), lambda bb, i: (bb, i, 0))]
                 + [_const_spec(c.shape) for c in consts]
                 + [tab(cq.shape[1]), tab(sq.shape[1]), tab(LANES), tab(LANES), _const_spec(rscale.shape)],
        out_specs=[heads(nh, d) for nh, d in outs],
        out_shape=[shape(nh, d) for nh, d in outs],
        compiler_params=_cparams(("arbitrary", "arbitrary")),
        name="proj",
    )(x, *consts, cq, sq, ck, sk, rscale)


def _causal_mask(n):
    row = lax.broadcasted_iota(jnp.int32, (n, n), 0)
    col = lax.broadcasted_iota(jnp.int32, (n, n), 1)
    return row, col


def _softmax_step(s, v_ones, m, acc):
    m_new = jnp.maximum(m, jnp.max(s, axis=-1, keepdims=True))
    p = jnp.exp((s - m_new).astype(BF16))
    alpha = jnp.exp(m - m_new)
    return m_new, alpha * acc + jnp.dot(p, v_ones, preferred_element_type=F32)


def _normalised(acc, width):
    return acc[:, :width] / acc[:, width:width + 1]


def _walk_full_blocks(n_blocks, blk, block, carry):
    carry = lax.fori_loop(0, n_blocks // 2, lambda j, c: block(2 * j, 2 * blk, c), carry)
    return lax.fori_loop(0, n_blocks % 2, lambda _, c: block(n_blocks - 1, blk, c), carry)


def _mla_kernel(q_ref, k_ref, v_ref, o_ref, *, blk):
    qi = pl.program_id(2)
    q = q_ref[0, 0]

    def kv_block(ki, width=blk):
        start = pl.multiple_of(ki * blk, blk)
        return k_ref[0, 0, pl.ds(start, width), :], v_ref[0, 0, pl.ds(start, width), :]

    def block(ki, width, carry):
        k, v = kv_block(ki, width)
        return _softmax_step(_nt_dot(q, k), v, *carry)

    init = (jnp.full((blk, 1), NEG_INF, F32), jnp.zeros((blk, LANES), F32))
    carry = _walk_full_blocks(qi, blk, block, init)
    k, v = kv_block(qi)
    row, col = _causal_mask(blk)
    s = jnp.where(col <= row, _nt_dot(q, k), NEG_INF)
    _, acc = _softmax_step(s, v, *carry)
    o_ref[0, 0] = _normalised(acc, MLA_V).astype(o_ref.dtype)


def _head_spec(blk, d):
    return pl.BlockSpec((1, 1, blk, d), lambda b, h, i: (b, h, i, 0))


def _full_spec(s_len, d):
    return pl.BlockSpec((1, 1, s_len, d), lambda b, h, i: (b, h, 0, 0))


def _mla_attention(q, k, v):
    b, nh, s_len, _ = q.shape
    blk = ATT_BLOCK
    return pl.pallas_call(
        functools.partial(_mla_kernel, blk=blk),
        grid=(b, nh, s_len // blk),
        in_specs=[_head_spec(blk, MLA_QK), _full_spec(s_len, MLA_QK), _full_spec(s_len, LANES)],
        out_specs=_head_spec(blk, MLA_V),
        out_shape=jax.ShapeDtypeStruct((b, nh, s_len, MLA_V), BF16),
        compiler_params=_cparams(("arbitrary",) * 3),
        name="mla",
    )(q, k, v)


def _log_sigmoid(z):
    return jnp.minimum(z, 0.0) - jnp.log(1.0 + jnp.exp(-jnp.abs(z)))


def _suffix_sum(l1m, upper):
    sub = upper.shape[0]
    n_sub = l1m.shape[1] // sub
    pieces = [None] * n_sub
    right = None
    for c in reversed(range(n_sub)):
        chunk = l1m[:, c * sub:(c + 1) * sub]
        hi = chunk.astype(BF16)
        lo = (chunk - hi.astype(F32)).astype(BF16)
        inner = jnp.dot(hi, upper, preferred_element_type=F32) + jnp.dot(lo, upper, preferred_element_type=F32)
        total = inner[:, 0:1] + chunk[:, 0:1]
        if right is None:
            pieces[c], right = inner, total
        else:
            pieces[c], right = inner + right, right + total
    return jnp.concatenate(pieces, axis=1), right


def _sb_kernel(q_ref, k_ref, v_ref, o_ref, *, blk):
    qi = pl.program_id(2)
    q = q_ref[0, 0]
    srow, scol = _causal_mask(SUFFIX_BLOCK)
    upper = (srow > scol).astype(BF16)

    def kv_block(ki):
        start = pl.multiple_of(ki * blk, blk)
        return k_ref[0, 0, pl.ds(start, blk), :], v_ref[0, 0, pl.ds(start, blk), :]

    row, col = _causal_mask(blk)
    k, v = kv_block(qi)
    z = _nt_dot(q, k)
    mask = col < row
    lb = _log_sigmoid(z)
    l1m = jnp.where(mask, lb - z, 0.0)
    suffix, passed = _suffix_sum(l1m, upper)
    a = jnp.where(mask, jnp.exp(lb + suffix), 0.0)
    acc = jnp.dot(a.astype(BF16), v, preferred_element_type=F32)

    def live(carry):
        j, passed, _ = carry
        return (j < qi) & (jnp.max(passed) >= EXP_UNDERFLOW)

    def body(carry):
        j, passed, acc = carry
        k, v = kv_block(qi - 1 - j)
        z = _nt_dot(q, k)
        lb = _log_sigmoid(z)
        l1m = lb - z
        suffix, total = _suffix_sum(l1m, upper)
        a = jnp.exp(lb + suffix + passed)
        acc = acc + jnp.dot(a.astype(BF16), v, preferred_element_type=F32)
        return j + 1, passed + total, acc

    _, _, acc = lax.while_loop(live, body, (jnp.int32(0), passed, acc))
    o_ref[0, 0] = acc.astype(o_ref.dtype)


def _sb_attention(q, k, v):
    b, nh, s_len, d = q.shape
    blk = ATT_BLOCK
    return pl.pallas_call(
        functools.partial(_sb_kernel, blk=blk),
        grid=(b, nh, s_len // blk),
        in_specs=[_head_spec(blk, d), _full_spec(s_len, d), _full_spec(s_len, d)],
        out_specs=_head_spec(blk, d),
        out_shape=jax.ShapeDtypeStruct((b, nh, s_len, d), BF16),
        compiler_params=_cparams(("arbitrary",) * 3),
        name="sb",
    )(q, k, v)


def _swa_kernel(slopes_ref, sinks_ref, q_ref, kp_ref, kc_ref, vp_ref, vc_ref,
                pq_ref, pkp_ref, pkc_ref, o_ref):
    g = pl.program_id(1)
    i = pl.program_id(2)
    w = SW_WINDOW
    q = q_ref[0].reshape(SW_GROUP * w, SW_DIM)
    k = jnp.concatenate([kp_ref[0, 0], kc_ref[0, 0]], axis=0)
    v = jnp.concatenate([vp_ref[0, 0], vc_ref[0, 0]], axis=0)
    s = _nt_dot(q, k)
    pk = jnp.concatenate([pkp_ref[...], pkc_ref[...]], axis=1)
    dist = pq_ref[...] - pk
    qi = lax.broadcasted_iota(jnp.int32, (w, 2 * w), 0)
    kj = lax.broadcasted_iota(jnp.int32, (w, 2 * w), 1)
    mask = (kj > qi) & (kj <= qi + w) & ((kj >= w) | (i > 0))
    for j in range(SW_GROUP):
        head = g * SW_GROUP + j
        sj = jnp.where(mask, s[j * w:(j + 1) * w] - slopes_ref[head] * dist, NEG_INF)
        sink = sinks_ref[head]
        m = jnp.maximum(jnp.max(sj, axis=-1, keepdims=True), sink)
        p = jnp.exp(sj - m)
        denom = jnp.sum(p, axis=-1, keepdims=True) + jnp.exp(sink - m)
        o = jnp.dot(p.astype(BF16), v, preferred_element_type=F32) / denom
        o_ref[0, j] = o.astype(o_ref.dtype)


def _swa_attention(q, k, v, slopes, sinks, pos_col, pos_row):
    b, _, s_len, d = q.shape
    w = SW_WINDOW
    prev = lambda i: jnp.maximum(i - 1, 0)
    smem = pl.BlockSpec(memory_space=pltpu.SMEM)
    kv_prev = pl.BlockSpec((1, 1, w, d), lambda bb, g, i: (bb, g, prev(i), 0))
    kv_cur = pl.BlockSpec((1, 1, w, d), lambda bb, g, i: (bb, g, i, 0))
    return pl.pallas_call(
        _swa_kernel,
        grid=(b, SW_KV_HEADS, s_len // w),
        in_specs=[smem, smem,
                  pl.BlockSpec((1, SW_GROUP, w, d), lambda bb, g, i: (bb, g, i, 0)),
                  kv_prev, kv_cur, kv_prev, kv_cur,
                  pl.BlockSpec((w, 1), lambda bb, g, i: (i, 0)),
                  pl.BlockSpec((1, w), lambda bb, g, i: (0, prev(i))),
                  pl.BlockSpec((1, w), lambda bb, g, i: (0, i))],
        out_specs=pl.BlockSpec((1, SW_GROUP, w, d), lambda bb, g, i: (bb, g, i, 0)),
        out_shape=jax.ShapeDtypeStruct((b, SW_Q_HEADS, s_len, d), BF16),
        compiler_params=_cparams(("arbitrary",) * 3),
        name="swa",
    )(slopes, sinks, q, k, k, v, v, pos_col, pos_row, pos_row)


def _diff_kernel(slopes_ref, q_ref, k_ref, v_ref, pk_ref, lq1_ref, lk1_ref, lq2_ref, lk2_ref,
                 subln_ref, o_ref, *, blk, lam_init):
    h = pl.program_id(1)
    qi = pl.program_id(2)
    q = q_ref[0, 0]
    lane = lax.broadcasted_iota(jnp.int32, q.shape, 1)
    zero = jnp.zeros_like(q)
    qq = jnp.concatenate([jnp.where(lane < DF_DIM, q, zero), jnp.where(lane >= DF_DIM, q, zero)], axis=0)
    slope = slopes_ref[h]

    def block(ki, width=blk):
        start = pl.multiple_of(ki * blk, blk)
        k = k_ref[0, 0, pl.ds(start, width), :]
        v = v_ref[0, 0, pl.ds(start, width), :]
        return _nt_dot(qq, k) + slope * pk_ref[:, pl.ds(start, width)], v

    def step(ki, width, carry):
        s, v = block(ki, width)
        return _softmax_step(s, v, *carry)

    init = (jnp.full((2 * blk, 1), NEG_INF, F32), jnp.zeros((2 * blk, LANES), F32))
    carry = _walk_full_blocks(qi, blk, step, init)
    s, v = block(qi)
    row = lax.broadcasted_iota(jnp.int32, (2 * blk, blk), 0)
    col = lax.broadcasted_iota(jnp.int32, (2 * blk, blk), 1)
    causal = (col <= row) & ((row < blk) | (col <= row - blk))
    _, acc = _softmax_step(jnp.where(causal, s, NEG_INF), v, *carry)
    lam = (jnp.exp(jnp.sum(lq1_ref[...] * lk1_ref[...], keepdims=True))
           - jnp.exp(jnp.sum(lq2_ref[...] * lk2_ref[...], keepdims=True)) + lam_init)
    o = _normalised(acc, DF_VDIM)
    o = o[:blk] - lam * o[blk:]
    o = o * lax.rsqrt(jnp.mean(o * o, axis=-1, keepdims=True) + RMS_EPS) * subln_ref[...] * (1.0 - lam_init)
    o_ref[0, 0] = o.astype(o_ref.dtype)


def _diff_attention(q, k, v, slopes, pos_row, lq1, lk1, lq2, lk2, subln, lam_init):
    b, nh, s_len, d = q.shape
    blk = ATT_BLOCK
    smem = pl.BlockSpec(memory_space=pltpu.SMEM)
    vec = lambda n: pl.BlockSpec((1, n), lambda bb, h, i: (0, 0))
    return pl.pallas_call(
        functools.partial(_diff_kernel, blk=blk, lam_init=lam_init),
        grid=(b, nh, s_len // blk),
        in_specs=[smem, _head_spec(blk, d), _full_spec(s_len, d), _full_spec(s_len, LANES),
                  pl.BlockSpec((1, s_len), lambda bb, h, i: (0, 0)),
                  vec(DF_DIM), vec(DF_DIM), vec(DF_DIM), vec(DF_DIM), vec(DF_VDIM)],
        out_specs=_head_spec(blk, DF_VDIM),
        out_shape=jax.ShapeDtypeStruct((b, nh, s_len, DF_VDIM), BF16),
        compiler_params=_cparams(("arbitrary",) * 3),
        name="diff",
    )(slopes, q, k, v, pos_row, lq1, lk1, lq2, lk2, subln)


def _layer_norm(v, g, b):
    mu = jnp.mean(v, axis=-1, keepdims=True)
    c = v - mu
    var = jnp.mean(c * c, axis=-1, keepdims=True)
    return c * lax.rsqrt(var + LN_EPS) * g + b


def _sigmoid(z):
    return 1.0 / (1.0 + jnp.exp(-z))


def _route(logits_t, bias_col):
    scores = _sigmoid(logits_t)
    biased = scores + bias_col
    ng = N_GROUPS
    sc = [scores[j * ng:(j + 1) * ng] for j in range(EXPERTS_PER_GROUP)]
    bs = [biased[j * ng:(j + 1) * ng] for j in range(EXPERTS_PER_GROUP)]
    gscore = None
    for a in range(EXPERTS_PER_GROUP):
        for c in range(a + 1, EXPERTS_PER_GROUP):
            pair = bs[a] + bs[c]
            gscore = pair if gscore is None else jnp.maximum(gscore, pair)
    gid = lax.broadcasted_iota(jnp.int32, gscore.shape, 0)
    best = jnp.max(gscore, axis=0, keepdims=True)
    gsel = jnp.min(jnp.where(gscore == best, gid, ng), axis=0, keepdims=True)
    onehot = gid == gsel
    pick = lambda a: jnp.sum(jnp.where(onehot, a, 0.0), axis=0, keepdims=True)
    b_in = [pick(a) for a in bs]
    s_in = [pick(a) for a in sc]
    chosen = []
    for j in range(EXPERTS_PER_GROUP):
        ahead = jnp.zeros_like(b_in[j])
        for c in range(EXPERTS_PER_GROUP):
            if c == j:
                continue
            beats = (b_in[c] > b_in[j]) | ((b_in[c] == b_in[j]) & (c < j))
            ahead = ahead + beats.astype(F32)
        chosen.append(jnp.where(ahead < 2.0, s_in[j], 0.0))
    total = chosen[0] + chosen[1] + chosen[2] + chosen[3]
    gates = [c / total for c in chosen]
    return gates, onehot


def _merge_kernel(x_ref, oa_ref, ob_ref, oc_ref, od_ref, wg_ref, bg_ref, wa_ref, wb_ref, wc_ref, wd_ref,
                  wo_ref, g_ref, b_ref, wr_ref, rb_ref, x1_ref, gate_ref, *, alpha):
    tm = x_ref.shape[1]
    x = x_ref[0]
    xb = x.astype(BF16)
    merged = None
    for n, (o_ref, w_ref) in enumerate(((oa_ref, wa_ref), (ob_ref, wb_ref), (oc_ref, wc_ref), (od_ref, wd_ref))):
        cols = slice(n * D_MODEL, (n + 1) * D_MODEL)
        gate = _sigmoid(jnp.dot(xb, wg_ref[:, cols], preferred_element_type=F32) + bg_ref[:, cols])
        heads = jnp.concatenate([o_ref[0, hd] for hd in range(o_ref.shape[1])], axis=1)
        term = gate * jnp.dot(heads, w_ref[...], preferred_element_type=F32)
        merged = term if merged is None else merged + term
    mix = jnp.dot(merged.astype(BF16), wo_ref[...], preferred_element_type=F32)
    y = _layer_norm(alpha * x + mix, g_ref[...], b_ref[...])
    x1_ref[0] = y

    logits_t = _nt_dot(wr_ref[...], y, precision=lax.Precision.HIGHEST)
    gates, onehot = _route(logits_t, rb_ref[...])
    rows = [jnp.where(onehot, gate, 0.0) for gate in gates]
    rows.append(jnp.zeros((LANES - N_EXPERTS, tm), F32))
    gate_ref[0] = jnp.concatenate(rows, axis=0).T


def _merge(x, oa, ob, oc, od, wg, bg, wa, wb, wc, wd, wo, g, b, wr_t, rb_col, alpha):
    bsz, s_len, _ = x.shape
    tm = TOKEN_BLOCK
    row = lambda w: pl.BlockSpec((1, tm, w), lambda bb, i: (bb, i, 0))
    heads = lambda o: pl.BlockSpec((1, o.shape[1], tm, o.shape[3]), lambda bb, i: (bb, 0, i, 0))
    consts = (wg, bg, wa, wb, wc, wd, wo, g, b, wr_t, rb_col)
    return pl.pallas_call(
        functools.partial(_merge_kernel, alpha=alpha),
        grid=(bsz, s_len // tm),
        in_specs=[row(D_MODEL), heads(oa), heads(ob), heads(oc), heads(od)]
                 + [_const_spec(c.shape) for c in consts],
        out_specs=[row(D_MODEL), row(LANES)],
        out_shape=[jax.ShapeDtypeStruct((bsz, s_len, D_MODEL), F32),
                   jax.ShapeDtypeStruct((bsz, s_len, LANES), F32)],
        compiler_params=_cparams(("arbitrary", "arbitrary")),
        name="merge",
    )(x, oa, ob, oc, od, *consts)


def _moe_kernel(x_ref, gate_ref, w1_ref, w3_ref, w2_ref, g_ref, b_ref, o_ref, acc_ref, *, alpha):
    grp = pl.program_id(1)

    @pl.when(grp == 0)
    def _():
        acc_ref[...] = jnp.zeros_like(acc_ref)

    xb = x_ref[...].astype(BF16)
    gates = gate_ref[...]
    lane = lax.broadcasted_iota(jnp.int32, gates.shape, 1)
    hid = []
    for e in range(EXPERTS_PER_GROUP):
        h1 = jnp.dot(xb, w1_ref[e], preferred_element_type=F32)
        h3 = jnp.dot(xb, w3_ref[e], preferred_element_type=F32)
        gate = jnp.sum(jnp.where(lane == e * N_GROUPS + grp, gates, 0.0), axis=1, keepdims=True)
        hid.append((h1 * _sigmoid(h1) * h3 * gate).astype(BF16))
    w2 = w2_ref[...].reshape(EXPERTS_PER_GROUP * D_EXPERT, D_MODEL)
    acc_ref[...] += jnp.dot(jnp.concatenate(hid, axis=1), w2, preferred_element_type=F32)

    @pl.when(grp == N_GROUPS - 1)
    def _():
        o_ref[...] = _layer_norm(alpha * x_ref[...] + acc_ref[...], g_ref[...], b_ref[...])


def _moe(x1, gates, w1, w3, w2, g, b, alpha):
    t = x1.shape[0]
    tm = MOE_ROWS
    epg = EXPERTS_PER_GROUP
    return pl.pallas_call(
        functools.partial(_moe_kernel, alpha=alpha),
        grid=(t // tm, N_GROUPS),
        in_specs=[pl.BlockSpec((tm, D_MODEL), lambda i, e: (i, 0)),
                  pl.BlockSpec((tm, LANES), lambda i, e: (i, 0)),
                  pl.BlockSpec((epg, D_MODEL, D_EXPERT), lambda i, e: (e, 0, 0)),
                  pl.BlockSpec((epg, D_MODEL, D_EXPERT), lambda i, e: (e, 0, 0)),
                  pl.BlockSpec((epg, D_EXPERT, D_MODEL), lambda i, e: (e, 0, 0)),
                  pl.BlockSpec((1, D_MODEL), lambda i, e: (0, 0)),
                  pl.BlockSpec((1, D_MODEL), lambda i, e: (0, 0))],
        out_specs=pl.BlockSpec((tm, D_MODEL), lambda i, e: (i, 0)),
        out_shape=jax.ShapeDtypeStruct((t, D_MODEL), F32),
        scratch_shapes=[pltpu.VMEM((tm, D_MODEL), F32)],
        compiler_params=_cparams(("arbitrary", "arbitrary")),
        name="moe",
    )(x1, gates, w1, w3, w2, g, b)


def _alibi_slopes(n_heads):
    return 2.0 ** (-8.0 * jnp.arange(1, n_heads + 1, dtype=F32) / n_heads)


def _rope_tables(positions):
    half = MLA_ROPE // 2
    inv_freq = ROPE_THETA ** (-jnp.arange(half, dtype=F32) / half)
    ang = positions.astype(F32)[:, None] * inv_freq[None, :]
    cos = jnp.cos(ang)
    sin = jnp.sin(ang)
    return jnp.concatenate([cos, cos], axis=1), jnp.concatenate([sin, sin], axis=1)


def _rotate_half_columns(w):
    half = w.shape[-1] // 2
    return jnp.concatenate([-w[..., half:], w[..., :half]], axis=-1)


def _prep_w_in(w_in):
    split_at = [int(c) for c in np.cumsum(IN_SPLITS)[:-1]]
    parts = jnp.split(w_in, split_at, axis=-1)
    c_q, c_kv, k_rope = parts[0], parts[1], parts[2]
    before = jnp.zeros((w_in.shape[0], MLA_NOPE), w_in.dtype)
    after = jnp.zeros((w_in.shape[0], LANES - MLA_QK), w_in.dtype)
    return jnp.concatenate([c_q, c_kv, before, k_rope, after, before, _rotate_half_columns(k_rope), after]
                           + list(parts[3:]), axis=-1).astype(BF16)


def _pad_heads(w, n_heads, width):
    w3 = w.reshape(w.shape[0], n_heads, width)
    return jnp.pad(w3, ((0, 0), (0, 0), (0, LANES - width))).reshape(w.shape[0], n_heads * LANES)


def _prep_w_uq(w_uq):
    w = w_uq.reshape(MLA_Q_RANK, MLA_HEADS, MLA_QK)
    rot = jnp.concatenate([jnp.zeros_like(w[..., :MLA_NOPE]), _rotate_half_columns(w[..., MLA_NOPE:])], axis=-1)
    return (_pad_heads(w_uq, MLA_HEADS, MLA_QK).astype(BF16),
            _pad_heads(rot.reshape(w_uq.shape), MLA_HEADS, MLA_QK).astype(BF16))


def _prep_w_ukv(w_ukv):
    w = w_ukv.reshape(MLA_KV_RANK, MLA_HEADS, MLA_NOPE + MLA_V)
    wk = w[..., :MLA_NOPE].reshape(MLA_KV_RANK, MLA_HEADS * MLA_NOPE)
    wv = w[..., MLA_NOPE:].reshape(MLA_KV_RANK, MLA_HEADS * MLA_V)
    return _pad_heads(wk, MLA_HEADS, MLA_NOPE).astype(BF16), _pad_heads(wv, MLA_HEADS, MLA_V).astype(BF16)


def kernel(x, positions, w_in, mla_q_norm, mla_kv_norm, w_uq, w_ukv, sw_sinks, df_lq1, df_lk1, df_lq2, df_lk2,
           df_subln, w_gate, b_gate, w_br_a, w_br_b, w_br_c, w_br_d, w_out, ln1_g, ln1_b, w_router, router_bias,
           moe_w1, moe_w3, moe_w2, ln2_g, ln2_b):
    b, s_len, d = x.shape
    depth = w_in.shape[0]
    t = b * s_len
    alpha = (2 * depth) ** 0.25
    assert d == D_MODEL and s_len % TOKEN_BLOCK == 0 and s_len % ATT_BLOCK == 0 and t % MOE_ROWS == 0
    assert SB_DIM == SW_DIM == DF_VDIM == 2 * DF_DIM == HEAD_WIDTH == LANES // 2
    assert ATT_BLOCK % SW_WINDOW == 0

    posf = positions.astype(F32)
    pos_col = posf[:, None]
    pos_row = posf[None, :]
    cos, sin = _rope_tables(positions)
    q_scale = MLA_QK ** -0.5
    ones_n = jnp.ones((s_len, MLA_NOPE), F32)
    tail = jnp.zeros((s_len, LANES - MLA_QK), F32)
    cq = jnp.tile(jnp.concatenate([ones_n * q_scale, cos * q_scale, tail], axis=1), (1, MLA_HEADS))
    sq = jnp.tile(jnp.concatenate([0.0 * ones_n, sin * q_scale, tail], axis=1), (1, MLA_HEADS))
    ck = jnp.concatenate([0.0 * ones_n, cos, tail], axis=1)
    sk = jnp.concatenate([0.0 * ones_n, sin, tail], axis=1)
    rscale = jnp.concatenate([
        jnp.full((B_WIDTH,), SB_DIM ** -0.5, F32), jnp.ones((2 * B_WIDTH,), F32),
        jnp.full((C_WIDTH,), SW_DIM ** -0.5, F32), jnp.ones((2 * SW_KV_HEADS * SW_DIM,), F32),
        jnp.full((2 * DF_HEADS * DF_DIM,), DF_DIM ** -0.5, F32), jnp.ones((2 * DF_HEADS * DF_DIM + D_WIDTH,), F32),
    ])[None, :]
    sw_slopes = _alibi_slopes(SW_Q_HEADS)
    df_slopes = _alibi_slopes(DF_HEADS)

    perm = np.arange(N_EXPERTS).reshape(N_GROUPS, EXPERTS_PER_GROUP).T.reshape(-1)
    wr_t = w_router.astype(F32).T[perm]
    rb_col = router_bias.astype(F32)[perm][:, None]

    x3 = x
    for l in range(depth):
        lam_init = 0.8 - 0.6 * math.exp(-0.3 * l)
        wuq, wuqs = _prep_w_uq(w_uq[l])
        wuk, wuv = _prep_w_ukv(w_ukv[l])
        (mla_q, mla_k, mla_v, sb_q, sb_k, sb_v, sw_q, sw_k, sw_v, df_q, df_k, df_v) = _proj(
            x3, _prep_w_in(w_in[l]), mla_q_norm[l][None, :], mla_kv_norm[l][None, :],
            wuq, wuqs, wuk, wuv, cq, sq, ck, sk, rscale)

        o_a = _mla_attention(mla_q, mla_k, mla_v)
        o_b = _sb_attention(sb_q, sb_k, sb_v)
        o_c = _swa_attention(sw_q, sw_k, sw_v, sw_slopes, sw_sinks[l].astype(F32), pos_col, pos_row)
        o_d = _diff_attention(df_q, df_k, df_v, df_slopes, pos_row,
                              df_lq1[l][None, :], df_lk1[l][None, :], df_lq2[l][None, :],
                              df_lk2[l][None, :], df_subln[l][None, :], lam_init)

        x1, gates = _merge(x3, o_a, o_b, o_c, o_d, w_gate[l].astype(BF16), b_gate[l][None, :],
                           w_br_a[l].astype(BF16), w_br_b[l].astype(BF16), w_br_c[l].astype(BF16),
                           w_br_d[l].astype(BF16), w_out[l].astype(BF16), ln1_g[l][None, :], ln1_b[l][None, :],
                           wr_t, rb_col, alpha)
        x2 = _moe(x1.reshape(t, d), gates.reshape(t, LANES), moe_w1[l].astype(BF16), moe_w3[l].astype(BF16),
                  moe_w2[l].astype(BF16), ln2_g[l][None, :], ln2_b[l][None, :], alpha)
        x3 = x2.reshape(b, s_len, d)
    return x3
```

```python
import functools
import math

import numpy as np
import jax
import jax.numpy as jnp
from jax import lax
from jax.experimental import pallas as pl
from jax.experimental.pallas import tpu as pltpu

F32 = jnp.float32
BF16 = jnp.bfloat16

D_MODEL = 1024
NEG_INF = -1e30
LN_EPS = 1e-5
RMS_EPS = 1e-6

MLA_HEADS = 4
MLA_Q_RANK = 256
MLA_KV_RANK = 128
MLA_NOPE = 64
MLA_ROPE = 32
MLA_V = 64
MLA_QK = MLA_NOPE + MLA_ROPE
ROPE_THETA = 10000.0

SB_HEADS = 4
SB_DIM = 64

SW_Q_HEADS = 8
SW_KV_HEADS = 2
SW_DIM = 64
SW_WINDOW = 128
SW_GROUP = SW_Q_HEADS // SW_KV_HEADS

DF_HEADS = 4
DF_DIM = 32
DF_VDIM = 2 * DF_DIM

A_WIDTH = MLA_HEADS * MLA_V
B_WIDTH = SB_HEADS * SB_DIM
C_WIDTH = SW_Q_HEADS * SW_DIM
D_WIDTH = DF_HEADS * DF_VDIM
N_BRANCH = 4

IN_SPLITS = (MLA_Q_RANK, MLA_KV_RANK, MLA_ROPE, 3 * B_WIDTH, C_WIDTH, 2 * SW_KV_HEADS * SW_DIM,
             2 * DF_HEADS * DF_DIM, 2 * DF_HEADS * DF_DIM, D_WIDTH)

N_EXPERTS = 32
N_GROUPS = 8
EXPERTS_PER_GROUP = N_EXPERTS // N_GROUPS
D_EXPERT = 256

LANES = 128
HEAD_WIDTH = 64

OFF_CQ = 0
OFF_CKV = OFF_CQ + MLA_Q_RANK
OFF_KR = OFF_CKV + MLA_KV_RANK
OFF_KRS = OFF_KR + LANES
OFF_REST = OFF_KRS + LANES
REST_WIDTH = sum(IN_SPLITS[3:])
IN_PAD_WIDTH = OFF_REST + REST_WIDTH

TOKEN_BLOCK = 512
ATT_BLOCK = 512
SUFFIX_BLOCK = 256
EXP_UNDERFLOW = -104.0
MOE_ROWS = 1024
MOE_CAP = 256
VMEM_LIMIT = 56 * 1024 * 1024


def _cparams(sem):
    return pltpu.CompilerParams(dimension_semantics=sem, vmem_limit_bytes=VMEM_LIMIT)


def _nt_dot(a, b, **kw):
    return lax.dot_general(a, b, (((1,), (1,)), ((), ())), preferred_element_type=F32, **kw)


def _const_spec(shape):
    nd = len(shape)
    return pl.BlockSpec(shape, lambda *_: (0,) * nd)


def _split_heads(val, out_ref, n_heads, width, ones_lane=None):
    for h in range(n_heads):
        tile = val[:, (h // 2) * LANES:(h // 2 + 1) * LANES]
        if h % 2:
            tile = pltpu.roll(tile, LANES - width, axis=1)
        if ones_lane is None:
            out_ref[0, h] = tile[:, :width].astype(out_ref.dtype)
        else:
            keep = lax.broadcasted_iota(jnp.int32, tile.shape, 1) < width
            out_ref[0, h] = (jnp.where(keep, tile, 0.0) + ones_lane).astype(out_ref.dtype)


def _proj_kernel(x_ref, win_ref, qn_ref, kvn_ref, wuq_ref, wuqs_ref, wuk_ref, wuv_ref,
                 cq_ref, sq_ref, ck_ref, sk_ref, rscale_ref,
                 mq_ref, mk_ref, mv_ref, sbq_ref, sbk_ref, sbv_ref, swq_ref, swk_ref, swv_ref,
                 dfq_ref, dfk_ref, dfv_ref):
    xb = x_ref[0].astype(BF16)
    h = jnp.dot(xb, win_ref[...], preferred_element_type=F32)
    c_q = h[:, OFF_CQ:OFF_CQ + MLA_Q_RANK]
    c_kv = h[:, OFF_CKV:OFF_CKV + MLA_KV_RANK]
    nq = c_q * lax.rsqrt(jnp.mean(c_q * c_q, axis=-1, keepdims=True) + RMS_EPS) * qn_ref[...]
    nqb = nq.astype(BF16)
    q = jnp.dot(nqb, wuq_ref[...], preferred_element_type=F32)
    qs = jnp.dot(nqb, wuqs_ref[...], preferred_element_type=F32)
    q = q * cq_ref[...] + qs * sq_ref[...]
    nkv = c_kv * lax.rsqrt(jnp.mean(c_kv * c_kv, axis=-1, keepdims=True) + RMS_EPS) * kvn_ref[...]
    nkvb = nkv.astype(BF16)
    kn = jnp.dot(nkvb, wuk_ref[...], preferred_element_type=F32)
    vv = jnp.dot(nkvb, wuv_ref[...], preferred_element_type=F32)
    kr = h[:, OFF_KR:OFF_KR + LANES] * ck_ref[...] + h[:, OFF_KRS:OFF_KRS + LANES] * sk_ref[...]
    ones_lane = (lax.broadcasted_iota(jnp.int32, (1, LANES), 1) == HEAD_WIDTH).astype(F32)
    for hd in range(MLA_HEADS):
        lanes = slice(hd * LANES, (hd + 1) * LANES)
        mq_ref[0, hd] = q[:, lanes][:, :MLA_QK].astype(mq_ref.dtype)
        mk_ref[0, hd] = (kn[:, lanes] + kr)[:, :MLA_QK].astype(mk_ref.dtype)
        mv_ref[0, hd] = (vv[:, lanes] + ones_lane).astype(mv_ref.dtype)
    rest = h[:, OFF_REST:] * rscale_ref[...]
    o = 0
    for ref, nh in ((sbq_ref, SB_HEADS), (sbk_ref, SB_HEADS), (sbv_ref, SB_HEADS),
                    (swq_ref, SW_Q_HEADS), (swk_ref, SW_KV_HEADS), (swv_ref, SW_KV_HEADS),
                    (dfq_ref, DF_HEADS), (dfk_ref, DF_HEADS), (dfv_ref, DF_HEADS)):
        _split_heads(rest[:, o:o + nh * HEAD_WIDTH], ref, nh, HEAD_WIDTH,
                     ones_lane if ref is dfv_ref else None)
        o += nh * HEAD_WIDTH


def _proj(x, win, qn, kvn, wuq, wuqs, wuk, wuv, cq, sq, ck, sk, rscale):
    b, s_len, _ = x.shape
    tm = TOKEN_BLOCK
    tab = lambda w: pl.BlockSpec((tm, w), lambda bb, i: (i, 0))
    heads = lambda nh, d: pl.BlockSpec((1, nh, tm, d), lambda bb, i: (bb, 0, i, 0))
    shape = lambda nh, d: jax.ShapeDtypeStruct((b, nh, s_len, d), BF16)
    outs = [(MLA_HEADS, MLA_QK), (MLA_HEADS, MLA_QK), (MLA_HEADS, LANES)] + [(SB_HEADS, SB_DIM)] * 3 + [
        (SW_Q_HEADS, SW_DIM), (SW_KV_HEADS, SW_DIM), (SW_KV_HEADS, SW_DIM)] + [
        (DF_HEADS, HEAD_WIDTH), (DF_HEADS, HEAD_WIDTH), (DF_HEADS, LANES)]
    consts = (win, qn, kvn, wuq, wuqs, wuk, wuv)
    return pl.pallas_call(
        _proj_kernel,
        grid=(b, s_len // tm),
        in_specs=[pl.BlockSpec((1, tm, D_MODEL), lambda bb, i: (bb, i, 0))]
                 + [_const_spec(c.shape) for c in consts]
                 + [tab(cq.shape[1]), tab(sq.shape[1]), tab(LANES), tab(LANES), _const_spec(rscale.shape)],
        out_specs=[heads(nh, d) for nh, d in outs],
        out_shape=[shape(nh, d) for nh, d in outs],
        compiler_params=_cparams(("arbitrary", "arbitrary")),
        name="proj",
    )(x, *consts, cq, sq, ck, sk, rscale)


def _causal_mask(n):
    row = lax.broadcasted_iota(jnp.int32, (n, n), 0)
    col = lax.broadcasted_iota(jnp.int32, (n, n), 1)
    return row, col


def _softmax_step(s, v_ones, m, acc):
    m_new = jnp.maximum(m, jnp.max(s, axis=-1, keepdims=True))
    p = jnp.exp((s - m_new).astype(BF16))
    alpha = jnp.exp(m - m_new)
    return m_new, alpha * acc + jnp.dot(p, v_ones, preferred_element_type=F32)


def _normalised(acc, width):
    return acc[:, :width] / acc[:, width:width + 1]


def _walk_full_blocks(n_blocks, blk, block, carry):
    carry = lax.fori_loop(0, n_blocks // 2, lambda j, c: block(2 * j, 2 * blk, c), carry)
    return lax.fori_loop(0, n_blocks % 2, lambda _, c: block(n_blocks - 1, blk, c), carry)


def _mla_kernel(q_ref, k_ref, v_ref, o_ref, *, blk):
    qi = pl.program_id(2)
    q = q_ref[0, 0]

    def kv_block(ki, width=blk):
        start = pl.multiple_of(ki * blk, blk)
        return k_ref[0, 0, pl.ds(start, width), :], v_ref[0, 0, pl.ds(start, width), :]

    def block(ki, width, carry):
        k, v = kv_block(ki, width)
        return _softmax_step(_nt_dot(q, k), v, *carry)

    init = (jnp.full((blk, 1), NEG_INF, F32), jnp.zeros((blk, LANES), F32))
    carry = _walk_full_blocks(qi, blk, block, init)
    k, v = kv_block(qi)
    row, col = _causal_mask(blk)
    s = jnp.where(col <= row, _nt_dot(q, k), NEG_INF)
    _, acc = _softmax_step(s, v, *carry)
    o_ref[0, 0] = _normalised(acc, MLA_V).astype(o_ref.dtype)


def _head_spec(blk, d):
    return pl.BlockSpec((1, 1, blk, d), lambda b, h, i: (b, h, i, 0))


def _full_spec(s_len, d):
    return pl.BlockSpec((1, 1, s_len, d), lambda b, h, i: (b, h, 0, 0))


def _mla_attention(q, k, v):
    b, nh, s_len, _ = q.shape
    blk = ATT_BLOCK
    return pl.pallas_call(
        functools.partial(_mla_kernel, blk=blk),
        grid=(b, nh, s_len // blk),
        in_specs=[_head_spec(blk, MLA_QK), _full_spec(s_len, MLA_QK), _full_spec(s_len, LANES)],
        out_specs=_head_spec(blk, MLA_V),
        out_shape=jax.ShapeDtypeStruct((b, nh, s_len, MLA_V), BF16),
        compiler_params=_cparams(("arbitrary",) * 3),
        name="mla",
    )(q, k, v)


def _log_sigmoid(z):
    return jnp.minimum(z, 0.0) - jnp.log(1.0 + jnp.exp(-jnp.abs(z)))


def _suffix_sum(l1m, upper):
    sub = upper.shape[0]
    n_sub = l1m.shape[1] // sub
    pieces = [None] * n_sub
    right = None
    for c in reversed(range(n_sub)):
        chunk = l1m[:, c * sub:(c + 1) * sub]
        hi = chunk.astype(BF16)
        lo = (chunk - hi.astype(F32)).astype(BF16)
        inner = jnp.dot(hi, upper, preferred_element_type=F32) + jnp.dot(lo, upper, preferred_element_type=F32)
        total = inner[:, 0:1] + chunk[:, 0:1]
        if right is None:
            pieces[c], right = inner, total
        else:
            pieces[c], right = inner + right, right + total
    return jnp.concatenate(pieces, axis=1), right


def _sb_kernel(q_ref, k_ref, v_ref, o_ref, *, blk):
    qi = pl.program_id(2)
    q = q_ref[0, 0]
    srow, scol = _causal_mask(SUFFIX_BLOCK)
    upper = (srow > scol).astype(BF16)

    def kv_block(ki):
        start = pl.multiple_of(ki * blk, blk)
        return k_ref[0, 0, pl.ds(start, blk), :], v_ref[0, 0, pl.ds(start, blk), :]

    row, col = _causal_mask(blk)
    k, v = kv_block(qi)
    z = _nt_dot(q, k)
    mask = col < row
    lb = _log_sigmoid(z)
    l1m = jnp.where(mask, lb - z, 0.0)
    suffix, passed = _suffix_sum(l1m, upper)
    a = jnp.where(mask, jnp.exp(lb + suffix), 0.0)
    acc = jnp.dot(a.astype(BF16), v, preferred_element_type=F32)

    def live(carry):
        j, passed, _ = carry
        return (j < qi) & (jnp.max(passed) >= EXP_UNDERFLOW)

    def body(carry):
        j, passed, acc = carry
        k, v = kv_block(qi - 1 - j)
        z = _nt_dot(q, k)
        lb = _log_sigmoid(z)
        l1m = lb - z
        suffix, total = _suffix_sum(l1m, upper)
        a = jnp.exp(lb + suffix + passed)
        acc = acc + jnp.dot(a.astype(BF16), v, preferred_element_type=F32)
        return j + 1, passed + total, acc

    _, _, acc = lax.while_loop(live, body, (jnp.int32(0), passed, acc))
    o_ref[0, 0] = acc.astype(o_ref.dtype)


def _sb_attention(q, k, v):
    b, nh, s_len, d = q.shape
    blk = ATT_BLOCK
    return pl.pallas_call(
        functools.partial(_sb_kernel, blk=blk),
        grid=(b, nh, s_len // blk),
        in_specs=[_head_spec(blk, d), _full_spec(s_len, d), _full_spec(s_len, d)],
        out_specs=_head_spec(blk, d),
        out_shape=jax.ShapeDtypeStruct((b, nh, s_len, d), BF16),
        compiler_params=_cparams(("arbitrary",) * 3),
        name="sb",
    )(q, k, v)


def _swa_kernel(slopes_ref, sinks_ref, q_ref, kp_ref, kc_ref, vp_ref, vc_ref,
                pq_ref, pkp_ref, pkc_ref, o_ref):
    g = pl.program_id(1)
    i = pl.program_id(2)
    w = SW_WINDOW
    q = q_ref[0].reshape(SW_GROUP * w, SW_DIM)
    k = jnp.concatenate([kp_ref[0, 0], kc_ref[0, 0]], axis=0)
    v = jnp.concatenate([vp_ref[0, 0], vc_ref[0, 0]], axis=0)
    s = _nt_dot(q, k)
    pk = jnp.concatenate([pkp_ref[...], pkc_ref[...]], axis=1)
    dist = pq_ref[...] - pk
    qi = lax.broadcasted_iota(jnp.int32, (w, 2 * w), 0)
    kj = lax.broadcasted_iota(jnp.int32, (w, 2 * w), 1)
    mask = (kj > qi) & (kj <= qi + w) & ((kj >= w) | (i > 0))
    for j in range(SW_GROUP):
        head = g * SW_GROUP + j
        sj = jnp.where(mask, s[j * w:(j + 1) * w] - slopes_ref[head] * dist, NEG_INF)
        sink = sinks_ref[head]
        m = jnp.maximum(jnp.max(sj, axis=-1, keepdims=True), sink)
        p = jnp.exp(sj - m)
        denom = jnp.sum(p, axis=-1, keepdims=True) + jnp.exp(sink - m)
        o = jnp.dot(p.astype(BF16), v, preferred_element_type=F32) / denom
        o_ref[0, j] = o.astype(o_ref.dtype)


def _swa_attention(q, k, v, slopes, sinks, pos_col, pos_row):
    b, _, s_len, d = q.shape
    w = SW_WINDOW
    prev = lambda i: jnp.maximum(i - 1, 0)
    smem = pl.BlockSpec(memory_space=pltpu.SMEM)
    kv_prev = pl.BlockSpec((1, 1, w, d), lambda bb, g, i: (bb, g, prev(i), 0))
    kv_cur = pl.BlockSpec((1, 1, w, d), lambda bb, g, i: (bb, g, i, 0))
    return pl.pallas_call(
        _swa_kernel,
        grid=(b, SW_KV_HEADS, s_len // w),
        in_specs=[smem, smem,
                  pl.BlockSpec((1, SW_GROUP, w, d), lambda bb, g, i: (bb, g, i, 0)),
                  kv_prev, kv_cur, kv_prev, kv_cur,
                  pl.BlockSpec((w, 1), lambda bb, g, i: (i, 0)),
                  pl.BlockSpec((1, w), lambda bb, g, i: (0, prev(i))),
                  pl.BlockSpec((1, w), lambda bb, g, i: (0, i))],
        out_specs=pl.BlockSpec((1, SW_GROUP, w, d), lambda bb, g, i: (bb, g, i, 0)),
        out_shape=jax.ShapeDtypeStruct((b, SW_Q_HEADS, s_len, d), BF16),
        compiler_params=_cparams(("arbitrary",) * 3),
        name="swa",
    )(slopes, sinks, q, k, k, v, v, pos_col, pos_row, pos_row)


def _diff_kernel(slopes_ref, q_ref, k_ref, v_ref, pk_ref, lq1_ref, lk1_ref, lq2_ref, lk2_ref,
                 subln_ref, o_ref, *, blk, lam_init):
    h = pl.program_id(1)
    qi = pl.program_id(2)
    q = q_ref[0, 0]
    lane = lax.broadcasted_iota(jnp.int32, q.shape, 1)
    zero = jnp.zeros_like(q)
    qq = jnp.concatenate([jnp.where(lane < DF_DIM, q, zero), jnp.where(lane >= DF_DIM, q, zero)], axis=0)
    slope = slopes_ref[h]

    def block(ki, width=blk):
        start = pl.multiple_of(ki * blk, blk)
        k = k_ref[0, 0, pl.ds(start, width), :]
        v = v_ref[0, 0, pl.ds(start, width), :]
        return _nt_dot(qq, k) + slope * pk_ref[:, pl.ds(start, width)], v

    def step(ki, width, carry):
        s, v = block(ki, width)
        return _softmax_step(s, v, *carry)

    init = (jnp.full((2 * blk, 1), NEG_INF, F32), jnp.zeros((2 * blk, LANES), F32))
    carry = _walk_full_blocks(qi, blk, step, init)
    s, v = block(qi)
    row = lax.broadcasted_iota(jnp.int32, (2 * blk, blk), 0)
    col = lax.broadcasted_iota(jnp.int32, (2 * blk, blk), 1)
    causal = (col <= row) & ((row < blk) | (col <= row - blk))
    _, acc = _softmax_step(jnp.where(causal, s, NEG_INF), v, *carry)
    lam = (jnp.exp(jnp.sum(lq1_ref[...] * lk1_ref[...], keepdims=True))
           - jnp.exp(jnp.sum(lq2_ref[...] * lk2_ref[...], keepdims=True)) + lam_init)
    o = _normalised(acc, DF_VDIM)
    o = o[:blk] - lam * o[blk:]
    o = o * lax.rsqrt(jnp.mean(o * o, axis=-1, keepdims=True) + RMS_EPS) * subln_ref[...] * (1.0 - lam_init)
    o_ref[0, 0] = o.astype(o_ref.dtype)


def _diff_attention(q, k, v, slopes, pos_row, lq1, lk1, lq2, lk2, subln, lam_init):
    b, nh, s_len, d = q.shape
    blk = ATT_BLOCK
    smem = pl.BlockSpec(memory_space=pltpu.SMEM)
    vec = lambda n: pl.BlockSpec((1, n), lambda bb, h, i: (0, 0))
    return pl.pallas_call(
        functools.partial(_diff_kernel, blk=blk, lam_init=lam_init),
        grid=(b, nh, s_len // blk),
        in_specs=[smem, _head_spec(blk, d), _full_spec(s_len, d), _full_spec(s_len, LANES),
                  pl.BlockSpec((1, s_len), lambda bb, h, i: (0, 0)),
                  vec(DF_DIM), vec(DF_DIM), vec(DF_DIM), vec(DF_DIM), vec(DF_VDIM)],
        out_specs=_head_spec(blk, DF_VDIM),
        out_shape=jax.ShapeDtypeStruct((b, nh, s_len, DF_VDIM), BF16),
        compiler_params=_cparams(("arbitrary",) * 3),
        name="diff",
    )(slopes, q, k, v, pos_row, lq1, lk1, lq2, lk2, subln)


def _layer_norm(v, g, b):
    mu = jnp.mean(v, axis=-1, keepdims=True)
    c = v - mu
    var = jnp.mean(c * c, axis=-1, keepdims=True)
    return c * lax.rsqrt(var + LN_EPS) * g + b


def _sigmoid(z):
    return 1.0 / (1.0 + jnp.exp(-z))


def _route(logits_t, bias_col):
    scores = _sigmoid(logits_t)
    biased = scores + bias_col
    ng = N_GROUPS
    sc = [scores[j * ng:(j + 1) * ng] for j in range(EXPERTS_PER_GROUP)]
    bs = [biased[j * ng:(j + 1) * ng] for j in range(EXPERTS_PER_GROUP)]
    gscore = None
    for a in range(EXPERTS_PER_GROUP):
        for c in range(a + 1, EXPERTS_PER_GROUP):
            pair = bs[a] + bs[c]
            gscore = pair if gscore is None else jnp.maximum(gscore, pair)
    gid = lax.broadcasted_iota(jnp.int32, gscore.shape, 0)
    best = jnp.max(gscore, axis=0, keepdims=True)
    gsel = jnp.min(jnp.where(gscore == best, gid, ng), axis=0, keepdims=True)
    onehot = gid == gsel
    pick = lambda a: jnp.sum(jnp.where(onehot, a, 0.0), axis=0, keepdims=True)
    b_in = [pick(a) for a in bs]
    s_in = [pick(a) for a in sc]
    chosen = []
    for j in range(EXPERTS_PER_GROUP):
        ahead = jnp.zeros_like(b_in[j])
        for c in range(EXPERTS_PER_GROUP):
            if c == j:
                continue
            beats = (b_in[c] > b_in[j]) | ((b_in[c] == b_in[j]) & (c < j))
            ahead = ahead + beats.astype(F32)
        chosen.append(jnp.where(ahead < 2.0, s_in[j], 0.0))
    total = chosen[0] + chosen[1] + chosen[2] + chosen[3]
    gates = [c / total for c in chosen]
    return gates, onehot


def _merge_kernel(x_ref, oa_ref, ob_ref, oc_ref, od_ref, wg_ref, bg_ref, wa_ref, wb_ref, wc_ref, wd_ref,
                  wo_ref, g_ref, b_ref, wr_ref, rb_ref, x1_ref, gate_ref, route_ref, count_sc, *, alpha):
    tm = x_ref.shape[1]
    x = x_ref[0]
    xb = x.astype(BF16)
    merged = None
    for n, (o_ref, w_ref) in enumerate(((oa_ref, wa_ref), (ob_ref, wb_ref), (oc_ref, wc_ref), (od_ref, wd_ref))):
        cols = slice(n * D_MODEL, (n + 1) * D_MODEL)
        gate = _sigmoid(jnp.dot(xb, wg_ref[:, cols], preferred_element_type=F32) + bg_ref[:, cols])
        heads = jnp.concatenate([o_ref[0, hd] for hd in range(o_ref.shape[1])], axis=1)
        term = gate * jnp.dot(heads, w_ref[...], preferred_element_type=F32)
        merged = term if merged is None else merged + term
    mix = jnp.dot(merged.astype(BF16), wo_ref[...], preferred_element_type=F32)
    y = _layer_norm(alpha * x + mix, g_ref[...], b_ref[...])
    x1_ref[0] = y

    logits_t = _nt_dot(wr_ref[...], y, precision=lax.Precision.HIGHEST)
    gates, onehot = _route(logits_t, rb_ref[...])

    @pl.when(pl.program_id(1) % (MOE_ROWS // tm) == 0)
    def _():
        count_sc[...] = jnp.zeros_like(count_sc)

    onef = onehot.astype(F32)
    before = (lax.broadcasted_iota(jnp.int32, (tm, tm), 0) < lax.broadcasted_iota(jnp.int32, (tm, tm), 1))
    prefix = jnp.dot(onehot.astype(BF16), before.astype(BF16), preferred_element_type=F32)
    seen = count_sc[...]
    rank = jnp.sum(onef * (prefix + seen[:, 0:1]), axis=0, keepdims=True)
    count_sc[...] = seen + jnp.sum(onef, axis=1, keepdims=True)
    group = jnp.sum(onef * lax.broadcasted_iota(jnp.int32, onef.shape, 0).astype(F32), axis=0, keepdims=True)

    rows = [jnp.where(onehot, gate, 0.0) for gate in gates] + [group, rank]
    rows.append(jnp.zeros((LANES - N_EXPERTS - 2, tm), F32))
    gate_ref[0] = jnp.concatenate(rows, axis=0).T
    route_ref[0] = jnp.concatenate([group, rank, jnp.zeros((6, tm), F32)], axis=0)


def _merge(x, oa, ob, oc, od, wg, bg, wa, wb, wc, wd, wo, g, b, wr_t, rb_col, alpha):
    bsz, s_len, _ = x.shape
    tm = TOKEN_BLOCK
    row = lambda w: pl.BlockSpec((1, tm, w), lambda bb, i: (bb, i, 0))
    heads = lambda o: pl.BlockSpec((1, o.shape[1], tm, o.shape[3]), lambda bb, i: (bb, 0, i, 0))
    consts = (wg, bg, wa, wb, wc, wd, wo, g, b, wr_t, rb_col)
    return pl.pallas_call(
        functools.partial(_merge_kernel, alpha=alpha),
        grid=(bsz, s_len // tm),
        in_specs=[row(D_MODEL), heads(oa), heads(ob), heads(oc), heads(od)]
                 + [_const_spec(c.shape) for c in consts],
        out_specs=[row(D_MODEL), row(LANES),
                   pl.BlockSpec((1, 8, tm), lambda bb, i: (bb * (s_len // tm) + i, 0, 0))],
        out_shape=[jax.ShapeDtypeStruct((bsz, s_len, D_MODEL), F32),
                   jax.ShapeDtypeStruct((bsz, s_len, LANES), F32),
                   jax.ShapeDtypeStruct((bsz * s_len // tm, 8, tm), F32)],
        scratch_shapes=[pltpu.VMEM((N_GROUPS, LANES), F32)],
        compiler_params=_cparams(("arbitrary", "arbitrary")),
        name="merge",
    )(x, oa, ob, oc, od, *consts)


def _two_bf16(a):
    hi = a.astype(BF16)
    return hi, (a - hi.astype(F32)).astype(BF16)


def _moe_kernel(x_ref, gate_ref, route_ref, w1_ref, w3_ref, w2_ref, g_ref, b_ref, o_ref,
                acc_ref, xb_ref, ghi_ref, glo_ref, *, alpha):
    grp = pl.program_id(1)

    @pl.when(grp == 0)
    def _():
        acc_ref[...] = jnp.zeros_like(acc_ref)
        xb_ref[...] = x_ref[...].astype(BF16)
        ghi_ref[...], glo_ref[...] = _two_bf16(gate_ref[...])

    tm = x_ref.shape[0]
    n_sub = route_ref.shape[0]
    sub = tm // n_sub
    cap = MOE_CAP
    grp_f = grp.astype(F32)
    in_group_col = gate_ref[:, N_EXPERTS:N_EXPERTS + 1] == grp_f
    rank_col = gate_ref[:, N_EXPERTS + 1:N_EXPERTS + 2]
    in_group_row = [route_ref[j, 0:1, :] == grp_f for j in range(n_sub)]
    rank_row = [route_ref[j, 1:2, :] for j in range(n_sub)]
    count = sum(jnp.sum(m.astype(F32)) for m in in_group_row)
    n_chunks = jnp.ceil(count / cap).astype(jnp.int32)
    w2 = w2_ref[...].reshape(EXPERTS_PER_GROUP * D_EXPERT, D_MODEL)
    lane = lax.broadcasted_iota(jnp.int32, (cap, LANES), 1)

    def chunk(c, carry):
        first = c * cap
        slot_col = (lax.broadcasted_iota(jnp.int32, (cap, 1), 0) + first).astype(F32)
        slot_row = (lax.broadcasted_iota(jnp.int32, (1, cap), 1) + first).astype(F32)
        xg = None
        gg = None
        for j in range(n_sub):
            sel = ((rank_row[j] == slot_col) & in_group_row[j]).astype(BF16)
            tok = slice(j * sub, (j + 1) * sub)
            part = jnp.dot(sel, xb_ref[tok, :], preferred_element_type=F32)
            gpart = (jnp.dot(sel, ghi_ref[tok, :], preferred_element_type=F32)
                     + jnp.dot(sel, glo_ref[tok, :], preferred_element_type=F32))
            xg = part if xg is None else xg + part
            gg = gpart if gg is None else gg + gpart
        xg = xg.astype(BF16)
        hid = []
        for e in range(EXPERTS_PER_GROUP):
            h1 = jnp.dot(xg, w1_ref[e], preferred_element_type=F32)
            h3 = jnp.dot(xg, w3_ref[e], preferred_element_type=F32)
            gate = jnp.sum(jnp.where(lane == e * N_GROUPS + grp, gg, 0.0), axis=1, keepdims=True)
            hid.append((h1 * _sigmoid(h1) * h3 * gate).astype(BF16))
        y = jnp.dot(jnp.concatenate(hid, axis=1), w2, preferred_element_type=F32)
        y_hi, y_lo = _two_bf16(y)
        back = ((rank_col == slot_row) & in_group_col).astype(BF16)
        acc_ref[...] += (jnp.dot(back, y_hi, preferred_element_type=F32)
                         + jnp.dot(back, y_lo, preferred_element_type=F32))
        return carry

    lax.fori_loop(0, n_chunks, chunk, 0)

    @pl.when(grp == N_GROUPS - 1)
    def _():
        o_ref[...] = _layer_norm(alpha * x_ref[...] + acc_ref[...], g_ref[...], b_ref[...])


def _moe(x1, gates, route, w1, w3, w2, g, b, alpha):
    t = x1.shape[0]
    tm = MOE_ROWS
    epg = EXPERTS_PER_GROUP
    n_sub = tm // route.shape[2]
    return pl.pallas_call(
        functools.partial(_moe_kernel, alpha=alpha),
        grid=(t // tm, N_GROUPS),
        in_specs=[pl.BlockSpec((tm, D_MODEL), lambda i, e: (i, 0)),
                  pl.BlockSpec((tm, LANES), lambda i, e: (i, 0)),
                  pl.BlockSpec((n_sub,) + route.shape[1:], lambda i, e: (i, 0, 0)),
                  pl.BlockSpec((epg, D_MODEL, D_EXPERT), lambda i, e: (e, 0, 0)),
                  pl.BlockSpec((epg, D_MODEL, D_EXPERT), lambda i, e: (e, 0, 0)),
                  pl.BlockSpec((epg, D_EXPERT, D_MODEL), lambda i, e: (e, 0, 0)),
                  pl.BlockSpec((1, D_MODEL), lambda i, e: (0, 0)),
                  pl.BlockSpec((1, D_MODEL), lambda i, e: (0, 0))],
        out_specs=pl.BlockSpec((tm, D_MODEL), lambda i, e: (i, 0)),
        out_shape=jax.ShapeDtypeStruct((t, D_MODEL), F32),
        scratch_shapes=[pltpu.VMEM((tm, D_MODEL), F32), pltpu.VMEM((tm, D_MODEL), BF16),
                        pltpu.VMEM((tm, LANES), BF16), pltpu.VMEM((tm, LANES), BF16)],
        compiler_params=_cparams(("arbitrary", "arbitrary")),
        name="moe",
    )(x1, gates, route, w1, w3, w2, g, b)


def _alibi_slopes(n_heads):
    return 2.0 ** (-8.0 * jnp.arange(1, n_heads + 1, dtype=F32) / n_heads)


def _rope_tables(positions):
    half = MLA_ROPE // 2
    inv_freq = ROPE_THETA ** (-jnp.arange(half, dtype=F32) / half)
    ang = positions.astype(F32)[:, None] * inv_freq[None, :]
    cos = jnp.cos(ang)
    sin = jnp.sin(ang)
    return jnp.concatenate([cos, cos], axis=1), jnp.concatenate([sin, sin], axis=1)


def _rotate_half_columns(w):
    half = w.shape[-1] // 2
    return jnp.concatenate([-w[..., half:], w[..., :half]], axis=-1)


def _prep_w_in(w_in):
    split_at = [int(c) for c in np.cumsum(IN_SPLITS)[:-1]]
    parts = jnp.split(w_in, split_at, axis=-1)
    c_q, c_kv, k_rope = parts[0], parts[1], parts[2]
    before = jnp.zeros((w_in.shape[0], MLA_NOPE), w_in.dtype)
    after = jnp.zeros((w_in.shape[0], LANES - MLA_QK), w_in.dtype)
    return jnp.concatenate([c_q, c_kv, before, k_rope, after, before, _rotate_half_columns(k_rope), after]
                           + list(parts[3:]), axis=-1).astype(BF16)


def _pad_heads(w, n_heads, width):
    w3 = w.reshape(w.shape[0], n_heads, width)
    return jnp.pad(w3, ((0, 0), (0, 0), (0, LANES - width))).reshape(w.shape[0], n_heads * LANES)


def _prep_w_uq(w_uq):
    w = w_uq.reshape(MLA_Q_RANK, MLA_HEADS, MLA_QK)
    rot = jnp.concatenate([jnp.zeros_like(w[..., :MLA_NOPE]), _rotate_half_columns(w[..., MLA_NOPE:])], axis=-1)
    return (_pad_heads(w_uq, MLA_HEADS, MLA_QK).astype(BF16),
            _pad_heads(rot.reshape(w_uq.shape), MLA_HEADS, MLA_QK).astype(BF16))


def _prep_w_ukv(w_ukv):
    w = w_ukv.reshape(MLA_KV_RANK, MLA_HEADS, MLA_NOPE + MLA_V)
    wk = w[..., :MLA_NOPE].reshape(MLA_KV_RANK, MLA_HEADS * MLA_NOPE)
    wv = w[..., MLA_NOPE:].reshape(MLA_KV_RANK, MLA_HEADS * MLA_V)
    return _pad_heads(wk, MLA_HEADS, MLA_NOPE).astype(BF16), _pad_heads(wv, MLA_HEADS, MLA_V).astype(BF16)


def kernel(x, positions, w_in, mla_q_norm, mla_kv_norm, w_uq, w_ukv, sw_sinks, df_lq1, df_lk1, df_lq2, df_lk2,
           df_subln, w_gate, b_gate, w_br_a, w_br_b, w_br_c, w_br_d, w_out, ln1_g, ln1_b, w_router, router_bias,
           moe_w1, moe_w3, moe_w2, ln2_g, ln2_b):
    b, s_len, d = x.shape
    depth = w_in.shape[0]
    t = b * s_len
    alpha = (2 * depth) ** 0.25
    assert d == D_MODEL and s_len % TOKEN_BLOCK == 0 and s_len % ATT_BLOCK == 0 and s_len % MOE_ROWS == 0
    assert MOE_ROWS % TOKEN_BLOCK == 0
    assert SB_DIM == SW_DIM == DF_VDIM == 2 * DF_DIM == HEAD_WIDTH == LANES // 2
    assert ATT_BLOCK % SW_WINDOW == 0

    posf = positions.astype(F32)
    pos_col = posf[:, None]
    pos_row = posf[None, :]
    cos, sin = _rope_tables(positions)
    q_scale = MLA_QK ** -0.5
    ones_n = jnp.ones((s_len, MLA_NOPE), F32)
    tail = jnp.zeros((s_len, LANES - MLA_QK), F32)
    cq = jnp.tile(jnp.concatenate([ones_n * q_scale, cos * q_scale, tail], axis=1), (1, MLA_HEADS))
    sq = jnp.tile(jnp.concatenate([0.0 * ones_n, sin * q_scale, tail], axis=1), (1, MLA_HEADS))
    ck = jnp.concatenate([0.0 * ones_n, cos, tail], axis=1)
    sk = jnp.concatenate([0.0 * ones_n, sin, tail], axis=1)
    rscale = jnp.concatenate([
        jnp.full((B_WIDTH,), SB_DIM ** -0.5, F32), jnp.ones((2 * B_WIDTH,), F32),
        jnp.full((C_WIDTH,), SW_DIM ** -0.5, F32), jnp.ones((2 * SW_KV_HEADS * SW_DIM,), F32),
        jnp.full((2 * DF_HEADS * DF_DIM,), DF_DIM ** -0.5, F32), jnp.ones((2 * DF_HEADS * DF_DIM + D_WIDTH,), F32),
    ])[None, :]
    sw_slopes = _alibi_slopes(SW_Q_HEADS)
    df_slopes = _alibi_slopes(DF_HEADS)

    perm = np.arange(N_EXPERTS).reshape(N_GROUPS, EXPERTS_PER_GROUP).T.reshape(-1)
    wr_t = w_router.astype(F32).T[perm]
    rb_col = router_bias.astype(F32)[perm][:, None]

    x3 = x
    for l in range(depth):
        lam_init = 0.8 - 0.6 * math.exp(-0.3 * l)
        wuq, wuqs = _prep_w_uq(w_uq[l])
        wuk, wuv = _prep_w_ukv(w_ukv[l])
        (mla_q, mla_k, mla_v, sb_q, sb_k, sb_v, sw_q, sw_k, sw_v, df_q, df_k, df_v) = _proj(
            x3, _prep_w_in(w_in[l]), mla_q_norm[l][None, :], mla_kv_norm[l][None, :],
            wuq, wuqs, wuk, wuv, cq, sq, ck, sk, rscale)

        o_a = _mla_attention(mla_q, mla_k, mla_v)
        o_b = _sb_attention(sb_q, sb_k, sb_v)
        o_c = _swa_attention(sw_q, sw_k, sw_v, sw_slopes, sw_sinks[l].astype(F32), pos_col, pos_row)
        o_d = _diff_attention(df_q, df_k, df_v, df_slopes, pos_row,
                              df_lq1[l][None, :], df_lk1[l][None, :], df_lq2[l][None, :],
                              df_lk2[l][None, :], df_subln[l][None, :], lam_init)

        x1, gates, route = _merge(x3, o_a, o_b, o_c, o_d, w_gate[l].astype(BF16), b_gate[l][None, :],
                           w_br_a[l].astype(BF16), w_br_b[l].astype(BF16), w_br_c[l].astype(BF16),
                           w_br_d[l].astype(BF16), w_out[l].astype(BF16), ln1_g[l][None, :], ln1_b[l][None, :],
                           wr_t, rb_col, alpha)
        x2 = _moe(x1.reshape(t, d), gates.reshape(t, LANES), route, moe_w1[l].astype(BF16), moe_w3[l].astype(BF16),
                  moe_w2[l].astype(BF16), ln2_g[l][None, :], ln2_b[l][None, :], alpha)
        x3 = x2.reshape(b, s_len, d)
    return x3
```

```python
import functools
import math

import numpy as np
import jax
import jax.numpy as jnp
from jax import lax
from jax.experimental import pallas as pl
from jax.experimental.pallas import tpu as pltpu

F32 = jnp.float32
BF16 = jnp.bfloat16

D_MODEL = 1024
NEG_INF = -1e30
LN_EPS = 1e-5
RMS_EPS = 1e-6

MLA_HEADS = 4
MLA_Q_RANK = 256
MLA_KV_RANK = 128
MLA_NOPE = 64
MLA_ROPE = 32
MLA_V = 64
MLA_QK = MLA_NOPE + MLA_ROPE
ROPE_THETA = 10000.0

SB_HEADS = 4
SB_DIM = 64

SW_Q_HEADS = 8
SW_KV_HEADS = 2
SW_DIM = 64
SW_WINDOW = 128
SW_GROUP = SW_Q_HEADS // SW_KV_HEADS

DF_HEADS = 4
DF_DIM = 32
DF_VDIM = 2 * DF_DIM

A_WIDTH = MLA_HEADS * MLA_V
B_WIDTH = SB_HEADS * SB_DIM
C_WIDTH = SW_Q_HEADS * SW_DIM
D_WIDTH = DF_HEADS * DF_VDIM
N_BRANCH = 4

IN_SPLITS = (MLA_Q_RANK, MLA_KV_RANK, MLA_ROPE, 3 * B_WIDTH, C_WIDTH, 2 * SW_KV_HEADS * SW_DIM,
             2 * DF_HEADS * DF_DIM, 2 * DF_HEADS * DF_DIM, D_WIDTH)

N_EXPERTS = 32
N_GROUPS = 8
EXPERTS_PER_GROUP = N_EXPERTS // N_GROUPS
D_EXPERT = 256

LANES = 128
HEAD_WIDTH = 64

OFF_CQ = 0
OFF_CKV = OFF_CQ + MLA_Q_RANK
OFF_KR = OFF_CKV + MLA_KV_RANK
OFF_KRS = OFF_KR + LANES
OFF_REST = OFF_KRS + LANES
REST_WIDTH = sum(IN_SPLITS[3:])
IN_PAD_WIDTH = OFF_REST + REST_WIDTH

TOKEN_BLOCK = 512
ATT_BLOCK = 512
SUFFIX_BLOCK = 256
SB_WALK = 256
EXP_UNDERFLOW = -104.0
MOE_ROWS = 1024
MOE_CAP = 256
VMEM_LIMIT = 56 * 1024 * 1024


def _cparams(sem):
    return pltpu.CompilerParams(dimension_semantics=sem, vmem_limit_bytes=VMEM_LIMIT)


def _nt_dot(a, b, **kw):
    return lax.dot_general(a, b, (((1,), (1,)), ((), ())), preferred_element_type=F32, **kw)


def _const_spec(shape):
    nd = len(shape)
    return pl.BlockSpec(shape, lambda *_: (0,) * nd)


def _split_heads(val, out_ref, n_heads, width, ones_lane=None):
    for h in range(n_heads):
        tile = val[:, (h // 2) * LANES:(h // 2 + 1) * LANES]
        if h % 2:
            tile = pltpu.roll(tile, LANES - width, axis=1)
        if ones_lane is None:
            out_ref[0, h] = tile[:, :width].astype(out_ref.dtype)
        else:
            keep = lax.broadcasted_iota(jnp.int32, tile.shape, 1) < width
            out_ref[0, h] = (jnp.where(keep, tile, 0.0) + ones_lane).astype(out_ref.dtype)


def _proj_kernel(x_ref, win_ref, qn_ref, kvn_ref, wuq_ref, wuqs_ref, wuk_ref, wuv_ref,
                 cq_ref, sq_ref, ck_ref, sk_ref, rscale_ref,
                 mq_ref, mk_ref, mv_ref, sbq_ref, sbk_ref, sbv_ref, swq_ref, swk_ref, swv_ref,
                 dfq_ref, dfk_ref, dfv_ref):
    xb = x_ref[0].astype(BF16)
    h = jnp.dot(xb, win_ref[...], preferred_element_type=F32)
    c_q = h[:, OFF_CQ:OFF_CQ + MLA_Q_RANK]
    c_kv = h[:, OFF_CKV:OFF_CKV + MLA_KV_RANK]
    nq = c_q * lax.rsqrt(jnp.mean(c_q * c_q, axis=-1, keepdims=True) + RMS_EPS) * qn_ref[...]
    nqb = nq.astype(BF16)
    q = jnp.dot(nqb, wuq_ref[...], preferred_element_type=F32)
    qs = jnp.dot(nqb, wuqs_ref[...], preferred_element_type=F32)
    q = q * cq_ref[...] + qs * sq_ref[...]
    nkv = c_kv * lax.rsqrt(jnp.mean(c_kv * c_kv, axis=-1, keepdims=True) + RMS_EPS) * kvn_ref[...]
    nkvb = nkv.astype(BF16)
    kn = jnp.dot(nkvb, wuk_ref[...], preferred_element_type=F32)
    vv = jnp.dot(nkvb, wuv_ref[...], preferred_element_type=F32)
    kr = h[:, OFF_KR:OFF_KR + LANES] * ck_ref[...] + h[:, OFF_KRS:OFF_KRS + LANES] * sk_ref[...]
    ones_lane = (lax.broadcasted_iota(jnp.int32, (1, LANES), 1) == HEAD_WIDTH).astype(F32)
    for hd in range(MLA_HEADS):
        lanes = slice(hd * LANES, (hd + 1) * LANES)
        mq_ref[0, hd] = q[:, lanes][:, :MLA_QK].astype(mq_ref.dtype)
        mk_ref[0, hd] = (kn[:, lanes] + kr)[:, :MLA_QK].astype(mk_ref.dtype)
        mv_ref[0, hd] = (vv[:, lanes] + ones_lane).astype(mv_ref.dtype)
    rest = h[:, OFF_REST:] * rscale_ref[...]
    o = 0
    for ref, nh in ((sbq_ref, SB_HEADS), (sbk_ref, SB_HEADS), (sbv_ref, SB_HEADS),
                    (swq_ref, SW_Q_HEADS), (swk_ref, SW_KV_HEADS), (swv_ref, SW_KV_HEADS),
                    (dfq_ref, DF_HEADS), (dfk_ref, DF_HEADS), (dfv_ref, DF_HEADS)):
        _split_heads(rest[:, o:o + nh * HEAD_WIDTH], ref, nh, HEAD_WIDTH,
                     ones_lane if ref is dfv_ref else None)
        o += nh * HEAD_WIDTH


def _proj(x, win, qn, kvn, wuq, wuqs, wuk, wuv, cq, sq, ck, sk, rscale):
    b, s_len, _ = x.shape
    tm = TOKEN_BLOCK
    tab = lambda w: pl.BlockSpec((tm, w), lambda bb, i: (i, 0))
    heads = lambda nh, d: pl.BlockSpec((1, nh, tm, d), lambda bb, i: (bb, 0, i, 0))
    shape = lambda nh, d: jax.ShapeDtypeStruct((b, nh, s_len, d), BF16)
    outs = [(MLA_HEADS, MLA_QK), (MLA_HEADS, MLA_QK), (MLA_HEADS, LANES)] + [(SB_HEADS, SB_DIM)] * 3 + [
        (SW_Q_HEADS, SW_DIM), (SW_KV_HEADS, SW_DIM), (SW_KV_HEADS, SW_DIM)] + [
        (DF_HEADS, HEAD_WIDTH), (DF_HEADS, HEAD_WIDTH), (DF_HEADS, LANES)]
    consts = (win, qn, kvn, wuq, wuqs, wuk, wuv)
    return pl.pallas_call(
        _proj_kernel,
        grid=(b, s_len // tm),
        in_specs=[pl.BlockSpec((1, tm, D_MODEL), lambda bb, i: (bb, i, 0))]
                 + [_const_spec(c.shape) for c in consts]
                 + [tab(cq.shape[1]), tab(sq.shape[1]), tab(LANES), tab(LANES), _const_spec(rscale.shape)],
        out_specs=[heads(nh, d) for nh, d in outs],
        out_shape=[shape(nh, d) for nh, d in outs],
        compiler_params=_cparams(("arbitrary", "arbitrary")),
        name="proj",
    )(x, *consts, cq, sq, ck, sk, rscale)


def _causal_mask(n):
    row = lax.broadcasted_iota(jnp.int32, (n, n), 0)
    col = lax.broadcasted_iota(jnp.int32, (n, n), 1)
    return row, col


def _softmax_step(s, v_ones, m, acc):
    m_new = jnp.maximum(m, jnp.max(s, axis=-1, keepdims=True))
    p = jnp.exp((s - m_new).astype(BF16))
    alpha = jnp.exp(m - m_new)
    return m_new, alpha * acc + jnp.dot(p, v_ones, preferred_element_type=F32)


def _normalised(acc, width):
    return acc[:, :width] / acc[:, width:width + 1]


def _walk_full_blocks(n_blocks, blk, block, carry):
    carry = lax.fori_loop(0, n_blocks // 2, lambda j, c: block(2 * j, 2 * blk, c), carry)
    return lax.fori_loop(0, n_blocks % 2, lambda _, c: block(n_blocks - 1, blk, c), carry)


def _mla_kernel(q_ref, k_ref, v_ref, o_ref, *, blk):
    qi = pl.program_id(2)
    q = q_ref[0, 0]

    def kv_block(ki, width=blk):
        start = pl.multiple_of(ki * blk, blk)
        return k_ref[0, 0, pl.ds(start, width), :], v_ref[0, 0, pl.ds(start, width), :]

    def block(ki, width, carry):
        k, v = kv_block(ki, width)
        return _softmax_step(_nt_dot(q, k), v, *carry)

    init = (jnp.full((blk, 1), NEG_INF, F32), jnp.zeros((blk, LANES), F32))
    carry = _walk_full_blocks(qi, blk, block, init)
    k, v = kv_block(qi)
    row, col = _causal_mask(blk)
    s = jnp.where(col <= row, _nt_dot(q, k), NEG_INF)
    _, acc = _softmax_step(s, v, *carry)
    o_ref[0, 0] = _normalised(acc, MLA_V).astype(o_ref.dtype)


def _head_spec(blk, d):
    return pl.BlockSpec((1, 1, blk, d), lambda b, h, i: (b, h, i, 0))


def _full_spec(s_len, d):
    return pl.BlockSpec((1, 1, s_len, d), lambda b, h, i: (b, h, 0, 0))


def _mla_attention(q, k, v):
    b, nh, s_len, _ = q.shape
    blk = ATT_BLOCK
    return pl.pallas_call(
        functools.partial(_mla_kernel, blk=blk),
        grid=(b, nh, s_len // blk),
        in_specs=[_head_spec(blk, MLA_QK), _full_spec(s_len, MLA_QK), _full_spec(s_len, LANES)],
        out_specs=_head_spec(blk, MLA_V),
        out_shape=jax.ShapeDtypeStruct((b, nh, s_len, MLA_V), BF16),
        compiler_params=_cparams(("arbitrary",) * 3),
        name="mla",
    )(q, k, v)


def _log_sigmoid(z):
    return jnp.minimum(z, 0.0) - jnp.log(1.0 + jnp.exp(-jnp.abs(z)))


def _suffix_sum(l1m, upper):
    sub = upper.shape[0]
    n_sub = l1m.shape[1] // sub
    pieces = [None] * n_sub
    right = None
    for c in reversed(range(n_sub)):
        chunk = l1m[:, c * sub:(c + 1) * sub]
        hi = chunk.astype(BF16)
        lo = (chunk - hi.astype(F32)).astype(BF16)
        inner = jnp.dot(hi, upper, preferred_element_type=F32) + jnp.dot(lo, upper, preferred_element_type=F32)
        total = inner[:, 0:1] + chunk[:, 0:1]
        if right is None:
            pieces[c], right = inner, total
        else:
            pieces[c], right = inner + right, right + total
    return jnp.concatenate(pieces, axis=1), right


def _sb_kernel(q_ref, k_ref, v_ref, o_ref, *, blk):
    qi = pl.program_id(2)
    q = q_ref[0, 0]
    srow, scol = _causal_mask(SUFFIX_BLOCK)
    upper = (srow > scol).astype(BF16)

    def kv_rows(start, width):
        start = pl.multiple_of(start, width)
        return k_ref[0, 0, pl.ds(start, width), :], v_ref[0, 0, pl.ds(start, width), :]

    row, col = _causal_mask(blk)
    k, v = kv_rows(qi * blk, blk)
    z = _nt_dot(q, k)
    mask = col < row
    lb = _log_sigmoid(z)
    l1m = jnp.where(mask, lb - z, 0.0)
    suffix, passed = _suffix_sum(l1m, upper)
    a = jnp.where(mask, jnp.exp(lb + suffix), 0.0)
    acc = jnp.dot(a.astype(BF16), v, preferred_element_type=F32)

    n_walk = qi * (blk // SB_WALK)

    def live(carry):
        j, passed, _ = carry
        return (j < n_walk) & (jnp.max(passed) >= EXP_UNDERFLOW)

    def body(carry):
        j, passed, acc = carry
        k, v = kv_rows((n_walk - 1 - j) * SB_WALK, SB_WALK)
        z = _nt_dot(q, k)
        lb = _log_sigmoid(z)
        l1m = lb - z
        suffix, total = _suffix_sum(l1m, upper)
        a = jnp.exp(lb + suffix + passed)
        acc = acc + jnp.dot(a.astype(BF16), v, preferred_element_type=F32)
        return j + 1, passed + total, acc

    _, _, acc = lax.while_loop(live, body, (jnp.int32(0), passed, acc))
    o_ref[0, 0] = acc.astype(o_ref.dtype)


def _sb_attention(q, k, v):
    b, nh, s_len, d = q.shape
    blk = ATT_BLOCK
    return pl.pallas_call(
        functools.partial(_sb_kernel, blk=blk),
        grid=(b, nh, s_len // blk),
        in_specs=[_head_spec(blk, d), _full_spec(s_len, d), _full_spec(s_len, d)],
        out_specs=_head_spec(blk, d),
        out_shape=jax.ShapeDtypeStruct((b, nh, s_len, d), BF16),
        compiler_params=_cparams(("arbitrary",) * 3),
        name="sb",
    )(q, k, v)


def _swa_kernel(slopes_ref, sinks_ref, q_ref, kp_ref, kc_ref, vp_ref, vc_ref,
                pq_ref, pkp_ref, pkc_ref, o_ref):
    i = pl.program_id(1)
    w = SW_WINDOW
    pk = jnp.concatenate([pkp_ref[...], pkc_ref[...]], axis=1)
    dist = pq_ref[...] - pk
    qi = lax.broadcasted_iota(jnp.int32, (w, 2 * w), 0)
    kj = lax.broadcasted_iota(jnp.int32, (w, 2 * w), 1)
    mask = (kj > qi) & (kj <= qi + w) & ((kj >= w) | (i > 0))
    for g in range(SW_KV_HEADS):
        q = q_ref[0, g * SW_GROUP:(g + 1) * SW_GROUP].reshape(SW_GROUP * w, SW_DIM)
        k = jnp.concatenate([kp_ref[0, g], kc_ref[0, g]], axis=0)
        v = jnp.concatenate([vp_ref[0, g], vc_ref[0, g]], axis=0)
        s = _nt_dot(q, k)
        for j in range(SW_GROUP):
            head = g * SW_GROUP + j
            sj = jnp.where(mask, s[j * w:(j + 1) * w] - slopes_ref[head] * dist, NEG_INF)
            sink = sinks_ref[head]
            m = jnp.maximum(jnp.max(sj, axis=-1, keepdims=True), sink)
            p = jnp.exp(sj - m)
            denom = jnp.sum(p, axis=-1, keepdims=True) + jnp.exp(sink - m)
            o = jnp.dot(p.astype(BF16), v, preferred_element_type=F32) / denom
            o_ref[0, head] = o.astype(o_ref.dtype)


def _swa_attention(q, k, v, slopes, sinks, pos_col, pos_row):
    b, _, s_len, d = q.shape
    w = SW_WINDOW
    prev = lambda i: jnp.maximum(i - 1, 0)
    smem = pl.BlockSpec(memory_space=pltpu.SMEM)
    kv_prev = pl.BlockSpec((1, SW_KV_HEADS, w, d), lambda bb, i: (bb, 0, prev(i), 0))
    kv_cur = pl.BlockSpec((1, SW_KV_HEADS, w, d), lambda bb, i: (bb, 0, i, 0))
    return pl.pallas_call(
        _swa_kernel,
        grid=(b, s_len // w),
        in_specs=[smem, smem,
                  pl.BlockSpec((1, SW_Q_HEADS, w, d), lambda bb, i: (bb, 0, i, 0)),
                  kv_prev, kv_cur, kv_prev, kv_cur,
                  pl.BlockSpec((w, 1), lambda bb, i: (i, 0)),
                  pl.BlockSpec((1, w), lambda bb, i: (0, prev(i))),
                  pl.BlockSpec((1, w), lambda bb, i: (0, i))],
        out_specs=pl.BlockSpec((1, SW_Q_HEADS, w, d), lambda bb, i: (bb, 0, i, 0)),
        out_shape=jax.ShapeDtypeStruct((b, SW_Q_HEADS, s_len, d), BF16),
        compiler_params=_cparams(("arbitrary",) * 2),
        name="swa",
    )(slopes, sinks, q, k, k, v, v, pos_col, pos_row, pos_row)


def _diff_kernel(slopes_ref, q_ref, k_ref, v_ref, pk_ref, lq1_ref, lk1_ref, lq2_ref, lk2_ref,
                 subln_ref, o_ref, *, blk, lam_init):
    h = pl.program_id(1)
    qi = pl.program_id(2)
    q = q_ref[0, 0]
    lane = lax.broadcasted_iota(jnp.int32, q.shape, 1)
    zero = jnp.zeros_like(q)
    qq = jnp.concatenate([jnp.where(lane < DF_DIM, q, zero), jnp.where(lane >= DF_DIM, q, zero)], axis=0)
    slope = slopes_ref[h]

    def block(ki, width=blk):
        start = pl.multiple_of(ki * blk, blk)
        k = k_ref[0, 0, pl.ds(start, width), :]
        v = v_ref[0, 0, pl.ds(start, width), :]
        return _nt_dot(qq, k) + slope * pk_ref[:, pl.ds(start, width)], v

    def step(ki, width, carry):
        s, v = block(ki, width)
        return _softmax_step(s, v, *carry)

    init = (jnp.full((2 * blk, 1), NEG_INF, F32), jnp.zeros((2 * blk, LANES), F32))
    carry = _walk_full_blocks(qi, blk, step, init)
    s, v = block(qi)
    row = lax.broadcasted_iota(jnp.int32, (2 * blk, blk), 0)
    col = lax.broadcasted_iota(jnp.int32, (2 * blk, blk), 1)
    causal = (col <= row) & ((row < blk) | (col <= row - blk))
    _, acc = _softmax_step(jnp.where(causal, s, NEG_INF), v, *carry)
    lam = (jnp.exp(jnp.sum(lq1_ref[...] * lk1_ref[...], keepdims=True))
           - jnp.exp(jnp.sum(lq2_ref[...] * lk2_ref[...], keepdims=True)) + lam_init)
    o = _normalised(acc, DF_VDIM)
    o = o[:blk] - lam * o[blk:]
    o = o * lax.rsqrt(jnp.mean(o * o, axis=-1, keepdims=True) + RMS_EPS) * subln_ref[...] * (1.0 - lam_init)
    o_ref[0, 0] = o.astype(o_ref.dtype)


def _diff_attention(q, k, v, slopes, pos_row, lq1, lk1, lq2, lk2, subln, lam_init):
    b, nh, s_len, d = q.shape
    blk = ATT_BLOCK
    smem = pl.BlockSpec(memory_space=pltpu.SMEM)
    vec = lambda n: pl.BlockSpec((1, n), lambda bb, h, i: (0, 0))
    return pl.pallas_call(
        functools.partial(_diff_kernel, blk=blk, lam_init=lam_init),
        grid=(b, nh, s_len // blk),
        in_specs=[smem, _head_spec(blk, d), _full_spec(s_len, d), _full_spec(s_len, LANES),
                  pl.BlockSpec((1, s_len), lambda bb, h, i: (0, 0)),
                  vec(DF_DIM), vec(DF_DIM), vec(DF_DIM), vec(DF_DIM), vec(DF_VDIM)],
        out_specs=_head_spec(blk, DF_VDIM),
        out_shape=jax.ShapeDtypeStruct((b, nh, s_len, DF_VDIM), BF16),
        compiler_params=_cparams(("arbitrary",) * 3),
        name="diff",
    )(slopes, q, k, v, pos_row, lq1, lk1, lq2, lk2, subln)


def _layer_norm(v, g, b):
    mu = jnp.mean(v, axis=-1, keepdims=True)
    c = v - mu
    var = jnp.mean(c * c, axis=-1, keepdims=True)
    return c * lax.rsqrt(var + LN_EPS) * g + b


def _sigmoid(z):
    return 1.0 / (1.0 + jnp.exp(-z))


def _route(logits_t, bias_col):
    scores = _sigmoid(logits_t)
    biased = scores + bias_col
    ng = N_GROUPS
    sc = [scores[j * ng:(j + 1) * ng] for j in range(EXPERTS_PER_GROUP)]
    bs = [biased[j * ng:(j + 1) * ng] for j in range(EXPERTS_PER_GROUP)]
    gscore = None
    for a in range(EXPERTS_PER_GROUP):
        for c in range(a + 1, EXPERTS_PER_GROUP):
            pair = bs[a] + bs[c]
            gscore = pair if gscore is None else jnp.maximum(gscore, pair)
    gid = lax.broadcasted_iota(jnp.int32, gscore.shape, 0)
    best = jnp.max(gscore, axis=0, keepdims=True)
    gsel = jnp.min(jnp.where(gscore == best, gid, ng), axis=0, keepdims=True)
    onehot = gid == gsel
    pick = lambda a: jnp.sum(jnp.where(onehot, a, 0.0), axis=0, keepdims=True)
    b_in = [pick(a) for a in bs]
    s_in = [pick(a) for a in sc]
    chosen = []
    for j in range(EXPERTS_PER_GROUP):
        ahead = jnp.zeros_like(b_in[j])
        for c in range(EXPERTS_PER_GROUP):
            if c == j:
                continue
            beats = (b_in[c] > b_in[j]) | ((b_in[c] == b_in[j]) & (c < j))
            ahead = ahead + beats.astype(F32)
        chosen.append(jnp.where(ahead < 2.0, s_in[j], 0.0))
    total = chosen[0] + chosen[1] + chosen[2] + chosen[3]
    gates = [c / total for c in chosen]
    return gates, onehot


def _merge_kernel(x_ref, oa_ref, ob_ref, oc_ref, od_ref, wg_ref, bg_ref, wa_ref, wb_ref, wc_ref, wd_ref,
                  wo_ref, g_ref, b_ref, wr_ref, rb_ref, x1_ref, gate_ref, route_ref, count_sc, *, alpha):
    tm = x_ref.shape[1]
    x = x_ref[0]
    xb = x.astype(BF16)
    merged = None
    for n, (o_ref, w_ref) in enumerate(((oa_ref, wa_ref), (ob_ref, wb_ref), (oc_ref, wc_ref), (od_ref, wd_ref))):
        cols = slice(n * D_MODEL, (n + 1) * D_MODEL)
        gate = _sigmoid(jnp.dot(xb, wg_ref[:, cols], preferred_element_type=F32) + bg_ref[:, cols])
        heads = jnp.concatenate([o_ref[0, hd] for hd in range(o_ref.shape[1])], axis=1)
        term = gate * jnp.dot(heads, w_ref[...], preferred_element_type=F32)
        merged = term if merged is None else merged + term
    mix = jnp.dot(merged.astype(BF16), wo_ref[...], preferred_element_type=F32)
    y = _layer_norm(alpha * x + mix, g_ref[...], b_ref[...])
    x1_ref[0] = y

    logits_t = _nt_dot(wr_ref[...], y, precision=lax.Precision.HIGHEST)
    gates, onehot = _route(logits_t, rb_ref[...])

    @pl.when(pl.program_id(1) % (MOE_ROWS // tm) == 0)
    def _():
        count_sc[...] = jnp.zeros_like(count_sc)

    onef = onehot.astype(F32)
    before = (lax.broadcasted_iota(jnp.int32, (tm, tm), 0) < lax.broadcasted_iota(jnp.int32, (tm, tm), 1))
    prefix = jnp.dot(onehot.astype(BF16), before.astype(BF16), preferred_element_type=F32)
    seen = count_sc[...]
    rank = jnp.sum(onef * (prefix + seen[:, 0:1]), axis=0, keepdims=True)
    count_sc[...] = seen + jnp.sum(onef, axis=1, keepdims=True)
    group = jnp.sum(onef * lax.broadcasted_iota(jnp.int32, onef.shape, 0).astype(F32), axis=0, keepdims=True)

    rows = [jnp.where(onehot, gate, 0.0) for gate in gates] + [group, rank]
    rows.append(jnp.zeros((LANES - N_EXPERTS - 2, tm), F32))
    gate_ref[0] = jnp.concatenate(rows, axis=0).T
    route_ref[0] = jnp.concatenate([group, rank, jnp.zeros((6, tm), F32)], axis=0)


def _merge(x, oa, ob, oc, od, wg, bg, wa, wb, wc, wd, wo, g, b, wr_t, rb_col, alpha):
    bsz, s_len, _ = x.shape
    tm = TOKEN_BLOCK
    row = lambda w: pl.BlockSpec((1, tm, w), lambda bb, i: (bb, i, 0))
    heads = lambda o: pl.BlockSpec((1, o.shape[1], tm, o.shape[3]), lambda bb, i: (bb, 0, i, 0))
    consts = (wg, bg, wa, wb, wc, wd, wo, g, b, wr_t, rb_col)
    return pl.pallas_call(
        functools.partial(_merge_kernel, alpha=alpha),
        grid=(bsz, s_len // tm),
        in_specs=[row(D_MODEL), heads(oa), heads(ob), heads(oc), heads(od)]
                 + [_const_spec(c.shape) for c in consts],
        out_specs=[row(D_MODEL), row(LANES),
                   pl.BlockSpec((1, 8, tm), lambda bb, i: (bb * (s_len // tm) + i, 0, 0))],
        out_shape=[jax.ShapeDtypeStruct((bsz, s_len, D_MODEL), F32),
                   jax.ShapeDtypeStruct((bsz, s_len, LANES), F32),
                   jax.ShapeDtypeStruct((bsz * s_len // tm, 8, tm), F32)],
        scratch_shapes=[pltpu.VMEM((N_GROUPS, LANES), F32)],
        compiler_params=_cparams(("arbitrary", "arbitrary")),
        name="merge",
    )(x, oa, ob, oc, od, *consts)


def _two_bf16(a):
    hi = a.astype(BF16)
    return hi, (a - hi.astype(F32)).astype(BF16)


def _moe_kernel(x_ref, gate_ref, route_ref, w1_ref, w3_ref, w2_ref, g_ref, b_ref, o_ref,
                acc_ref, xb_ref, ghi_ref, glo_ref, *, alpha):
    grp = pl.program_id(1)

    @pl.when(grp == 0)
    def _():
        acc_ref[...] = jnp.zeros_like(acc_ref)
        xb_ref[...] = x_ref[...].astype(BF16)
        ghi_ref[...], glo_ref[...] = _two_bf16(gate_ref[...])

    tm = x_ref.shape[0]
    n_sub = route_ref.shape[0]
    sub = tm // n_sub
    cap = MOE_CAP
    grp_f = grp.astype(F32)
    in_group_col = gate_ref[:, N_EXPERTS:N_EXPERTS + 1] == grp_f
    rank_col = gate_ref[:, N_EXPERTS + 1:N_EXPERTS + 2]
    in_group_row = [route_ref[j, 0:1, :] == grp_f for j in range(n_sub)]
    rank_row = [route_ref[j, 1:2, :] for j in range(n_sub)]
    count = sum(jnp.sum(m.astype(F32)) for m in in_group_row)
    n_chunks = jnp.ceil(count / cap).astype(jnp.int32)
    w2 = w2_ref[...].reshape(EXPERTS_PER_GROUP * D_EXPERT, D_MODEL)
    lane = lax.broadcasted_iota(jnp.int32, (cap, LANES), 1)

    def chunk(c, carry):
        first = c * cap
        slot_col = (lax.broadcasted_iota(jnp.int32, (cap, 1), 0) + first).astype(F32)
        slot_row = (lax.broadcasted_iota(jnp.int32, (1, cap), 1) + first).astype(F32)
        xg = None
        gg = None
        for j in range(n_sub):
            sel = ((rank_row[j] == slot_col) & in_group_row[j]).astype(BF16)
            tok = slice(j * sub, (j + 1) * sub)
            part = jnp.dot(sel, xb_ref[tok, :], preferred_element_type=F32)
            gpart = (jnp.dot(sel, ghi_ref[tok, :], preferred_element_type=F32)
                     + jnp.dot(sel, glo_ref[tok, :], preferred_element_type=F32))
            xg = part if xg is None else xg + part
            gg = gpart if gg is None else gg + gpart
        xg = xg.astype(BF16)
        hid = []
        for e in range(EXPERTS_PER_GROUP):
            h1 = jnp.dot(xg, w1_ref[e], preferred_element_type=F32)
            h3 = jnp.dot(xg, w3_ref[e], preferred_element_type=F32)
            gate = jnp.sum(jnp.where(lane == e * N_GROUPS + grp, gg, 0.0), axis=1, keepdims=True)
            hid.append((h1 * _sigmoid(h1) * h3 * gate).astype(BF16))
        y = jnp.dot(jnp.concatenate(hid, axis=1), w2, preferred_element_type=F32)
        y_hi, y_lo = _two_bf16(y)
        back = ((rank_col == slot_row) & in_group_col).astype(BF16)
        acc_ref[...] += (jnp.dot(back, y_hi, preferred_element_type=F32)
                         + jnp.dot(back, y_lo, preferred_element_type=F32))
        return carry

    lax.fori_loop(0, n_chunks, chunk, 0)

    @pl.when(grp == N_GROUPS - 1)
    def _():
        o_ref[...] = _layer_norm(alpha * x_ref[...] + acc_ref[...], g_ref[...], b_ref[...])


def _moe(x1, gates, route, w1, w3, w2, g, b, alpha):
    t = x1.shape[0]
    tm = MOE_ROWS
    epg = EXPERTS_PER_GROUP
    n_sub = tm // route.shape[2]
    return pl.pallas_call(
        functools.partial(_moe_kernel, alpha=alpha),
        grid=(t // tm, N_GROUPS),
        in_specs=[pl.BlockSpec((tm, D_MODEL), lambda i, e: (i, 0)),
                  pl.BlockSpec((tm, LANES), lambda i, e: (i, 0)),
                  pl.BlockSpec((n_sub,) + route.shape[1:], lambda i, e: (i, 0, 0)),
                  pl.BlockSpec((epg, D_MODEL, D_EXPERT), lambda i, e: (e, 0, 0)),
                  pl.BlockSpec((epg, D_MODEL, D_EXPERT), lambda i, e: (e, 0, 0)),
                  pl.BlockSpec((epg, D_EXPERT, D_MODEL), lambda i, e: (e, 0, 0)),
                  pl.BlockSpec((1, D_MODEL), lambda i, e: (0, 0)),
                  pl.BlockSpec((1, D_MODEL), lambda i, e: (0, 0))],
        out_specs=pl.BlockSpec((tm, D_MODEL), lambda i, e: (i, 0)),
        out_shape=jax.ShapeDtypeStruct((t, D_MODEL), F32),
        scratch_shapes=[pltpu.VMEM((tm, D_MODEL), F32), pltpu.VMEM((tm, D_MODEL), BF16),
                        pltpu.VMEM((tm, LANES), BF16), pltpu.VMEM((tm, LANES), BF16)],
        compiler_params=_cparams(("arbitrary", "arbitrary")),
        name="moe",
    )(x1, gates, route, w1, w3, w2, g, b)


def _alibi_slopes(n_heads):
    return 2.0 ** (-8.0 * jnp.arange(1, n_heads + 1, dtype=F32) / n_heads)


def _rope_tables(positions):
    half = MLA_ROPE // 2
    inv_freq = ROPE_THETA ** (-jnp.arange(half, dtype=F32) / half)
    ang = positions.astype(F32)[:, None] * inv_freq[None, :]
    cos = jnp.cos(ang)
    sin = jnp.sin(ang)
    return jnp.concatenate([cos, cos], axis=1), jnp.concatenate([sin, sin], axis=1)


def _rotate_half_columns(w):
    half = w.shape[-1] // 2
    return jnp.concatenate([-w[..., half:], w[..., :half]], axis=-1)


def _prep_w_in(w_in):
    split_at = [int(c) for c in np.cumsum(IN_SPLITS)[:-1]]
    parts = jnp.split(w_in, split_at, axis=-1)
    c_q, c_kv, k_rope = parts[0], parts[1], parts[2]
    before = jnp.zeros((w_in.shape[0], MLA_NOPE), w_in.dtype)
    after = jnp.zeros((w_in.shape[0], LANES - MLA_QK), w_in.dtype)
    return jnp.concatenate([c_q, c_kv, before, k_rope, after, before, _rotate_half_columns(k_rope), after]
                           + list(parts[3:]), axis=-1).astype(BF16)


def _pad_heads(w, n_heads, width):
    w3 = w.reshape(w.shape[0], n_heads, width)
    return jnp.pad(w3, ((0, 0), (0, 0), (0, LANES - width))).reshape(w.shape[0], n_heads * LANES)


def _prep_w_uq(w_uq):
    w = w_uq.reshape(MLA_Q_RANK, MLA_HEADS, MLA_QK)
    rot = jnp.concatenate([jnp.zeros_like(w[..., :MLA_NOPE]), _rotate_half_columns(w[..., MLA_NOPE:])], axis=-1)
    return (_pad_heads(w_uq, MLA_HEADS, MLA_QK).astype(BF16),
            _pad_heads(rot.reshape(w_uq.shape), MLA_HEADS, MLA_QK).astype(BF16))


def _prep_w_ukv(w_ukv):
    w = w_ukv.reshape(MLA_KV_RANK, MLA_HEADS, MLA_NOPE + MLA_V)
    wk = w[..., :MLA_NOPE].reshape(MLA_KV_RANK, MLA_HEADS * MLA_NOPE)
    wv = w[..., MLA_NOPE:].reshape(MLA_KV_RANK, MLA_HEADS * MLA_V)
    return _pad_heads(wk, MLA_HEADS, MLA_NOPE).astype(BF16), _pad_heads(wv, MLA_HEADS, MLA_V).astype(BF16)


def kernel(x, positions, w_in, mla_q_norm, mla_kv_norm, w_uq, w_ukv, sw_sinks, df_lq1, df_lk1, df_lq2, df_lk2,
           df_subln, w_gate, b_gate, w_br_a, w_br_b, w_br_c, w_br_d, w_out, ln1_g, ln1_b, w_router, router_bias,
           moe_w1, moe_w3, moe_w2, ln2_g, ln2_b):
    b, s_len, d = x.shape
    depth = w_in.shape[0]
    t = b * s_len
    alpha = (2 * depth) ** 0.25
    assert d == D_MODEL and s_len % TOKEN_BLOCK == 0 and s_len % ATT_BLOCK == 0 and s_len % MOE_ROWS == 0
    assert MOE_ROWS % TOKEN_BLOCK == 0
    assert SB_DIM == SW_DIM == DF_VDIM == 2 * DF_DIM == HEAD_WIDTH == LANES // 2
    assert ATT_BLOCK % SW_WINDOW == 0

    posf = positions.astype(F32)
    pos_col = posf[:, None]
    pos_row = posf[None, :]
    cos, sin = _rope_tables(positions)
    q_scale = MLA_QK ** -0.5
    ones_n = jnp.ones((s_len, MLA_NOPE), F32)
    tail = jnp.zeros((s_len, LANES - MLA_QK), F32)
    cq = jnp.tile(jnp.concatenate([ones_n * q_scale, cos * q_scale, tail], axis=1), (1, MLA_HEADS))
    sq = jnp.tile(jnp.concatenate([0.0 * ones_n, sin * q_scale, tail], axis=1), (1, MLA_HEADS))
    ck = jnp.concatenate([0.0 * ones_n, cos, tail], axis=1)
    sk = jnp.concatenate([0.0 * ones_n, sin, tail], axis=1)
    rscale = jnp.concatenate([
        jnp.full((B_WIDTH,), SB_DIM ** -0.5, F32), jnp.ones((2 * B_WIDTH,), F32),
        jnp.full((C_WIDTH,), SW_DIM ** -0.5, F32), jnp.ones((2 * SW_KV_HEADS * SW_DIM,), F32),
        jnp.full((2 * DF_HEADS * DF_DIM,), DF_DIM ** -0.5, F32), jnp.ones((2 * DF_HEADS * DF_DIM + D_WIDTH,), F32),
    ])[None, :]
    sw_slopes = _alibi_slopes(SW_Q_HEADS)
    df_slopes = _alibi_slopes(DF_HEADS)

    perm = np.arange(N_EXPERTS).reshape(N_GROUPS, EXPERTS_PER_GROUP).T.reshape(-1)
    wr_t = w_router.astype(F32).T[perm]
    rb_col = router_bias.astype(F32)[perm][:, None]

    x3 = x
    for l in range(depth):
        lam_init = 0.8 - 0.6 * math.exp(-0.3 * l)
        wuq, wuqs = _prep_w_uq(w_uq[l])
        wuk, wuv = _prep_w_ukv(w_ukv[l])
        (mla_q, mla_k, mla_v, sb_q, sb_k, sb_v, sw_q, sw_k, sw_v, df_q, df_k, df_v) = _proj(
            x3, _prep_w_in(w_in[l]), mla_q_norm[l][None, :], mla_kv_norm[l][None, :],
            wuq, wuqs, wuk, wuv, cq, sq, ck, sk, rscale)

        o_a = _mla_attention(mla_q, mla_k, mla_v)
        o_b = _sb_attention(sb_q, sb_k, sb_v)
        o_c = _swa_attention(sw_q, sw_k, sw_v, sw_slopes, sw_sinks[l].astype(F32), pos_col, pos_row)
        o_d = _diff_attention(df_q, df_k, df_v, df_slopes, pos_row,
                              df_lq1[l][None, :], df_lk1[l][None, :], df_lq2[l][None, :],
                              df_lk2[l][None, :], df_subln[l][None, :], lam_init)

        x1, gates, route = _merge(x3, o_a, o_b, o_c, o_d, w_gate[l].astype(BF16), b_gate[l][None, :],
                           w_br_a[l].astype(BF16), w_br_b[l].astype(BF16), w_br_c[l].astype(BF16),
                           w_br_d[l].astype(BF16), w_out[l].astype(BF16), ln1_g[l][None, :], ln1_b[l][None, :],
                           wr_t, rb_col, alpha)
        x2 = _moe(x1.reshape(t, d), gates.reshape(t, LANES), route, moe_w1[l].astype(BF16), moe_w3[l].astype(BF16),
                  moe_w2[l].astype(BF16), ln2_g[l][None, :], ln2_b[l][None, :], alpha)
        x3 = x2.reshape(b, s_len, d)
    return x3
```

```python
import functools
import math

import numpy as np
import jax
import jax.numpy as jnp
from jax import lax
from jax.experimental import pallas as pl
from jax.experimental.pallas import tpu as pltpu

F32 = jnp.float32
BF16 = jnp.bfloat16

D_MODEL = 1024
NEG_INF = -1e30
LN_EPS = 1e-5
RMS_EPS = 1e-6

MLA_HEADS = 4
MLA_Q_RANK = 256
MLA_KV_RANK = 128
MLA_NOPE = 64
MLA_ROPE = 32
MLA_V = 64
MLA_QK = MLA_NOPE + MLA_ROPE
ROPE_THETA = 10000.0

SB_HEADS = 4
SB_DIM = 64

SW_Q_HEADS = 8
SW_KV_HEADS = 2
SW_DIM = 64
SW_WINDOW = 128
SW_GROUP = SW_Q_HEADS // SW_KV_HEADS

DF_HEADS = 4
DF_DIM = 32
DF_VDIM = 2 * DF_DIM

A_WIDTH = MLA_HEADS * MLA_V
B_WIDTH = SB_HEADS * SB_DIM
C_WIDTH = SW_Q_HEADS * SW_DIM
D_WIDTH = DF_HEADS * DF_VDIM
N_BRANCH = 4

IN_SPLITS = (MLA_Q_RANK, MLA_KV_RANK, MLA_ROPE, 3 * B_WIDTH, C_WIDTH, 2 * SW_KV_HEADS * SW_DIM,
             2 * DF_HEADS * DF_DIM, 2 * DF_HEADS * DF_DIM, D_WIDTH)

N_EXPERTS = 32
N_GROUPS = 8
EXPERTS_PER_GROUP = N_EXPERTS // N_GROUPS
D_EXPERT = 256

LANES = 128
HEAD_WIDTH = 64

OFF_CQ = 0
OFF_CKV = OFF_CQ + MLA_Q_RANK
OFF_KR = OFF_CKV + MLA_KV_RANK
OFF_KRS = OFF_KR + LANES
OFF_REST = OFF_KRS + LANES
REST_WIDTH = sum(IN_SPLITS[3:])
IN_PAD_WIDTH = OFF_REST + REST_WIDTH

TOKEN_BLOCK = 512
ATT_BLOCK = 512
SUFFIX_BLOCK = 256
SB_WALK = 256
EXP_UNDERFLOW = -104.0
MOE_ROWS = 1024
MOE_CAP = 192
VMEM_LIMIT = 56 * 1024 * 1024


def _cparams(sem):
    return pltpu.CompilerParams(dimension_semantics=sem, vmem_limit_bytes=VMEM_LIMIT)


def _nt_dot(a, b, **kw):
    return lax.dot_general(a, b, (((1,), (1,)), ((), ())), preferred_element_type=F32, **kw)


def _const_spec(shape):
    nd = len(shape)
    return pl.BlockSpec(shape, lambda *_: (0,) * nd)


def _split_heads(val, out_ref, n_heads, width, ones_lane=None):
    for h in range(n_heads):
        tile = val[:, (h // 2) * LANES:(h // 2 + 1) * LANES]
        if h % 2:
            tile = pltpu.roll(tile, LANES - width, axis=1)
        if ones_lane is None:
            out_ref[0, h] = tile[:, :width].astype(out_ref.dtype)
        else:
            keep = lax.broadcasted_iota(jnp.int32, tile.shape, 1) < width
            out_ref[0, h] = (jnp.where(keep, tile, 0.0) + ones_lane).astype(out_ref.dtype)


def _proj_kernel(x_ref, win_ref, qn_ref, kvn_ref, wuq_ref, wuqs_ref, wuk_ref, wuv_ref,
                 cq_ref, sq_ref, ck_ref, sk_ref, rscale_ref,
                 mq_ref, mk_ref, mv_ref, sbq_ref, sbk_ref, sbv_ref, swq_ref, swk_ref, swv_ref,
                 dfq_ref, dfk_ref, dfv_ref):
    xb = x_ref[0].astype(BF16)
    h = jnp.dot(xb, win_ref[...], preferred_element_type=F32)
    c_q = h[:, OFF_CQ:OFF_CQ + MLA_Q_RANK]
    c_kv = h[:, OFF_CKV:OFF_CKV + MLA_KV_RANK]
    nq = c_q * lax.rsqrt(jnp.mean(c_q * c_q, axis=-1, keepdims=True) + RMS_EPS) * qn_ref[...]
    nqb = nq.astype(BF16)
    q = jnp.dot(nqb, wuq_ref[...], preferred_element_type=F32)
    qs = jnp.dot(nqb, wuqs_ref[...], preferred_element_type=F32)
    q = q * cq_ref[...] + qs * sq_ref[...]
    nkv = c_kv * lax.rsqrt(jnp.mean(c_kv * c_kv, axis=-1, keepdims=True) + RMS_EPS) * kvn_ref[...]
    nkvb = nkv.astype(BF16)
    kn = jnp.dot(nkvb, wuk_ref[...], preferred_element_type=F32)
    vv = jnp.dot(nkvb, wuv_ref[...], preferred_element_type=F32)
    kr = h[:, OFF_KR:OFF_KR + LANES] * ck_ref[...] + h[:, OFF_KRS:OFF_KRS + LANES] * sk_ref[...]
    ones_lane = (lax.broadcasted_iota(jnp.int32, (1, LANES), 1) == HEAD_WIDTH).astype(F32)
    for hd in range(MLA_HEADS):
        lanes = slice(hd * LANES, (hd + 1) * LANES)
        mq_ref[0, hd] = q[:, lanes][:, :MLA_QK].astype(mq_ref.dtype)
        mk_ref[0, hd] = (kn[:, lanes] + kr)[:, :MLA_QK].astype(mk_ref.dtype)
        mv_ref[0, hd] = (vv[:, lanes] + ones_lane).astype(mv_ref.dtype)
    rest = h[:, OFF_REST:] * rscale_ref[...]
    o = 0
    for ref, nh in ((sbq_ref, SB_HEADS), (sbk_ref, SB_HEADS), (sbv_ref, SB_HEADS),
                    (swq_ref, SW_Q_HEADS), (swk_ref, SW_KV_HEADS), (swv_ref, SW_KV_HEADS),
                    (dfq_ref, DF_HEADS), (dfk_ref, DF_HEADS), (dfv_ref, DF_HEADS)):
        _split_heads(rest[:, o:o + nh * HEAD_WIDTH], ref, nh, HEAD_WIDTH,
                     ones_lane if ref is dfv_ref else None)
        o += nh * HEAD_WIDTH


def _proj(x, win, qn, kvn, wuq, wuqs, wuk, wuv, cq, sq, ck, sk, rscale):
    b, s_len, _ = x.shape
    tm = TOKEN_BLOCK
    tab = lambda w: pl.BlockSpec((tm, w), lambda bb, i: (i, 0))
    heads = lambda nh, d: pl.BlockSpec((1, nh, tm, d), lambda bb, i: (bb, 0, i, 0))
    shape = lambda nh, d: jax.ShapeDtypeStruct((b, nh, s_len, d), BF16)
    outs = [(MLA_HEADS, MLA_QK), (MLA_HEADS, MLA_QK), (MLA_HEADS, LANES)] + [(SB_HEADS, SB_DIM)] * 3 + [
        (SW_Q_HEADS, SW_DIM), (SW_KV_HEADS, SW_DIM), (SW_KV_HEADS, SW_DIM)] + [
        (DF_HEADS, HEAD_WIDTH), (DF_HEADS, HEAD_WIDTH), (DF_HEADS, LANES)]
    consts = (win, qn, kvn, wuq, wuqs, wuk, wuv)
    return pl.pallas_call(
        _proj_kernel,
        grid=(b, s_len // tm),
        in_specs=[pl.BlockSpec((1, tm, D_MODEL), lambda bb, i: (bb, i, 0))]
                 + [_const_spec(c.shape) for c in consts]
                 + [tab(cq.shape[1]), tab(sq.shape[1]), tab(LANES), tab(LANES), _const_spec(rscale.shape)],
        out_specs=[heads(nh, d) for nh, d in outs],
        out_shape=[shape(nh, d) for nh, d in outs],
        compiler_params=_cparams(("arbitrary", "arbitrary")),
        name="proj",
    )(x, *consts, cq, sq, ck, sk, rscale)


def _causal_mask(n):
    row = lax.broadcasted_iota(jnp.int32, (n, n), 0)
    col = lax.broadcasted_iota(jnp.int32, (n, n), 1)
    return row, col


def _softmax_step(s, v_ones, m, acc):
    m_new = jnp.maximum(m, jnp.max(s, axis=-1, keepdims=True))
    p = jnp.exp((s - m_new).astype(BF16))
    alpha = jnp.exp(m - m_new)
    return m_new, alpha * acc + jnp.dot(p, v_ones, preferred_element_type=F32)


def _normalised(acc, width):
    return acc[:, :width] / acc[:, width:width + 1]


def _walk_full_blocks(n_blocks, blk, block, carry):
    quads = n_blocks // 4
    carry = lax.fori_loop(0, quads, lambda j, c: block(4 * j, 4 * blk, c), carry)
    done = 4 * quads
    pair = (n_blocks - done) // 2
    carry = lax.fori_loop(0, pair, lambda _, c: block(done, 2 * blk, c), carry)
    done = done + 2 * pair
    return lax.fori_loop(done, n_blocks, lambda ki, c: block(ki, blk, c), carry)


def _mla_kernel(q_ref, k_ref, v_ref, o_ref, *, blk):
    qi = pl.program_id(2)
    q = q_ref[0, 0]

    def kv_block(ki, width=blk):
        start = pl.multiple_of(ki * blk, blk)
        return k_ref[0, 0, pl.ds(start, width), :], v_ref[0, 0, pl.ds(start, width), :]

    def block(ki, width, carry):
        k, v = kv_block(ki, width)
        return _softmax_step(_nt_dot(q, k), v, *carry)

    init = (jnp.full((blk, 1), NEG_INF, F32), jnp.zeros((blk, LANES), F32))
    carry = _walk_full_blocks(qi, blk, block, init)
    k, v = kv_block(qi)
    row, col = _causal_mask(blk)
    s = jnp.where(col <= row, _nt_dot(q, k), NEG_INF)
    _, acc = _softmax_step(s, v, *carry)
    o_ref[0, 0] = _normalised(acc, MLA_V).astype(o_ref.dtype)


def _head_spec(blk, d):
    return pl.BlockSpec((1, 1, blk, d), lambda b, h, i: (b, h, i, 0))


def _full_spec(s_len, d):
    return pl.BlockSpec((1, 1, s_len, d), lambda b, h, i: (b, h, 0, 0))


def _mla_attention(q, k, v):
    b, nh, s_len, _ = q.shape
    blk = ATT_BLOCK
    return pl.pallas_call(
        functools.partial(_mla_kernel, blk=blk),
        grid=(b, nh, s_len // blk),
        in_specs=[_head_spec(blk, MLA_QK), _full_spec(s_len, MLA_QK), _full_spec(s_len, LANES)],
        out_specs=_head_spec(blk, MLA_V),
        out_shape=jax.ShapeDtypeStruct((b, nh, s_len, MLA_V), BF16),
        compiler_params=_cparams(("arbitrary",) * 3),
        name="mla",
    )(q, k, v)


def _log_sigmoid(z):
    return jnp.minimum(z, 0.0) - jnp.log(1.0 + jnp.exp(-jnp.abs(z)))


def _suffix_sum(l1m, upper):
    sub = upper.shape[0]
    n_sub = l1m.shape[1] // sub
    pieces = [None] * n_sub
    right = None
    for c in reversed(range(n_sub)):
        chunk = l1m[:, c * sub:(c + 1) * sub]
        hi = chunk.astype(BF16)
        lo = (chunk - hi.astype(F32)).astype(BF16)
        inner = jnp.dot(hi, upper, preferred_element_type=F32) + jnp.dot(lo, upper, preferred_element_type=F32)
        total = inner[:, 0:1] + chunk[:, 0:1]
        if right is None:
            pieces[c], right = inner, total
        else:
            pieces[c], right = inner + right, right + total
    return jnp.concatenate(pieces, axis=1), right


def _sb_kernel(q_ref, k_ref, v_ref, o_ref, *, blk):
    qi = pl.program_id(2)
    q = q_ref[0, 0]
    srow, scol = _causal_mask(SUFFIX_BLOCK)
    upper = (srow > scol).astype(BF16)

    def kv_rows(start, width):
        start = pl.multiple_of(start, width)
        return k_ref[0, 0, pl.ds(start, width), :], v_ref[0, 0, pl.ds(start, width), :]

    row, col = _causal_mask(blk)
    k, v = kv_rows(qi * blk, blk)
    z = _nt_dot(q, k)
    mask = col < row
    lb = _log_sigmoid(z)
    l1m = jnp.where(mask, lb - z, 0.0)
    suffix, passed = _suffix_sum(l1m, upper)
    a = jnp.where(mask, jnp.exp(lb + suffix), 0.0)
    acc = jnp.dot(a.astype(BF16), v, preferred_element_type=F32)

    n_walk = qi * (blk // SB_WALK)

    def live(carry):
        j, passed, _ = carry
        return (j < n_walk) & (jnp.max(passed) >= EXP_UNDERFLOW)

    def body(carry):
        j, passed, acc = carry
        k, v = kv_rows((n_walk - 1 - j) * SB_WALK, SB_WALK)
        z = _nt_dot(q, k)
        lb = _log_sigmoid(z)
        l1m = lb - z
        suffix, total = _suffix_sum(l1m, upper)
        a = jnp.exp(lb + suffix + passed)
        acc = acc + jnp.dot(a.astype(BF16), v, preferred_element_type=F32)
        return j + 1, passed + total, acc

    _, _, acc = lax.while_loop(live, body, (jnp.int32(0), passed, acc))
    o_ref[0, 0] = acc.astype(o_ref.dtype)


def _sb_attention(q, k, v):
    b, nh, s_len, d = q.shape
    blk = ATT_BLOCK
    return pl.pallas_call(
        functools.partial(_sb_kernel, blk=blk),
        grid=(b, nh, s_len // blk),
        in_specs=[_head_spec(blk, d), _full_spec(s_len, d), _full_spec(s_len, d)],
        out_specs=_head_spec(blk, d),
        out_shape=jax.ShapeDtypeStruct((b, nh, s_len, d), BF16),
        compiler_params=_cparams(("arbitrary",) * 3),
        name="sb",
    )(q, k, v)


def _swa_kernel(slopes_ref, sinks_ref, q_ref, kp_ref, kc_ref, vp_ref, vc_ref,
                pq_ref, pkp_ref, pkc_ref, o_ref):
    i = pl.program_id(1)
    w = SW_WINDOW
    pk = jnp.concatenate([pkp_ref[...], pkc_ref[...]], axis=1)
    dist = pq_ref[...] - pk
    qi = lax.broadcasted_iota(jnp.int32, (w, 2 * w), 0)
    kj = lax.broadcasted_iota(jnp.int32, (w, 2 * w), 1)
    mask = (kj > qi) & (kj <= qi + w) & ((kj >= w) | (i > 0))
    for g in range(SW_KV_HEADS):
        q = q_ref[0, g * SW_GROUP:(g + 1) * SW_GROUP].reshape(SW_GROUP * w, SW_DIM)
        k = jnp.concatenate([kp_ref[0, g], kc_ref[0, g]], axis=0)
        v = jnp.concatenate([vp_ref[0, g], vc_ref[0, g]], axis=0)
        s = _nt_dot(q, k)
        for j in range(SW_GROUP):
            head = g * SW_GROUP + j
            sj = jnp.where(mask, s[j * w:(j + 1) * w] - slopes_ref[head] * dist, NEG_INF)
            sink = sinks_ref[head]
            m = jnp.maximum(jnp.max(sj, axis=-1, keepdims=True), sink)
            p = jnp.exp(sj - m)
            denom = jnp.sum(p, axis=-1, keepdims=True) + jnp.exp(sink - m)
            o = jnp.dot(p.astype(BF16), v, preferred_element_type=F32) / denom
            o_ref[0, head] = o.astype(o_ref.dtype)


def _swa_attention(q, k, v, slopes, sinks, pos_col, pos_row):
    b, _, s_len, d = q.shape
    w = SW_WINDOW
    prev = lambda i: jnp.maximum(i - 1, 0)
    smem = pl.BlockSpec(memory_space=pltpu.SMEM)
    kv_prev = pl.BlockSpec((1, SW_KV_HEADS, w, d), lambda bb, i: (bb, 0, prev(i), 0))
    kv_cur = pl.BlockSpec((1, SW_KV_HEADS, w, d), lambda bb, i: (bb, 0, i, 0))
    return pl.pallas_call(
        _swa_kernel,
        grid=(b, s_len // w),
        in_specs=[smem, smem,
                  pl.BlockSpec((1, SW_Q_HEADS, w, d), lambda bb, i: (bb, 0, i, 0)),
                  kv_prev, kv_cur, kv_prev, kv_cur,
                  pl.BlockSpec((w, 1), lambda bb, i: (i, 0)),
                  pl.BlockSpec((1, w), lambda bb, i: (0, prev(i))),
                  pl.BlockSpec((1, w), lambda bb, i: (0, i))],
        out_specs=pl.BlockSpec((1, SW_Q_HEADS, w, d), lambda bb, i: (bb, 0, i, 0)),
        out_shape=jax.ShapeDtypeStruct((b, SW_Q_HEADS, s_len, d), BF16),
        compiler_params=_cparams(("arbitrary",) * 2),
        name="swa",
    )(slopes, sinks, q, k, k, v, v, pos_col, pos_row, pos_row)


def _diff_kernel(slopes_ref, q_ref, k_ref, v_ref, pk_ref, lq1_ref, lk1_ref, lq2_ref, lk2_ref,
                 subln_ref, o_ref, *, blk, lam_init):
    h = pl.program_id(1)
    qi = pl.program_id(2)
    q = q_ref[0, 0]
    lane = lax.broadcasted_iota(jnp.int32, q.shape, 1)
    zero = jnp.zeros_like(q)
    qq = jnp.concatenate([jnp.where(lane < DF_DIM, q, zero), jnp.where(lane >= DF_DIM, q, zero)], axis=0)
    slope = slopes_ref[h]

    def block(ki, width=blk):
        start = pl.multiple_of(ki * blk, blk)
        k = k_ref[0, 0, pl.ds(start, width), :]
        v = v_ref[0, 0, pl.ds(start, width), :]
        return _nt_dot(qq, k) + slope * pk_ref[:, pl.ds(start, width)], v

    def step(ki, width, carry):
        s, v = block(ki, width)
        return _softmax_step(s, v, *carry)

    init = (jnp.full((2 * blk, 1), NEG_INF, F32), jnp.zeros((2 * blk, LANES), F32))
    carry = _walk_full_blocks(qi, blk, step, init)
    s, v = block(qi)
    row = lax.broadcasted_iota(jnp.int32, (2 * blk, blk), 0)
    col = lax.broadcasted_iota(jnp.int32, (2 * blk, blk), 1)
    causal = (col <= row) & ((row < blk) | (col <= row - blk))
    _, acc = _softmax_step(jnp.where(causal, s, NEG_INF), v, *carry)
    lam = (jnp.exp(jnp.sum(lq1_ref[...] * lk1_ref[...], keepdims=True))
           - jnp.exp(jnp.sum(lq2_ref[...] * lk2_ref[...], keepdims=True)) + lam_init)
    o = _normalised(acc, DF_VDIM)
    o = o[:blk] - lam * o[blk:]
    o = o * lax.rsqrt(jnp.mean(o * o, axis=-1, keepdims=True) + RMS_EPS) * subln_ref[...] * (1.0 - lam_init)
    o_ref[0, 0] = o.astype(o_ref.dtype)


def _diff_attention(q, k, v, slopes, pos_row, lq1, lk1, lq2, lk2, subln, lam_init):
    b, nh, s_len, d = q.shape
    blk = ATT_BLOCK
    smem = pl.BlockSpec(memory_space=pltpu.SMEM)
    vec = lambda n: pl.BlockSpec((1, n), lambda bb, h, i: (0, 0))
    return pl.pallas_call(
        functools.partial(_diff_kernel, blk=blk, lam_init=lam_init),
        grid=(b, nh, s_len // blk),
        in_specs=[smem, _head_spec(blk, d), _full_spec(s_len, d), _full_spec(s_len, LANES),
                  pl.BlockSpec((1, s_len), lambda bb, h, i: (0, 0)),
                  vec(DF_DIM), vec(DF_DIM), vec(DF_DIM), vec(DF_DIM), vec(DF_VDIM)],
        out_specs=_head_spec(blk, DF_VDIM),
        out_shape=jax.ShapeDtypeStruct((b, nh, s_len, DF_VDIM), BF16),
        compiler_params=_cparams(("arbitrary",) * 3),
        name="diff",
    )(slopes, q, k, v, pos_row, lq1, lk1, lq2, lk2, subln)


def _layer_norm(v, g, b):
    mu = jnp.mean(v, axis=-1, keepdims=True)
    c = v - mu
    var = jnp.mean(c * c, axis=-1, keepdims=True)
    return c * lax.rsqrt(var + LN_EPS) * g + b


def _sigmoid(z):
    return 1.0 / (1.0 + jnp.exp(-z))


def _route(logits_t, bias_col):
    scores = _sigmoid(logits_t)
    biased = scores + bias_col
    ng = N_GROUPS
    sc = [scores[j * ng:(j + 1) * ng] for j in range(EXPERTS_PER_GROUP)]
    bs = [biased[j * ng:(j + 1) * ng] for j in range(EXPERTS_PER_GROUP)]
    gscore = None
    for a in range(EXPERTS_PER_GROUP):
        for c in range(a + 1, EXPERTS_PER_GROUP):
            pair = bs[a] + bs[c]
            gscore = pair if gscore is None else jnp.maximum(gscore, pair)
    gid = lax.broadcasted_iota(jnp.int32, gscore.shape, 0)
    best = jnp.max(gscore, axis=0, keepdims=True)
    gsel = jnp.min(jnp.where(gscore == best, gid, ng), axis=0, keepdims=True)
    onehot = gid == gsel
    pick = lambda a: jnp.sum(jnp.where(onehot, a, 0.0), axis=0, keepdims=True)
    b_in = [pick(a) for a in bs]
    s_in = [pick(a) for a in sc]
    chosen = []
    for j in range(EXPERTS_PER_GROUP):
        ahead = jnp.zeros_like(b_in[j])
        for c in range(EXPERTS_PER_GROUP):
            if c == j:
                continue
            beats = (b_in[c] > b_in[j]) | ((b_in[c] == b_in[j]) & (c < j))
            ahead = ahead + beats.astype(F32)
        chosen.append(jnp.where(ahead < 2.0, s_in[j], 0.0))
    total = chosen[0] + chosen[1] + chosen[2] + chosen[3]
    gates = [c / total for c in chosen]
    return gates, onehot


def _merge_kernel(x_ref, oa_ref, ob_ref, oc_ref, od_ref, wg_ref, bg_ref, wa_ref, wb_ref, wc_ref, wd_ref,
                  wo_ref, g_ref, b_ref, wr_ref, rb_ref, x1_ref, gate_ref, route_ref, count_sc, *, alpha):
    tm = x_ref.shape[1]
    x = x_ref[0]
    xb = x.astype(BF16)
    merged = None
    for n, (o_ref, w_ref) in enumerate(((oa_ref, wa_ref), (ob_ref, wb_ref), (oc_ref, wc_ref), (od_ref, wd_ref))):
        cols = slice(n * D_MODEL, (n + 1) * D_MODEL)
        gate = _sigmoid(jnp.dot(xb, wg_ref[:, cols], preferred_element_type=F32) + bg_ref[:, cols])
        heads = jnp.concatenate([o_ref[0, hd] for hd in range(o_ref.shape[1])], axis=1)
        term = gate * jnp.dot(heads, w_ref[...], preferred_element_type=F32)
        merged = term if merged is None else merged + term
    mix = jnp.dot(merged.astype(BF16), wo_ref[...], preferred_element_type=F32)
    y = _layer_norm(alpha * x + mix, g_ref[...], b_ref[...])
    x1_ref[0] = y

    logits_t = _nt_dot(wr_ref[...], y, precision=lax.Precision.HIGHEST)
    gates, onehot = _route(logits_t, rb_ref[...])

    @pl.when(pl.program_id(1) % (MOE_ROWS // tm) == 0)
    def _():
        count_sc[...] = jnp.zeros_like(count_sc)

    onef = onehot.astype(F32)
    before = (lax.broadcasted_iota(jnp.int32, (tm, tm), 0) < lax.broadcasted_iota(jnp.int32, (tm, tm), 1))
    prefix = jnp.dot(onehot.astype(BF16), before.astype(BF16), preferred_element_type=F32)
    seen = count_sc[...]
    rank = jnp.sum(onef * (prefix + seen[:, 0:1]), axis=0, keepdims=True)
    count_sc[...] = seen + jnp.sum(onef, axis=1, keepdims=True)
    group = jnp.sum(onef * lax.broadcasted_iota(jnp.int32, onef.shape, 0).astype(F32), axis=0, keepdims=True)

    rows = [jnp.where(onehot, gate, 0.0) for gate in gates] + [group, rank]
    rows.append(jnp.zeros((LANES - N_EXPERTS - 2, tm), F32))
    gate_ref[0] = jnp.concatenate(rows, axis=0).T
    route_ref[0] = jnp.concatenate([group, rank, jnp.zeros((6, tm), F32)], axis=0)


def _merge(x, oa, ob, oc, od, wg, bg, wa, wb, wc, wd, wo, g, b, wr_t, rb_col, alpha):
    bsz, s_len, _ = x.shape
    tm = TOKEN_BLOCK
    row = lambda w: pl.BlockSpec((1, tm, w), lambda bb, i: (bb, i, 0))
    heads = lambda o: pl.BlockSpec((1, o.shape[1], tm, o.shape[3]), lambda bb, i: (bb, 0, i, 0))
    consts = (wg, bg, wa, wb, wc, wd, wo, g, b, wr_t, rb_col)
    return pl.pallas_call(
        functools.partial(_merge_kernel, alpha=alpha),
        grid=(bsz, s_len // tm),
        in_specs=[row(D_MODEL), heads(oa), heads(ob), heads(oc), heads(od)]
                 + [_const_spec(c.shape) for c in consts],
        out_specs=[row(D_MODEL), row(LANES),
                   pl.BlockSpec((1, 8, tm), lambda bb, i: (bb * (s_len // tm) + i, 0, 0))],
        out_shape=[jax.ShapeDtypeStruct((bsz, s_len, D_MODEL), F32),
                   jax.ShapeDtypeStruct((bsz, s_len, LANES), F32),
                   jax.ShapeDtypeStruct((bsz * s_len // tm, 8, tm), F32)],
        scratch_shapes=[pltpu.VMEM((N_GROUPS, LANES), F32)],
        compiler_params=_cparams(("arbitrary", "arbitrary")),
        name="merge",
    )(x, oa, ob, oc, od, *consts)


def _two_bf16(a):
    hi = a.astype(BF16)
    return hi, (a - hi.astype(F32)).astype(BF16)


def _moe_kernel(x_ref, gate_ref, route_ref, w1_ref, w3_ref, w2_ref, g_ref, b_ref, o_ref,
                acc_ref, xb_ref, ghi_ref, glo_ref, *, alpha):
    grp = pl.program_id(1)

    @pl.when(grp == 0)
    def _():
        acc_ref[...] = jnp.zeros_like(acc_ref)
        xb_ref[...] = x_ref[...].astype(BF16)
        ghi_ref[...], glo_ref[...] = _two_bf16(gate_ref[...])

    tm = x_ref.shape[0]
    n_sub = route_ref.shape[0]
    sub = tm // n_sub
    cap = MOE_CAP
    grp_f = grp.astype(F32)
    in_group_col = gate_ref[:, N_EXPERTS:N_EXPERTS + 1] == grp_f
    rank_col = gate_ref[:, N_EXPERTS + 1:N_EXPERTS + 2]
    in_group_row = [route_ref[j, 0:1, :] == grp_f for j in range(n_sub)]
    rank_row = [route_ref[j, 1:2, :] for j in range(n_sub)]
    count = sum(jnp.sum(m.astype(F32)) for m in in_group_row)
    n_chunks = jnp.ceil(count / cap).astype(jnp.int32)
    w2 = w2_ref[...].reshape(EXPERTS_PER_GROUP * D_EXPERT, D_MODEL)
    lane = lax.broadcasted_iota(jnp.int32, (cap, LANES), 1)

    def chunk(c, carry):
        first = c * cap
        slot_col = (lax.broadcasted_iota(jnp.int32, (cap, 1), 0) + first).astype(F32)
        slot_row = (lax.broadcasted_iota(jnp.int32, (1, cap), 1) + first).astype(F32)
        xg = None
        gg = None
        for j in range(n_sub):
            sel = ((rank_row[j] == slot_col) & in_group_row[j]).astype(BF16)
            tok = slice(j * sub, (j + 1) * sub)
            part = jnp.dot(sel, xb_ref[tok, :], preferred_element_type=F32)
            gpart = (jnp.dot(sel, ghi_ref[tok, :], preferred_element_type=F32)
                     + jnp.dot(sel, glo_ref[tok, :], preferred_element_type=F32))
            xg = part if xg is None else xg + part
            gg = gpart if gg is None else gg + gpart
        xg = xg.astype(BF16)
        hid = []
        for e in range(EXPERTS_PER_GROUP):
            h1 = jnp.dot(xg, w1_ref[e], preferred_element_type=F32)
            h3 = jnp.dot(xg, w3_ref[e], preferred_element_type=F32)
            gate = jnp.sum(jnp.where(lane == e * N_GROUPS + grp, gg, 0.0), axis=1, keepdims=True)
            hid.append((h1 * _sigmoid(h1) * h3 * gate).astype(BF16))
        y = jnp.dot(jnp.concatenate(hid, axis=1), w2, preferred_element_type=F32)
        y_hi, y_lo = _two_bf16(y)
        back = ((rank_col == slot_row) & in_group_col).astype(BF16)
        acc_ref[...] += (jnp.dot(back, y_hi, preferred_element_type=F32)
                         + jnp.dot(back, y_lo, preferred_element_type=F32))
        return carry

    lax.fori_loop(0, n_chunks, chunk, 0)

    @pl.when(grp == N_GROUPS - 1)
    def _():
        o_ref[...] = _layer_norm(alpha * x_ref[...] + acc_ref[...], g_ref[...], b_ref[...])


def _moe(x1, gates, route, w1, w3, w2, g, b, alpha):
    t = x1.shape[0]
    tm = MOE_ROWS
    epg = EXPERTS_PER_GROUP
    n_sub = tm // route.shape[2]
    return pl.pallas_call(
        functools.partial(_moe_kernel, alpha=alpha),
        grid=(t // tm, N_GROUPS),
        in_specs=[pl.BlockSpec((tm, D_MODEL), lambda i, e: (i, 0)),
                  pl.BlockSpec((tm, LANES), lambda i, e: (i, 0)),
                  pl.BlockSpec((n_sub,) + route.shape[1:], lambda i, e: (i, 0, 0)),
                  pl.BlockSpec((epg, D_MODEL, D_EXPERT), lambda i, e: (e, 0, 0)),
                  pl.BlockSpec((epg, D_MODEL, D_EXPERT), lambda i, e: (e, 0, 0)),
                  pl.BlockSpec((epg, D_EXPERT, D_MODEL), lambda i, e: (e, 0, 0)),
                  pl.BlockSpec((1, D_MODEL), lambda i, e: (0, 0)),
                  pl.BlockSpec((1, D_MODEL), lambda i, e: (0, 0))],
        out_specs=pl.BlockSpec((tm, D_MODEL), lambda i, e: (i, 0)),
        out_shape=jax.ShapeDtypeStruct((t, D_MODEL), F32),
        scratch_shapes=[pltpu.VMEM((tm, D_MODEL), F32), pltpu.VMEM((tm, D_MODEL), BF16),
                        pltpu.VMEM((tm, LANES), BF16), pltpu.VMEM((tm, LANES), BF16)],
        compiler_params=_cparams(("arbitrary", "arbitrary")),
        name="moe",
    )(x1, gates, route, w1, w3, w2, g, b)


def _alibi_slopes(n_heads):
    return 2.0 ** (-8.0 * jnp.arange(1, n_heads + 1, dtype=F32) / n_heads)


def _rope_tables(positions):
    half = MLA_ROPE // 2
    inv_freq = ROPE_THETA ** (-jnp.arange(half, dtype=F32) / half)
    ang = positions.astype(F32)[:, None] * inv_freq[None, :]
    cos = jnp.cos(ang)
    sin = jnp.sin(ang)
    return jnp.concatenate([cos, cos], axis=1), jnp.concatenate([sin, sin], axis=1)


def _rotate_half_columns(w):
    half = w.shape[-1] // 2
    return jnp.concatenate([-w[..., half:], w[..., :half]], axis=-1)


def _prep_w_in(w_in):
    split_at = [int(c) for c in np.cumsum(IN_SPLITS)[:-1]]
    parts = jnp.split(w_in, split_at, axis=-1)
    c_q, c_kv, k_rope = parts[0], parts[1], parts[2]
    before = jnp.zeros((w_in.shape[0], MLA_NOPE), w_in.dtype)
    after = jnp.zeros((w_in.shape[0], LANES - MLA_QK), w_in.dtype)
    return jnp.concatenate([c_q, c_kv, before, k_rope, after, before, _rotate_half_columns(k_rope), after]
                           + list(parts[3:]), axis=-1).astype(BF16)


def _pad_heads(w, n_heads, width):
    w3 = w.reshape(w.shape[0], n_heads, width)
    return jnp.pad(w3, ((0, 0), (0, 0), (0, LANES - width))).reshape(w.shape[0], n_heads * LANES)


def _prep_w_uq(w_uq):
    w = w_uq.reshape(MLA_Q_RANK, MLA_HEADS, MLA_QK)
    rot = jnp.concatenate([jnp.zeros_like(w[..., :MLA_NOPE]), _rotate_half_columns(w[..., MLA_NOPE:])], axis=-1)
    return (_pad_heads(w_uq, MLA_HEADS, MLA_QK).astype(BF16),
            _pad_heads(rot.reshape(w_uq.shape), MLA_HEADS, MLA_QK).astype(BF16))


def _prep_w_ukv(w_ukv):
    w = w_ukv.reshape(MLA_KV_RANK, MLA_HEADS, MLA_NOPE + MLA_V)
    wk = w[..., :MLA_NOPE].reshape(MLA_KV_RANK, MLA_HEADS * MLA_NOPE)
    wv = w[..., MLA_NOPE:].reshape(MLA_KV_RANK, MLA_HEADS * MLA_V)
    return _pad_heads(wk, MLA_HEADS, MLA_NOPE).astype(BF16), _pad_heads(wv, MLA_HEADS, MLA_V).astype(BF16)


def kernel(x, positions, w_in, mla_q_norm, mla_kv_norm, w_uq, w_ukv, sw_sinks, df_lq1, df_lk1, df_lq2, df_lk2,
           df_subln, w_gate, b_gate, w_br_a, w_br_b, w_br_c, w_br_d, w_out, ln1_g, ln1_b, w_router, router_bias,
           moe_w1, moe_w3, moe_w2, ln2_g, ln2_b):
    b, s_len, d = x.shape
    depth = w_in.shape[0]
    t = b * s_len
    alpha = (2 * depth) ** 0.25
    assert d == D_MODEL and s_len % TOKEN_BLOCK == 0 and s_len % ATT_BLOCK == 0 and s_len % MOE_ROWS == 0
    assert MOE_ROWS % TOKEN_BLOCK == 0
    assert SB_DIM == SW_DIM == DF_VDIM == 2 * DF_DIM == HEAD_WIDTH == LANES // 2
    assert ATT_BLOCK % SW_WINDOW == 0

    posf = positions.astype(F32)
    pos_col = posf[:, None]
    pos_row = posf[None, :]
    cos, sin = _rope_tables(positions)
    q_scale = MLA_QK ** -0.5
    ones_n = jnp.ones((s_len, MLA_NOPE), F32)
    tail = jnp.zeros((s_len, LANES - MLA_QK), F32)
    cq = jnp.tile(jnp.concatenate([ones_n * q_scale, cos * q_scale, tail], axis=1), (1, MLA_HEADS))
    sq = jnp.tile(jnp.concatenate([0.0 * ones_n, sin * q_scale, tail], axis=1), (1, MLA_HEADS))
    ck = jnp.concatenate([0.0 * ones_n, cos, tail], axis=1)
    sk = jnp.concatenate([0.0 * ones_n, sin, tail], axis=1)
    rscale = jnp.concatenate([
        jnp.full((B_WIDTH,), SB_DIM ** -0.5, F32), jnp.ones((2 * B_WIDTH,), F32),
        jnp.full((C_WIDTH,), SW_DIM ** -0.5, F32), jnp.ones((2 * SW_KV_HEADS * SW_DIM,), F32),
        jnp.full((2 * DF_HEADS * DF_DIM,), DF_DIM ** -0.5, F32), jnp.ones((2 * DF_HEADS * DF_DIM + D_WIDTH,), F32),
    ])[None, :]
    sw_slopes = _alibi_slopes(SW_Q_HEADS)
    df_slopes = _alibi_slopes(DF_HEADS)

    perm = np.arange(N_EXPERTS).reshape(N_GROUPS, EXPERTS_PER_GROUP).T.reshape(-1)
    wr_t = w_router.astype(F32).T[perm]
    rb_col = router_bias.astype(F32)[perm][:, None]

    x3 = x
    for l in range(depth):
        lam_init = 0.8 - 0.6 * math.exp(-0.3 * l)
        wuq, wuqs = _prep_w_uq(w_uq[l])
        wuk, wuv = _prep_w_ukv(w_ukv[l])
        (mla_q, mla_k, mla_v, sb_q, sb_k, sb_v, sw_q, sw_k, sw_v, df_q, df_k, df_v) = _proj(
            x3, _prep_w_in(w_in[l]), mla_q_norm[l][None, :], mla_kv_norm[l][None, :],
            wuq, wuqs, wuk, wuv, cq, sq, ck, sk, rscale)

        o_a = _mla_attention(mla_q, mla_k, mla_v)
        o_b = _sb_attention(sb_q, sb_k, sb_v)
        o_c = _swa_attention(sw_q, sw_k, sw_v, sw_slopes, sw_sinks[l].astype(F32), pos_col, pos_row)
        o_d = _diff_attention(df_q, df_k, df_v, df_slopes, pos_row,
                              df_lq1[l][None, :], df_lk1[l][None, :], df_lq2[l][None, :],
                              df_lk2[l][None, :], df_subln[l][None, :], lam_init)

        x1, gates, route = _merge(x3, o_a, o_b, o_c, o_d, w_gate[l].astype(BF16), b_gate[l][None, :],
                           w_br_a[l].astype(BF16), w_br_b[l].astype(BF16), w_br_c[l].astype(BF16),
                           w_br_d[l].astype(BF16), w_out[l].astype(BF16), ln1_g[l][None, :], ln1_b[l][None, :],
                           wr_t, rb_col, alpha)
        x2 = _moe(x1.reshape(t, d), gates.reshape(t, LANES), route, moe_w1[l].astype(BF16), moe_w3[l].astype(BF16),
                  moe_w2[l].astype(BF16), ln2_g[l][None, :], ln2_b[l][None, :], alpha)
        x3 = x2.reshape(b, s_len, d)
    return x3
```

```python
import functools
import math

import numpy as np
import jax
import jax.numpy as jnp
from jax import lax
from jax.experimental import pallas as pl
from jax.experimental.pallas import tpu as pltpu

F32 = jnp.float32
BF16 = jnp.bfloat16

D_MODEL = 1024
NEG_INF = -1e30
LN_EPS = 1e-5
RMS_EPS = 1e-6

MLA_HEADS = 4
MLA_Q_RANK = 256
MLA_KV_RANK = 128
MLA_NOPE = 64
MLA_ROPE = 32
MLA_V = 64
MLA_QK = MLA_NOPE + MLA_ROPE
ROPE_THETA = 10000.0

SB_HEADS = 4
SB_DIM = 64

SW_Q_HEADS = 8
SW_KV_HEADS = 2
SW_DIM = 64
SW_WINDOW = 128
SW_GROUP = SW_Q_HEADS // SW_KV_HEADS

DF_HEADS = 4
DF_DIM = 32
DF_VDIM = 2 * DF_DIM

A_WIDTH = MLA_HEADS * MLA_V
B_WIDTH = SB_HEADS * SB_DIM
C_WIDTH = SW_Q_HEADS * SW_DIM
D_WIDTH = DF_HEADS * DF_VDIM
N_BRANCH = 4

IN_SPLITS = (MLA_Q_RANK, MLA_KV_RANK, MLA_ROPE, 3 * B_WIDTH, C_WIDTH, 2 * SW_KV_HEADS * SW_DIM,
             2 * DF_HEADS * DF_DIM, 2 * DF_HEADS * DF_DIM, D_WIDTH)

N_EXPERTS = 32
N_GROUPS = 8
EXPERTS_PER_GROUP = N_EXPERTS // N_GROUPS
D_EXPERT = 256

LANES = 128
HEAD_WIDTH = 64

OFF_CQ = 0
OFF_CKV = OFF_CQ + MLA_Q_RANK
OFF_KR = OFF_CKV + MLA_KV_RANK
OFF_KRS = OFF_KR + LANES
OFF_REST = OFF_KRS + LANES
REST_WIDTH = sum(IN_SPLITS[3:])
IN_PAD_WIDTH = OFF_REST + REST_WIDTH

TOKEN_BLOCK = 512
ATT_BLOCK = 512
SUFFIX_BLOCK = 256
SB_WALK = 256
EXP_UNDERFLOW = -104.0
MOE_ROWS = 1024
MOE_CAP = 160
VMEM_LIMIT = 56 * 1024 * 1024


def _cparams(sem):
    return pltpu.CompilerParams(dimension_semantics=sem, vmem_limit_bytes=VMEM_LIMIT)


def _nt_dot(a, b, **kw):
    return lax.dot_general(a, b, (((1,), (1,)), ((), ())), preferred_element_type=F32, **kw)


def _const_spec(shape):
    nd = len(shape)
    return pl.BlockSpec(shape, lambda *_: (0,) * nd)


def _split_heads(val, out_ref, n_heads, width, ones_lane=None):
    for h in range(n_heads):
        tile = val[:, (h // 2) * LANES:(h // 2 + 1) * LANES]
        if h % 2:
            tile = pltpu.roll(tile, LANES - width, axis=1)
        if ones_lane is None:
            out_ref[0, h] = tile[:, :width].astype(out_ref.dtype)
        else:
            keep = lax.broadcasted_iota(jnp.int32, tile.shape, 1) < width
            out_ref[0, h] = (jnp.where(keep, tile, 0.0) + ones_lane).astype(out_ref.dtype)


def _proj_kernel(x_ref, win_ref, qn_ref, kvn_ref, wuq_ref, wuqs_ref, wuk_ref, wuv_ref,
                 cq_ref, sq_ref, ck_ref, sk_ref, rscale_ref,
                 mq_ref, mk_ref, mv_ref, sbq_ref, sbk_ref, sbv_ref, swq_ref, swk_ref, swv_ref,
                 dfq_ref, dfk_ref, dfv_ref):
    xb = x_ref[0].astype(BF16)
    h = jnp.dot(xb, win_ref[...], preferred_element_type=F32)
    c_q = h[:, OFF_CQ:OFF_CQ + MLA_Q_RANK]
    c_kv = h[:, OFF_CKV:OFF_CKV + MLA_KV_RANK]
    nq = c_q * lax.rsqrt(jnp.mean(c_q * c_q, axis=-1, keepdims=True) + RMS_EPS) * qn_ref[...]
    nqb = nq.astype(BF16)
    q = jnp.dot(nqb, wuq_ref[...], preferred_element_type=F32)
    qs = jnp.dot(nqb, wuqs_ref[...], preferred_element_type=F32)
    q = q * cq_ref[...] + qs * sq_ref[...]
    nkv = c_kv * lax.rsqrt(jnp.mean(c_kv * c_kv, axis=-1, keepdims=True) + RMS_EPS) * kvn_ref[...]
    nkvb = nkv.astype(BF16)
    kn = jnp.dot(nkvb, wuk_ref[...], preferred_element_type=F32)
    vv = jnp.dot(nkvb, wuv_ref[...], preferred_element_type=F32)
    kr = h[:, OFF_KR:OFF_KR + LANES] * ck_ref[...] + h[:, OFF_KRS:OFF_KRS + LANES] * sk_ref[...]
    ones_lane = (lax.broadcasted_iota(jnp.int32, (1, LANES), 1) == HEAD_WIDTH).astype(F32)
    for hd in range(MLA_HEADS):
        lanes = slice(hd * LANES, (hd + 1) * LANES)
        mq_ref[0, hd] = q[:, lanes][:, :MLA_QK].astype(mq_ref.dtype)
        mk_ref[0, hd] = (kn[:, lanes] + kr)[:, :MLA_QK].astype(mk_ref.dtype)
        mv_ref[0, hd] = (vv[:, lanes] + ones_lane).astype(mv_ref.dtype)
    rest = h[:, OFF_REST:] * rscale_ref[...]
    o = 0
    for ref, nh in ((sbq_ref, SB_HEADS), (sbk_ref, SB_HEADS), (sbv_ref, SB_HEADS),
                    (swq_ref, SW_Q_HEADS), (swk_ref, SW_KV_HEADS), (swv_ref, SW_KV_HEADS),
                    (dfq_ref, DF_HEADS), (dfk_ref, DF_HEADS), (dfv_ref, DF_HEADS)):
        _split_heads(rest[:, o:o + nh * HEAD_WIDTH], ref, nh, HEAD_WIDTH,
                     ones_lane if ref is dfv_ref else None)
        o += nh * HEAD_WIDTH


def _proj(x, win, qn, kvn, wuq, wuqs, wuk, wuv, cq, sq, ck, sk, rscale):
    b, s_len, _ = x.shape
    tm = TOKEN_BLOCK
    tab = lambda w: pl.BlockSpec((tm, w), lambda bb, i: (i, 0))
    heads = lambda nh, d: pl.BlockSpec((1, nh, tm, d), lambda bb, i: (bb, 0, i, 0))
    shape = lambda nh, d: jax.ShapeDtypeStruct((b, nh, s_len, d), BF16)
    outs = [(MLA_HEADS, MLA_QK), (MLA_HEADS, MLA_QK), (MLA_HEADS, LANES)] + [(SB_HEADS, SB_DIM)] * 3 + [
        (SW_Q_HEADS, SW_DIM), (SW_KV_HEADS, SW_DIM), (SW_KV_HEADS, SW_DIM)] + [
        (DF_HEADS, HEAD_WIDTH), (DF_HEADS, HEAD_WIDTH), (DF_HEADS, LANES)]
    consts = (win, qn, kvn, wuq, wuqs, wuk, wuv)
    return pl.pallas_call(
        _proj_kernel,
        grid=(b, s_len // tm),
        in_specs=[pl.BlockSpec((1, tm, D_MODEL), lambda bb, i: (bb, i, 0))]
                 + [_const_spec(c.shape) for c in consts]
                 + [tab(cq.shape[1]), tab(sq.shape[1]), tab(LANES), tab(LANES), _const_spec(rscale.shape)],
        out_specs=[heads(nh, d) for nh, d in outs],
        out_shape=[shape(nh, d) for nh, d in outs],
        compiler_params=_cparams(("arbitrary", "arbitrary")),
        name="proj",
    )(x, *consts, cq, sq, ck, sk, rscale)


def _causal_mask(n):
    row = lax.broadcasted_iota(jnp.int32, (n, n), 0)
    col = lax.broadcasted_iota(jnp.int32, (n, n), 1)
    return row, col


def _softmax_step(s, v_ones, m, acc):
    m_new = jnp.maximum(m, jnp.max(s, axis=-1, keepdims=True))
    p = jnp.exp((s - m_new).astype(BF16))
    alpha = jnp.exp(m - m_new)
    return m_new, alpha * acc + jnp.dot(p, v_ones, preferred_element_type=F32)


def _normalised(acc, width):
    return acc[:, :width] / acc[:, width:width + 1]


def _walk_full_blocks(n_blocks, blk, block, carry):
    quads = n_blocks // 4
    carry = lax.fori_loop(0, quads, lambda j, c: block(4 * j, 4 * blk, c), carry)
    done = 4 * quads
    pair = (n_blocks - done) // 2
    carry = lax.fori_loop(0, pair, lambda _, c: block(done, 2 * blk, c), carry)
    done = done + 2 * pair
    return lax.fori_loop(done, n_blocks, lambda ki, c: block(ki, blk, c), carry)


def _mla_kernel(q_ref, k_ref, v_ref, o_ref, *, blk):
    qi = pl.program_id(2)
    q = q_ref[0, 0]

    def kv_block(ki, width=blk):
        start = pl.multiple_of(ki * blk, blk)
        return k_ref[0, 0, pl.ds(start, width), :], v_ref[0, 0, pl.ds(start, width), :]

    def block(ki, width, carry):
        k, v = kv_block(ki, width)
        return _softmax_step(_nt_dot(q, k), v, *carry)

    init = (jnp.full((blk, 1), NEG_INF, F32), jnp.zeros((blk, LANES), F32))
    carry = _walk_full_blocks(qi, blk, block, init)
    k, v = kv_block(qi)
    row, col = _causal_mask(blk)
    s = jnp.where(col <= row, _nt_dot(q, k), NEG_INF)
    _, acc = _softmax_step(s, v, *carry)
    o_ref[0, 0] = _normalised(acc, MLA_V).astype(o_ref.dtype)


def _head_spec(blk, d):
    return pl.BlockSpec((1, 1, blk, d), lambda b, h, i: (b, h, i, 0))


def _full_spec(s_len, d):
    return pl.BlockSpec((1, 1, s_len, d), lambda b, h, i: (b, h, 0, 0))


def _mla_attention(q, k, v):
    b, nh, s_len, _ = q.shape
    blk = ATT_BLOCK
    return pl.pallas_call(
        functools.partial(_mla_kernel, blk=blk),
        grid=(b, nh, s_len // blk),
        in_specs=[_head_spec(blk, MLA_QK), _full_spec(s_len, MLA_QK), _full_spec(s_len, LANES)],
        out_specs=_head_spec(blk, MLA_V),
        out_shape=jax.ShapeDtypeStruct((b, nh, s_len, MLA_V), BF16),
        compiler_params=_cparams(("arbitrary",) * 3),
        name="mla",
    )(q, k, v)


def _log_sigmoid(z):
    return jnp.minimum(z, 0.0) - jnp.log(1.0 + jnp.exp(-jnp.abs(z)))


def _suffix_sum(l1m, upper):
    sub = upper.shape[0]
    n_sub = l1m.shape[1] // sub
    pieces = [None] * n_sub
    right = None
    for c in reversed(range(n_sub)):
        chunk = l1m[:, c * sub:(c + 1) * sub]
        hi = chunk.astype(BF16)
        lo = (chunk - hi.astype(F32)).astype(BF16)
        inner = jnp.dot(hi, upper, preferred_element_type=F32) + jnp.dot(lo, upper, preferred_element_type=F32)
        total = inner[:, 0:1] + chunk[:, 0:1]
        if right is None:
            pieces[c], right = inner, total
        else:
            pieces[c], right = inner + right, right + total
    return jnp.concatenate(pieces, axis=1), right


def _sb_kernel(q_ref, k_ref, v_ref, o_ref, *, blk):
    qi = pl.program_id(2)
    q = q_ref[0, 0]
    srow, scol = _causal_mask(SUFFIX_BLOCK)
    upper = (srow > scol).astype(BF16)

    def kv_rows(start, width):
        start = pl.multiple_of(start, width)
        return k_ref[0, 0, pl.ds(start, width), :], v_ref[0, 0, pl.ds(start, width), :]

    row, col = _causal_mask(blk)
    k, v = kv_rows(qi * blk, blk)
    z = _nt_dot(q, k)
    mask = col < row
    lb = _log_sigmoid(z)
    l1m = jnp.where(mask, lb - z, 0.0)
    suffix, passed = _suffix_sum(l1m, upper)
    a = jnp.where(mask, jnp.exp(lb + suffix), 0.0)
    acc = jnp.dot(a.astype(BF16), v, preferred_element_type=F32)

    n_walk = qi * (blk // SB_WALK)

    def live(carry):
        j, passed, _ = carry
        return (j < n_walk) & (jnp.max(passed) >= EXP_UNDERFLOW)

    def body(carry):
        j, passed, acc = carry
        k, v = kv_rows((n_walk - 1 - j) * SB_WALK, SB_WALK)
        z = _nt_dot(q, k)
        lb = _log_sigmoid(z)
        l1m = lb - z
        suffix, total = _suffix_sum(l1m, upper)
        a = jnp.exp(lb + suffix + passed)
        acc = acc + jnp.dot(a.astype(BF16), v, preferred_element_type=F32)
        return j + 1, passed + total, acc

    _, _, acc = lax.while_loop(live, body, (jnp.int32(0), passed, acc))
    o_ref[0, 0] = acc.astype(o_ref.dtype)


def _sb_attention(q, k, v):
    b, nh, s_len, d = q.shape
    blk = ATT_BLOCK
    return pl.pallas_call(
        functools.partial(_sb_kernel, blk=blk),
        grid=(b, nh, s_len // blk),
        in_specs=[_head_spec(blk, d), _full_spec(s_len, d), _full_spec(s_len, d)],
        out_specs=_head_spec(blk, d),
        out_shape=jax.ShapeDtypeStruct((b, nh, s_len, d), BF16),
        compiler_params=_cparams(("arbitrary",) * 3),
        name="sb",
    )(q, k, v)


def _swa_kernel(slopes_ref, sinks_ref, q_ref, kp_ref, kc_ref, vp_ref, vc_ref,
                pq_ref, pkp_ref, pkc_ref, o_ref):
    i = pl.program_id(1)
    w = SW_WINDOW
    pk = jnp.concatenate([pkp_ref[...], pkc_ref[...]], axis=1)
    dist = pq_ref[...] - pk
    qi = lax.broadcasted_iota(jnp.int32, (w, 2 * w), 0)
    kj = lax.broadcasted_iota(jnp.int32, (w, 2 * w), 1)
    mask = (kj > qi) & (kj <= qi + w) & ((kj >= w) | (i > 0))
    for g in range(SW_KV_HEADS):
        q = q_ref[0, g * SW_GROUP:(g + 1) * SW_GROUP].reshape(SW_GROUP * w, SW_DIM)
        k = jnp.concatenate([kp_ref[0, g], kc_ref[0, g]], axis=0)
        v = jnp.concatenate([vp_ref[0, g], vc_ref[0, g]], axis=0)
        s = _nt_dot(q, k)
        for j in range(SW_GROUP):
            head = g * SW_GROUP + j
            sj = jnp.where(mask, s[j * w:(j + 1) * w] - slopes_ref[head] * dist, NEG_INF)
            sink = sinks_ref[head]
            m = jnp.maximum(jnp.max(sj, axis=-1, keepdims=True), sink)
            p = jnp.exp(sj - m)
            denom = jnp.sum(p, axis=-1, keepdims=True) + jnp.exp(sink - m)
            o = jnp.dot(p.astype(BF16), v, preferred_element_type=F32) / denom
            o_ref[0, head] = o.astype(o_ref.dtype)


def _swa_attention(q, k, v, slopes, sinks, pos_col, pos_row):
    b, _, s_len, d = q.shape
    w = SW_WINDOW
    prev = lambda i: jnp.maximum(i - 1, 0)
    smem = pl.BlockSpec(memory_space=pltpu.SMEM)
    kv_prev = pl.BlockSpec((1, SW_KV_HEADS, w, d), lambda bb, i: (bb, 0, prev(i), 0))
    kv_cur = pl.BlockSpec((1, SW_KV_HEADS, w, d), lambda bb, i: (bb, 0, i, 0))
    return pl.pallas_call(
        _swa_kernel,
        grid=(b, s_len // w),
        in_specs=[smem, smem,
                  pl.BlockSpec((1, SW_Q_HEADS, w, d), lambda bb, i: (bb, 0, i, 0)),
                  kv_prev, kv_cur, kv_prev, kv_cur,
                  pl.BlockSpec((w, 1), lambda bb, i: (i, 0)),
                  pl.BlockSpec((1, w), lambda bb, i: (0, prev(i))),
                  pl.BlockSpec((1, w), lambda bb, i: (0, i))],
        out_specs=pl.BlockSpec((1, SW_Q_HEADS, w, d), lambda bb, i: (bb, 0, i, 0)),
        out_shape=jax.ShapeDtypeStruct((b, SW_Q_HEADS, s_len, d), BF16),
        compiler_params=_cparams(("arbitrary",) * 2),
        name="swa",
    )(slopes, sinks, q, k, k, v, v, pos_col, pos_row, pos_row)


def _diff_kernel(slopes_ref, q_ref, k_ref, v_ref, pk_ref, lq1_ref, lk1_ref, lq2_ref, lk2_ref,
                 subln_ref, o_ref, *, blk, lam_init):
    h = pl.program_id(1)
    qi = pl.program_id(2)
    q = q_ref[0, 0]
    lane = lax.broadcasted_iota(jnp.int32, q.shape, 1)
    zero = jnp.zeros_like(q)
    qq = jnp.concatenate([jnp.where(lane < DF_DIM, q, zero), jnp.where(lane >= DF_DIM, q, zero)], axis=0)
    slope = slopes_ref[h]
    base = pk_ref[:, pl.ds(pl.multiple_of(qi * blk, blk), LANES)][:, 0:1]

    def block(ki, width=blk):
        start = pl.multiple_of(ki * blk, blk)
        k = k_ref[0, 0, pl.ds(start, width), :]
        v = v_ref[0, 0, pl.ds(start, width), :]
        return _nt_dot(qq, k) + slope * (pk_ref[:, pl.ds(start, width)] - base), v

    def step(ki, width, carry):
        s, v = block(ki, width)
        return _softmax_step(s, v, *carry)

    init = (jnp.full((2 * blk, 1), NEG_INF, F32), jnp.zeros((2 * blk, LANES), F32))
    carry = _walk_full_blocks(qi, blk, step, init)
    s, v = block(qi)
    row = lax.broadcasted_iota(jnp.int32, (2 * blk, blk), 0)
    col = lax.broadcasted_iota(jnp.int32, (2 * blk, blk), 1)
    causal = (col <= row) & ((row < blk) | (col <= row - blk))
    _, acc = _softmax_step(jnp.where(causal, s, NEG_INF), v, *carry)
    lam = (jnp.exp(jnp.sum(lq1_ref[...] * lk1_ref[...], keepdims=True))
           - jnp.exp(jnp.sum(lq2_ref[...] * lk2_ref[...], keepdims=True)) + lam_init)
    o = _normalised(acc, DF_VDIM)
    o = o[:blk] - lam * o[blk:]
    o = o * lax.rsqrt(jnp.mean(o * o, axis=-1, keepdims=True) + RMS_EPS) * subln_ref[...] * (1.0 - lam_init)
    o_ref[0, 0] = o.astype(o_ref.dtype)


def _diff_attention(q, k, v, slopes, pos_row, lq1, lk1, lq2, lk2, subln, lam_init):
    b, nh, s_len, d = q.shape
    blk = ATT_BLOCK
    smem = pl.BlockSpec(memory_space=pltpu.SMEM)
    vec = lambda n: pl.BlockSpec((1, n), lambda bb, h, i: (0, 0))
    return pl.pallas_call(
        functools.partial(_diff_kernel, blk=blk, lam_init=lam_init),
        grid=(b, nh, s_len // blk),
        in_specs=[smem, _head_spec(blk, d), _full_spec(s_len, d), _full_spec(s_len, LANES),
                  pl.BlockSpec((1, s_len), lambda bb, h, i: (0, 0)),
                  vec(DF_DIM), vec(DF_DIM), vec(DF_DIM), vec(DF_DIM), vec(DF_VDIM)],
        out_specs=_head_spec(blk, DF_VDIM),
        out_shape=jax.ShapeDtypeStruct((b, nh, s_len, DF_VDIM), BF16),
        compiler_params=_cparams(("arbitrary",) * 3),
        name="diff",
    )(slopes, q, k, v, pos_row, lq1, lk1, lq2, lk2, subln)


def _layer_norm(v, g, b):
    mu = jnp.mean(v, axis=-1, keepdims=True)
    c = v - mu
    var = jnp.mean(c * c, axis=-1, keepdims=True)
    return c * lax.rsqrt(var + LN_EPS) * g + b


def _sigmoid(z):
    return 1.0 / (1.0 + jnp.exp(-z))


def _route(logits_t, bias_col):
    scores = _sigmoid(logits_t)
    biased = scores + bias_col
    ng = N_GROUPS
    sc = [scores[j * ng:(j + 1) * ng] for j in range(EXPERTS_PER_GROUP)]
    bs = [biased[j * ng:(j + 1) * ng] for j in range(EXPERTS_PER_GROUP)]
    gscore = None
    for a in range(EXPERTS_PER_GROUP):
        for c in range(a + 1, EXPERTS_PER_GROUP):
            pair = bs[a] + bs[c]
            gscore = pair if gscore is None else jnp.maximum(gscore, pair)
    gid = lax.broadcasted_iota(jnp.int32, gscore.shape, 0)
    best = jnp.max(gscore, axis=0, keepdims=True)
    gsel = jnp.min(jnp.where(gscore == best, gid, ng), axis=0, keepdims=True)
    onehot = gid == gsel
    pick = lambda a: jnp.sum(jnp.where(onehot, a, 0.0), axis=0, keepdims=True)
    b_in = [pick(a) for a in bs]
    s_in = [pick(a) for a in sc]
    chosen = []
    for j in range(EXPERTS_PER_GROUP):
        ahead = jnp.zeros_like(b_in[j])
        for c in range(EXPERTS_PER_GROUP):
            if c == j:
                continue
            beats = (b_in[c] > b_in[j]) | ((b_in[c] == b_in[j]) & (c < j))
            ahead = ahead + beats.astype(F32)
        chosen.append(jnp.where(ahead < 2.0, s_in[j], 0.0))
    total = chosen[0] + chosen[1] + chosen[2] + chosen[3]
    gates = [c / total for c in chosen]
    return gates, onehot


def _merge_kernel(x_ref, oa_ref, ob_ref, oc_ref, od_ref, wg_ref, bg_ref, wa_ref, wb_ref, wc_ref, wd_ref,
                  wo_ref, g_ref, b_ref, wr_ref, rb_ref, x1_ref, gate_ref, route_ref, count_sc, *, alpha):
    tm = x_ref.shape[1]
    x = x_ref[0]
    xb = x.astype(BF16)
    merged = None
    for n, (o_ref, w_ref) in enumerate(((oa_ref, wa_ref), (ob_ref, wb_ref), (oc_ref, wc_ref), (od_ref, wd_ref))):
        cols = slice(n * D_MODEL, (n + 1) * D_MODEL)
        gate = _sigmoid(jnp.dot(xb, wg_ref[:, cols], preferred_element_type=F32) + bg_ref[:, cols])
        heads = jnp.concatenate([o_ref[0, hd] for hd in range(o_ref.shape[1])], axis=1)
        term = gate * jnp.dot(heads, w_ref[...], preferred_element_type=F32)
        merged = term if merged is None else merged + term
    mix = jnp.dot(merged.astype(BF16), wo_ref[...], preferred_element_type=F32)
    y = _layer_norm(alpha * x + mix, g_ref[...], b_ref[...])
    x1_ref[0] = y

    logits_t = _nt_dot(wr_ref[...], y, precision=lax.Precision.HIGHEST)
    gates, onehot = _route(logits_t, rb_ref[...])

    @pl.when(pl.program_id(1) % (MOE_ROWS // tm) == 0)
    def _():
        count_sc[...] = jnp.zeros_like(count_sc)

    onef = onehot.astype(F32)
    before = (lax.broadcasted_iota(jnp.int32, (tm, tm), 0) < lax.broadcasted_iota(jnp.int32, (tm, tm), 1))
    prefix = jnp.dot(onehot.astype(BF16), before.astype(BF16), preferred_element_type=F32)
    seen = count_sc[...]
    rank = jnp.sum(onef * (prefix + seen[:, 0:1]), axis=0, keepdims=True)
    count_sc[...] = seen + jnp.sum(onef, axis=1, keepdims=True)
    group = jnp.sum(onef * lax.broadcasted_iota(jnp.int32, onef.shape, 0).astype(F32), axis=0, keepdims=True)

    rows = [jnp.where(onehot, gate, 0.0) for gate in gates] + [group, rank]
    rows.append(jnp.zeros((LANES - N_EXPERTS - 2, tm), F32))
    gate_ref[0] = jnp.concatenate(rows, axis=0).T
    route_ref[0] = jnp.concatenate([group, rank, jnp.zeros((6, tm), F32)], axis=0)


def _merge(x, oa, ob, oc, od, wg, bg, wa, wb, wc, wd, wo, g, b, wr_t, rb_col, alpha):
    bsz, s_len, _ = x.shape
    tm = TOKEN_BLOCK
    row = lambda w: pl.BlockSpec((1, tm, w), lambda bb, i: (bb, i, 0))
    heads = lambda o: pl.BlockSpec((1, o.shape[1], tm, o.shape[3]), lambda bb, i: (bb, 0, i, 0))
    consts = (wg, bg, wa, wb, wc, wd, wo, g, b, wr_t, rb_col)
    return pl.pallas_call(
        functools.partial(_merge_kernel, alpha=alpha),
        grid=(bsz, s_len // tm),
        in_specs=[row(D_MODEL), heads(oa), heads(ob), heads(oc), heads(od)]
                 + [_const_spec(c.shape) for c in consts],
        out_specs=[row(D_MODEL), row(LANES),
                   pl.BlockSpec((1, 8, tm), lambda bb, i: (bb * (s_len // tm) + i, 0, 0))],
        out_shape=[jax.ShapeDtypeStruct((bsz, s_len, D_MODEL), F32),
                   jax.ShapeDtypeStruct((bsz, s_len, LANES), F32),
                   jax.ShapeDtypeStruct((bsz * s_len // tm, 8, tm), F32)],
        scratch_shapes=[pltpu.VMEM((N_GROUPS, LANES), F32)],
        compiler_params=_cparams(("arbitrary", "arbitrary")),
        name="merge",
    )(x, oa, ob, oc, od, *consts)


def _two_bf16(a):
    hi = a.astype(BF16)
    return hi, (a - hi.astype(F32)).astype(BF16)


def _moe_kernel(x_ref, gate_ref, route_ref, w1_ref, w3_ref, w2_ref, g_ref, b_ref, o_ref,
                acc_ref, xb_ref, ghi_ref, glo_ref, *, alpha):
    grp = pl.program_id(1)

    @pl.when(grp == 0)
    def _():
        acc_ref[...] = jnp.zeros_like(acc_ref)
        xb_ref[...] = x_ref[...].astype(BF16)
        ghi_ref[...], glo_ref[...] = _two_bf16(gate_ref[...])

    tm = x_ref.shape[0]
    n_sub = route_ref.shape[0]
    sub = tm // n_sub
    cap = MOE_CAP
    grp_f = grp.astype(F32)
    in_group_col = gate_ref[:, N_EXPERTS:N_EXPERTS + 1] == grp_f
    rank_col = gate_ref[:, N_EXPERTS + 1:N_EXPERTS + 2]
    in_group_row = [route_ref[j, 0:1, :] == grp_f for j in range(n_sub)]
    rank_row = [route_ref[j, 1:2, :] for j in range(n_sub)]
    count = sum(jnp.sum(m.astype(F32)) for m in in_group_row)
    n_chunks = jnp.ceil(count / cap).astype(jnp.int32)
    w2 = w2_ref[...].reshape(EXPERTS_PER_GROUP * D_EXPERT, D_MODEL)
    lane = lax.broadcasted_iota(jnp.int32, (cap, LANES), 1)

    def chunk(c, carry):
        first = c * cap
        slot_col = (lax.broadcasted_iota(jnp.int32, (cap, 1), 0) + first).astype(F32)
        slot_row = (lax.broadcasted_iota(jnp.int32, (1, cap), 1) + first).astype(F32)
        xg = None
        gg = None
        for j in range(n_sub):
            sel = ((rank_row[j] == slot_col) & in_group_row[j]).astype(BF16)
            tok = slice(j * sub, (j + 1) * sub)
            part = jnp.dot(sel, xb_ref[tok, :], preferred_element_type=F32)
            gpart = (jnp.dot(sel, ghi_ref[tok, :], preferred_element_type=F32)
                     + jnp.dot(sel, glo_ref[tok, :], preferred_element_type=F32))
            xg = part if xg is None else xg + part
            gg = gpart if gg is None else gg + gpart
        xg = xg.astype(BF16)
        hid = []
        for e in range(EXPERTS_PER_GROUP):
            h1 = jnp.dot(xg, w1_ref[e], preferred_element_type=F32)
            h3 = jnp.dot(xg, w3_ref[e], preferred_element_type=F32)
            gate = jnp.sum(jnp.where(lane == e * N_GROUPS + grp, gg, 0.0), axis=1, keepdims=True)
            hid.append((h1 * _sigmoid(h1) * h3 * gate).astype(BF16))
        y = jnp.dot(jnp.concatenate(hid, axis=1), w2, preferred_element_type=F32)
        y_hi, y_lo = _two_bf16(y)
        back = ((rank_col == slot_row) & in_group_col).astype(BF16)
        acc_ref[...] += (jnp.dot(back, y_hi, preferred_element_type=F32)
                         + jnp.dot(back, y_lo, preferred_element_type=F32))
        return carry

    lax.fori_loop(0, n_chunks, chunk, 0)

    @pl.when(grp == N_GROUPS - 1)
    def _():
        o_ref[...] = _layer_norm(alpha * x_ref[...] + acc_ref[...], g_ref[...], b_ref[...])


def _moe(x1, gates, route, w1, w3, w2, g, b, alpha):
    t = x1.shape[0]
    tm = MOE_ROWS
    epg = EXPERTS_PER_GROUP
    n_sub = tm // route.shape[2]
    return pl.pallas_call(
        functools.partial(_moe_kernel, alpha=alpha),
        grid=(t // tm, N_GROUPS),
        in_specs=[pl.BlockSpec((tm, D_MODEL), lambda i, e: (i, 0)),
                  pl.BlockSpec((tm, LANES), lambda i, e: (i, 0)),
                  pl.BlockSpec((n_sub,) + route.shape[1:], lambda i, e: (i, 0, 0)),
                  pl.BlockSpec((epg, D_MODEL, D_EXPERT), lambda i, e: (e, 0, 0)),
                  pl.BlockSpec((epg, D_MODEL, D_EXPERT), lambda i, e: (e, 0, 0)),
                  pl.BlockSpec((epg, D_EXPERT, D_MODEL), lambda i, e: (e, 0, 0)),
                  pl.BlockSpec((1, D_MODEL), lambda i, e: (0, 0)),
                  pl.BlockSpec((1, D_MODEL), lambda i, e: (0, 0))],
        out_specs=pl.BlockSpec((tm, D_MODEL), lambda i, e: (i, 0)),
        out_shape=jax.ShapeDtypeStruct((t, D_MODEL), F32),
        scratch_shapes=[pltpu.VMEM((tm, D_MODEL), F32), pltpu.VMEM((tm, D_MODEL), BF16),
                        pltpu.VMEM((tm, LANES), BF16), pltpu.VMEM((tm, LANES), BF16)],
        compiler_params=_cparams(("arbitrary", "arbitrary")),
        name="moe",
    )(x1, gates, route, w1, w3, w2, g, b)


def _alibi_slopes(n_heads):
    return 2.0 ** (-8.0 * jnp.arange(1, n_heads + 1, dtype=F32) / n_heads)


def _rope_tables(positions):
    half = MLA_ROPE // 2
    inv_freq = ROPE_THETA ** (-jnp.arange(half, dtype=F32) / half)
    ang = positions.astype(F32)[:, None] * inv_freq[None, :]
    cos = jnp.cos(ang)
    sin = jnp.sin(ang)
    return jnp.concatenate([cos, cos], axis=1), jnp.concatenate([sin, sin], axis=1)


def _rotate_half_columns(w):
    half = w.shape[-1] // 2
    return jnp.concatenate([-w[..., half:], w[..., :half]], axis=-1)


def _prep_w_in(w_in):
    split_at = [int(c) for c in np.cumsum(IN_SPLITS)[:-1]]
    parts = jnp.split(w_in, split_at, axis=-1)
    c_q, c_kv, k_rope = parts[0], parts[1], parts[2]
    before = jnp.zeros((w_in.shape[0], MLA_NOPE), w_in.dtype)
    after = jnp.zeros((w_in.shape[0], LANES - MLA_QK), w_in.dtype)
    return jnp.concatenate([c_q, c_kv, before, k_rope, after, before, _rotate_half_columns(k_rope), after]
                           + list(parts[3:]), axis=-1).astype(BF16)


def _pad_heads(w, n_heads, width):
    w3 = w.reshape(w.shape[0], n_heads, width)
    return jnp.pad(w3, ((0, 0), (0, 0), (0, LANES - width))).reshape(w.shape[0], n_heads * LANES)


def _prep_w_uq(w_uq):
    w = w_uq.reshape(MLA_Q_RANK, MLA_HEADS, MLA_QK)
    rot = jnp.concatenate([jnp.zeros_like(w[..., :MLA_NOPE]), _rotate_half_columns(w[..., MLA_NOPE:])], axis=-1)
    return (_pad_heads(w_uq, MLA_HEADS, MLA_QK).astype(BF16),
            _pad_heads(rot.reshape(w_uq.shape), MLA_HEADS, MLA_QK).astype(BF16))


def _prep_w_ukv(w_ukv):
    w = w_ukv.reshape(MLA_KV_RANK, MLA_HEADS, MLA_NOPE + MLA_V)
    wk = w[..., :MLA_NOPE].reshape(MLA_KV_RANK, MLA_HEADS * MLA_NOPE)
    wv = w[..., MLA_NOPE:].reshape(MLA_KV_RANK, MLA_HEADS * MLA_V)
    return _pad_heads(wk, MLA_HEADS, MLA_NOPE).astype(BF16), _pad_heads(wv, MLA_HEADS, MLA_V).astype(BF16)


def kernel(x, positions, w_in, mla_q_norm, mla_kv_norm, w_uq, w_ukv, sw_sinks, df_lq1, df_lk1, df_lq2, df_lk2,
           df_subln, w_gate, b_gate, w_br_a, w_br_b, w_br_c, w_br_d, w_out, ln1_g, ln1_b, w_router, router_bias,
           moe_w1, moe_w3, moe_w2, ln2_g, ln2_b):
    b, s_len, d = x.shape
    depth = w_in.shape[0]
    t = b * s_len
    alpha = (2 * depth) ** 0.25
    assert d == D_MODEL and s_len % TOKEN_BLOCK == 0 and s_len % ATT_BLOCK == 0 and s_len % MOE_ROWS == 0
    assert MOE_ROWS % TOKEN_BLOCK == 0
    assert SB_DIM == SW_DIM == DF_VDIM == 2 * DF_DIM == HEAD_WIDTH == LANES // 2
    assert ATT_BLOCK % SW_WINDOW == 0

    posf = positions.astype(F32)
    pos_col = posf[:, None]
    pos_row = posf[None, :]
    cos, sin = _rope_tables(positions)
    q_scale = MLA_QK ** -0.5
    ones_n = jnp.ones((s_len, MLA_NOPE), F32)
    tail = jnp.zeros((s_len, LANES - MLA_QK), F32)
    cq = jnp.tile(jnp.concatenate([ones_n * q_scale, cos * q_scale, tail], axis=1), (1, MLA_HEADS))
    sq = jnp.tile(jnp.concatenate([0.0 * ones_n, sin * q_scale, tail], axis=1), (1, MLA_HEADS))
    ck = jnp.concatenate([0.0 * ones_n, cos, tail], axis=1)
    sk = jnp.concatenate([0.0 * ones_n, sin, tail], axis=1)
    rscale = jnp.concatenate([
        jnp.full((B_WIDTH,), SB_DIM ** -0.5, F32), jnp.ones((2 * B_WIDTH,), F32),
        jnp.full((C_WIDTH,), SW_DIM ** -0.5, F32), jnp.ones((2 * SW_KV_HEADS * SW_DIM,), F32),
        jnp.full((2 * DF_HEADS * DF_DIM,), DF_DIM ** -0.5, F32), jnp.ones((2 * DF_HEADS * DF_DIM + D_WIDTH,), F32),
    ])[None, :]
    sw_slopes = _alibi_slopes(SW_Q_HEADS)
    df_slopes = _alibi_slopes(DF_HEADS)

    perm = np.arange(N_EXPERTS).reshape(N_GROUPS, EXPERTS_PER_GROUP).T.reshape(-1)
    wr_t = w_router.astype(F32).T[perm]
    rb_col = router_bias.astype(F32)[perm][:, None]

    x3 = x
    for l in range(depth):
        lam_init = 0.8 - 0.6 * math.exp(-0.3 * l)
        wuq, wuqs = _prep_w_uq(w_uq[l])
        wuk, wuv = _prep_w_ukv(w_ukv[l])
        (mla_q, mla_k, mla_v, sb_q, sb_k, sb_v, sw_q, sw_k, sw_v, df_q, df_k, df_v) = _proj(
            x3, _prep_w_in(w_in[l]), mla_q_norm[l][None, :], mla_kv_norm[l][None, :],
            wuq, wuqs, wuk, wuv, cq, sq, ck, sk, rscale)

        o_a = _mla_attention(mla_q, mla_k, mla_v)
        o_b = _sb_attention(sb_q, sb_k, sb_v)
        o_c = _swa_attention(sw_q, sw_k, sw_v, sw_slopes, sw_sinks[l].astype(F32), pos_col, pos_row)
        o_d = _diff_attention(df_q, df_k, df_v, df_slopes, pos_row,
                              df_lq1[l][None, :], df_lk1[l][None, :], df_lq2[l][None, :],
                              df_lk2[l][None, :], df_subln[l][None, :], lam_init)

        x1, gates, route = _merge(x3, o_a, o_b, o_c, o_d, w_gate[l].astype(BF16), b_gate[l][None, :],
                           w_br_a[l].astype(BF16), w_br_b[l].astype(BF16), w_br_c[l].astype(BF16),
                           w_br_d[l].astype(BF16), w_out[l].astype(BF16), ln1_g[l][None, :], ln1_b[l][None, :],
                           wr_t, rb_col, alpha)
        x2 = _moe(x1.reshape(t, d), gates.reshape(t, LANES), route, moe_w1[l].astype(BF16), moe_w3[l].astype(BF16),
                  moe_w2[l].astype(BF16), ln2_g[l][None, :], ln2_b[l][None, :], alpha)
        x3 = x2.reshape(b, s_len, d)
    return x3
```

```python
import functools
import math

import numpy as np
import jax
import jax.numpy as jnp
from jax import lax
from jax.experimental import pallas as pl
from jax.experimental.pallas import tpu as pltpu

F32 = jnp.float32
BF16 = jnp.bfloat16

D_MODEL = 1024
NEG_INF = -1e30
LN_EPS = 1e-5
RMS_EPS = 1e-6

MLA_HEADS = 4
MLA_Q_RANK = 256
MLA_KV_RANK = 128
MLA_NOPE = 64
MLA_ROPE = 32
MLA_V = 64
MLA_QK = MLA_NOPE + MLA_ROPE
ROPE_THETA = 10000.0

SB_HEADS = 4
SB_DIM = 64

SW_Q_HEADS = 8
SW_KV_HEADS = 2
SW_DIM = 64
SW_WINDOW = 128
SW_GROUP = SW_Q_HEADS // SW_KV_HEADS

DF_HEADS = 4
DF_DIM = 32
DF_VDIM = 2 * DF_DIM

A_WIDTH = MLA_HEADS * MLA_V
B_WIDTH = SB_HEADS * SB_DIM
C_WIDTH = SW_Q_HEADS * SW_DIM
D_WIDTH = DF_HEADS * DF_VDIM
N_BRANCH = 4

IN_SPLITS = (MLA_Q_RANK, MLA_KV_RANK, MLA_ROPE, 3 * B_WIDTH, C_WIDTH, 2 * SW_KV_HEADS * SW_DIM,
             2 * DF_HEADS * DF_DIM, 2 * DF_HEADS * DF_DIM, D_WIDTH)

N_EXPERTS = 32
N_GROUPS = 8
EXPERTS_PER_GROUP = N_EXPERTS // N_GROUPS
D_EXPERT = 256

LANES = 128
HEAD_WIDTH = 64

OFF_CQ = 0
OFF_CKV = OFF_CQ + MLA_Q_RANK
OFF_KR = OFF_CKV + MLA_KV_RANK
OFF_KRS = OFF_KR + LANES
OFF_REST = OFF_KRS + LANES
REST_WIDTH = sum(IN_SPLITS[3:])
IN_PAD_WIDTH = OFF_REST + REST_WIDTH

TOKEN_BLOCK = 512
ATT_BLOCK = 512
MLA_BLOCK = 1024
SUFFIX_BLOCK = 256
SB_WALK = 256
EXP_UNDERFLOW = -104.0
MOE_ROWS = 1024
MOE_CAP = 160
VMEM_LIMIT = 56 * 1024 * 1024


def _cparams(sem):
    return pltpu.CompilerParams(dimension_semantics=sem, vmem_limit_bytes=VMEM_LIMIT)


def _nt_dot(a, b, **kw):
    return lax.dot_general(a, b, (((1,), (1,)), ((), ())), preferred_element_type=F32, **kw)


def _const_spec(shape):
    nd = len(shape)
    return pl.BlockSpec(shape, lambda *_: (0,) * nd)


def _split_heads(val, out_ref, n_heads, width, ones_lane=None):
    for h in range(n_heads):
        tile = val[:, (h // 2) * LANES:(h // 2 + 1) * LANES]
        if h % 2:
            tile = pltpu.roll(tile, LANES - width, axis=1)
        if ones_lane is None:
            out_ref[0, h] = tile[:, :width].astype(out_ref.dtype)
        else:
            keep = lax.broadcasted_iota(jnp.int32, tile.shape, 1) < width
            out_ref[0, h] = (jnp.where(keep, tile, 0.0) + ones_lane).astype(out_ref.dtype)


def _proj_kernel(x_ref, win_ref, qn_ref, kvn_ref, wuq_ref, wuqs_ref, wuk_ref, wuv_ref,
                 cq_ref, sq_ref, ck_ref, sk_ref, rscale_ref,
                 mq_ref, mk_ref, mv_ref, sbq_ref, sbk_ref, sbv_ref, swq_ref, swk_ref, swv_ref,
                 dfq_ref, dfk_ref, dfv_ref):
    xb = x_ref[0].astype(BF16)
    h = jnp.dot(xb, win_ref[...], preferred_element_type=F32)
    c_q = h[:, OFF_CQ:OFF_CQ + MLA_Q_RANK]
    c_kv = h[:, OFF_CKV:OFF_CKV + MLA_KV_RANK]
    nq = c_q * lax.rsqrt(jnp.mean(c_q * c_q, axis=-1, keepdims=True) + RMS_EPS) * qn_ref[...]
    nqb = nq.astype(BF16)
    q = jnp.dot(nqb, wuq_ref[...], preferred_element_type=F32)
    qs = jnp.dot(nqb, wuqs_ref[...], preferred_element_type=F32)
    q = q * cq_ref[...] + qs * sq_ref[...]
    nkv = c_kv * lax.rsqrt(jnp.mean(c_kv * c_kv, axis=-1, keepdims=True) + RMS_EPS) * kvn_ref[...]
    nkvb = nkv.astype(BF16)
    kn = jnp.dot(nkvb, wuk_ref[...], preferred_element_type=F32)
    vv = jnp.dot(nkvb, wuv_ref[...], preferred_element_type=F32)
    kr = h[:, OFF_KR:OFF_KR + LANES] * ck_ref[...] + h[:, OFF_KRS:OFF_KRS + LANES] * sk_ref[...]
    ones_lane = (lax.broadcasted_iota(jnp.int32, (1, LANES), 1) == HEAD_WIDTH).astype(F32)
    for hd in range(MLA_HEADS):
        lanes = slice(hd * LANES, (hd + 1) * LANES)
        mq_ref[0, hd] = q[:, lanes][:, :MLA_QK].astype(mq_ref.dtype)
        mk_ref[0, hd] = (kn[:, lanes] + kr)[:, :MLA_QK].astype(mk_ref.dtype)
        mv_ref[0, hd] = (vv[:, lanes] + ones_lane).astype(mv_ref.dtype)
    rest = h[:, OFF_REST:] * rscale_ref[...]
    o = 0
    for ref, nh in ((sbq_ref, SB_HEADS), (sbk_ref, SB_HEADS), (sbv_ref, SB_HEADS),
                    (swq_ref, SW_Q_HEADS), (swk_ref, SW_KV_HEADS), (swv_ref, SW_KV_HEADS),
                    (dfq_ref, DF_HEADS), (dfk_ref, DF_HEADS), (dfv_ref, DF_HEADS)):
        _split_heads(rest[:, o:o + nh * HEAD_WIDTH], ref, nh, HEAD_WIDTH,
                     ones_lane if ref is dfv_ref else None)
        o += nh * HEAD_WIDTH


def _proj(x, win, qn, kvn, wuq, wuqs, wuk, wuv, cq, sq, ck, sk, rscale):
    b, s_len, _ = x.shape
    tm = TOKEN_BLOCK
    tab = lambda w: pl.BlockSpec((tm, w), lambda bb, i: (i, 0))
    heads = lambda nh, d: pl.BlockSpec((1, nh, tm, d), lambda bb, i: (bb, 0, i, 0))
    shape = lambda nh, d: jax.ShapeDtypeStruct((b, nh, s_len, d), BF16)
    outs = [(MLA_HEADS, MLA_QK), (MLA_HEADS, MLA_QK), (MLA_HEADS, LANES)] + [(SB_HEADS, SB_DIM)] * 3 + [
        (SW_Q_HEADS, SW_DIM), (SW_KV_HEADS, SW_DIM), (SW_KV_HEADS, SW_DIM)] + [
        (DF_HEADS, HEAD_WIDTH), (DF_HEADS, HEAD_WIDTH), (DF_HEADS, LANES)]
    consts = (win, qn, kvn, wuq, wuqs, wuk, wuv)
    return pl.pallas_call(
        _proj_kernel,
        grid=(b, s_len // tm),
        in_specs=[pl.BlockSpec((1, tm, D_MODEL), lambda bb, i: (bb, i, 0))]
                 + [_const_spec(c.shape) for c in consts]
                 + [tab(cq.shape[1]), tab(sq.shape[1]), tab(LANES), tab(LANES), _const_spec(rscale.shape)],
        out_specs=[heads(nh, d) for nh, d in outs],
        out_shape=[shape(nh, d) for nh, d in outs],
        compiler_params=_cparams(("arbitrary", "arbitrary")),
        name="proj",
    )(x, *consts, cq, sq, ck, sk, rscale)


def _causal_mask(n):
    row = lax.broadcasted_iota(jnp.int32, (n, n), 0)
    col = lax.broadcasted_iota(jnp.int32, (n, n), 1)
    return row, col


def _softmax_step(s, v_ones, m, acc):
    m_new = jnp.maximum(m, jnp.max(s, axis=-1, keepdims=True))
    p = jnp.exp((s - m_new).astype(BF16))
    alpha = jnp.exp(m - m_new)
    return m_new, alpha * acc + jnp.dot(p, v_ones, preferred_element_type=F32)


def _normalised(acc, width):
    return acc[:, :width] / acc[:, width:width + 1]


def _walk_full_blocks(n_blocks, blk, block, carry):
    quads = n_blocks // 4
    carry = lax.fori_loop(0, quads, lambda j, c: block(4 * j, 4 * blk, c), carry)
    done = 4 * quads
    pair = (n_blocks - done) // 2
    carry = lax.fori_loop(0, pair, lambda _, c: block(done, 2 * blk, c), carry)
    done = done + 2 * pair
    return lax.fori_loop(done, n_blocks, lambda ki, c: block(ki, blk, c), carry)


def _mla_kernel(q_ref, k_ref, v_ref, o_ref, *, blk):
    qi = pl.program_id(2)
    q = q_ref[0, 0]

    def kv_block(ki, width=blk):
        start = pl.multiple_of(ki * blk, blk)
        return k_ref[0, 0, pl.ds(start, width), :], v_ref[0, 0, pl.ds(start, width), :]

    def block(ki, width, carry):
        k, v = kv_block(ki, width)
        return _softmax_step(_nt_dot(q, k), v, *carry)

    init = (jnp.full((blk, 1), NEG_INF, F32), jnp.zeros((blk, LANES), F32))
    carry = _walk_full_blocks(qi, blk, block, init)
    k, v = kv_block(qi)
    row, col = _causal_mask(blk)
    s = jnp.where(col <= row, _nt_dot(q, k), NEG_INF)
    _, acc = _softmax_step(s, v, *carry)
    o_ref[0, 0] = _normalised(acc, MLA_V).astype(o_ref.dtype)


def _head_spec(blk, d):
    return pl.BlockSpec((1, 1, blk, d), lambda b, h, i: (b, h, i, 0))


def _full_spec(s_len, d):
    return pl.BlockSpec((1, 1, s_len, d), lambda b, h, i: (b, h, 0, 0))


def _mla_attention(q, k, v):
    b, nh, s_len, _ = q.shape
    blk = MLA_BLOCK
    return pl.pallas_call(
        functools.partial(_mla_kernel, blk=blk),
        grid=(b, nh, s_len // blk),
        in_specs=[_head_spec(blk, MLA_QK), _full_spec(s_len, MLA_QK), _full_spec(s_len, LANES)],
        out_specs=_head_spec(blk, MLA_V),
        out_shape=jax.ShapeDtypeStruct((b, nh, s_len, MLA_V), BF16),
        compiler_params=_cparams(("arbitrary",) * 3),
        name="mla",
    )(q, k, v)


def _log_sigmoid(z):
    return jnp.minimum(z, 0.0) - jnp.log(1.0 + jnp.exp(-jnp.abs(z)))


def _suffix_sum(l1m, upper):
    sub = upper.shape[0]
    n_sub = l1m.shape[1] // sub
    pieces = [None] * n_sub
    right = None
    for c in reversed(range(n_sub)):
        chunk = l1m[:, c * sub:(c + 1) * sub]
        hi = chunk.astype(BF16)
        lo = (chunk - hi.astype(F32)).astype(BF16)
        inner = jnp.dot(hi, upper, preferred_element_type=F32) + jnp.dot(lo, upper, preferred_element_type=F32)
        total = inner[:, 0:1] + chunk[:, 0:1]
        if right is None:
            pieces[c], right = inner, total
        else:
            pieces[c], right = inner + right, right + total
    return jnp.concatenate(pieces, axis=1), right


def _sb_kernel(q_ref, k_ref, v_ref, o_ref, *, blk):
    qi = pl.program_id(2)
    q = q_ref[0, 0]
    srow, scol = _causal_mask(SUFFIX_BLOCK)
    upper = (srow > scol).astype(BF16)

    def kv_rows(start, width):
        start = pl.multiple_of(start, width)
        return k_ref[0, 0, pl.ds(start, width), :], v_ref[0, 0, pl.ds(start, width), :]

    row, col = _causal_mask(blk)
    k, v = kv_rows(qi * blk, blk)
    z = _nt_dot(q, k)
    mask = col < row
    lb = _log_sigmoid(z)
    l1m = jnp.where(mask, lb - z, 0.0)
    suffix, passed = _suffix_sum(l1m, upper)
    a = jnp.where(mask, jnp.exp(lb + suffix), 0.0)
    acc = jnp.dot(a.astype(BF16), v, preferred_element_type=F32)

    n_walk = qi * (blk // SB_WALK)

    def live(carry):
        j, passed, _ = carry
        return (j < n_walk) & (jnp.max(passed) >= EXP_UNDERFLOW)

    def body(carry):
        j, passed, acc = carry
        k, v = kv_rows((n_walk - 1 - j) * SB_WALK, SB_WALK)
        z = _nt_dot(q, k)
        lb = _log_sigmoid(z)
        l1m = lb - z
        suffix, total = _suffix_sum(l1m, upper)
        a = jnp.exp(lb + suffix + passed)
        acc = acc + jnp.dot(a.astype(BF16), v, preferred_element_type=F32)
        return j + 1, passed + total, acc

    _, _, acc = lax.while_loop(live, body, (jnp.int32(0), passed, acc))
    o_ref[0, 0] = acc.astype(o_ref.dtype)


def _sb_attention(q, k, v):
    b, nh, s_len, d = q.shape
    blk = ATT_BLOCK
    return pl.pallas_call(
        functools.partial(_sb_kernel, blk=blk),
        grid=(b, nh, s_len // blk),
        in_specs=[_head_spec(blk, d), _full_spec(s_len, d), _full_spec(s_len, d)],
        out_specs=_head_spec(blk, d),
        out_shape=jax.ShapeDtypeStruct((b, nh, s_len, d), BF16),
        compiler_params=_cparams(("arbitrary",) * 3),
        name="sb",
    )(q, k, v)


def _swa_kernel(slopes_ref, sinks_ref, q_ref, kp_ref, kc_ref, vp_ref, vc_ref,
                pq_ref, pkp_ref, pkc_ref, o_ref):
    i = pl.program_id(1)
    w = SW_WINDOW
    pk = jnp.concatenate([pkp_ref[...], pkc_ref[...]], axis=1)
    dist = pq_ref[...] - pk
    qi = lax.broadcasted_iota(jnp.int32, (w, 2 * w), 0)
    kj = lax.broadcasted_iota(jnp.int32, (w, 2 * w), 1)
    mask = (kj > qi) & (kj <= qi + w) & ((kj >= w) | (i > 0))
    for g in range(SW_KV_HEADS):
        q = q_ref[0, g * SW_GROUP:(g + 1) * SW_GROUP].reshape(SW_GROUP * w, SW_DIM)
        k = jnp.concatenate([kp_ref[0, g], kc_ref[0, g]], axis=0)
        v = jnp.concatenate([vp_ref[0, g], vc_ref[0, g]], axis=0)
        s = _nt_dot(q, k)
        for j in range(SW_GROUP):
            head = g * SW_GROUP + j
            sj = jnp.where(mask, s[j * w:(j + 1) * w] - slopes_ref[head] * dist, NEG_INF)
            sink = sinks_ref[head]
            m = jnp.maximum(jnp.max(sj, axis=-1, keepdims=True), sink)
            p = jnp.exp(sj - m)
            denom = jnp.sum(p, axis=-1, keepdims=True) + jnp.exp(sink - m)
            o = jnp.dot(p.astype(BF16), v, preferred_element_type=F32) / denom
            o_ref[0, head] = o.astype(o_ref.dtype)


def _swa_attention(q, k, v, slopes, sinks, pos_col, pos_row):
    b, _, s_len, d = q.shape
    w = SW_WINDOW
    prev = lambda i: jnp.maximum(i - 1, 0)
    smem = pl.BlockSpec(memory_space=pltpu.SMEM)
    kv_prev = pl.BlockSpec((1, SW_KV_HEADS, w, d), lambda bb, i: (bb, 0, prev(i), 0))
    kv_cur = pl.BlockSpec((1, SW_KV_HEADS, w, d), lambda bb, i: (bb, 0, i, 0))
    return pl.pallas_call(
        _swa_kernel,
        grid=(b, s_len // w),
        in_specs=[smem, smem,
                  pl.BlockSpec((1, SW_Q_HEADS, w, d), lambda bb, i: (bb, 0, i, 0)),
                  kv_prev, kv_cur, kv_prev, kv_cur,
                  pl.BlockSpec((w, 1), lambda bb, i: (i, 0)),
                  pl.BlockSpec((1, w), lambda bb, i: (0, prev(i))),
                  pl.BlockSpec((1, w), lambda bb, i: (0, i))],
        out_specs=pl.BlockSpec((1, SW_Q_HEADS, w, d), lambda bb, i: (bb, 0, i, 0)),
        out_shape=jax.ShapeDtypeStruct((b, SW_Q_HEADS, s_len, d), BF16),
        compiler_params=_cparams(("arbitrary",) * 2),
        name="swa",
    )(slopes, sinks, q, k, k, v, v, pos_col, pos_row, pos_row)


def _diff_kernel(slopes_ref, q_ref, k_ref, v_ref, pk_ref, lq1_ref, lk1_ref, lq2_ref, lk2_ref,
                 subln_ref, o_ref, *, blk, lam_init):
    h = pl.program_id(1)
    qi = pl.program_id(2)
    q = q_ref[0, 0]
    lane = lax.broadcasted_iota(jnp.int32, q.shape, 1)
    zero = jnp.zeros_like(q)
    qq = jnp.concatenate([jnp.where(lane < DF_DIM, q, zero), jnp.where(lane >= DF_DIM, q, zero)], axis=0)
    slope = slopes_ref[h]
    base = pk_ref[:, pl.ds(pl.multiple_of(qi * blk, blk), LANES)][:, 0:1]

    def block(ki, width=blk):
        start = pl.multiple_of(ki * blk, blk)
        k = k_ref[0, 0, pl.ds(start, width), :]
        v = v_ref[0, 0, pl.ds(start, width), :]
        return _nt_dot(qq, k) + slope * (pk_ref[:, pl.ds(start, width)] - base), v

    def step(ki, width, carry):
        s, v = block(ki, width)
        return _softmax_step(s, v, *carry)

    init = (jnp.full((2 * blk, 1), NEG_INF, F32), jnp.zeros((2 * blk, LANES), F32))
    carry = _walk_full_blocks(qi, blk, step, init)
    s, v = block(qi)
    row = lax.broadcasted_iota(jnp.int32, (2 * blk, blk), 0)
    col = lax.broadcasted_iota(jnp.int32, (2 * blk, blk), 1)
    causal = (col <= row) & ((row < blk) | (col <= row - blk))
    _, acc = _softmax_step(jnp.where(causal, s, NEG_INF), v, *carry)
    lam = (jnp.exp(jnp.sum(lq1_ref[...] * lk1_ref[...], keepdims=True))
           - jnp.exp(jnp.sum(lq2_ref[...] * lk2_ref[...], keepdims=True)) + lam_init)
    o = _normalised(acc, DF_VDIM)
    o = o[:blk] - lam * o[blk:]
    o = o * lax.rsqrt(jnp.mean(o * o, axis=-1, keepdims=True) + RMS_EPS) * subln_ref[...] * (1.0 - lam_init)
    o_ref[0, 0] = o.astype(o_ref.dtype)


def _diff_attention(q, k, v, slopes, pos_row, lq1, lk1, lq2, lk2, subln, lam_init):
    b, nh, s_len, d = q.shape
    blk = ATT_BLOCK
    smem = pl.BlockSpec(memory_space=pltpu.SMEM)
    vec = lambda n: pl.BlockSpec((1, n), lambda bb, h, i: (0, 0))
    return pl.pallas_call(
        functools.partial(_diff_kernel, blk=blk, lam_init=lam_init),
        grid=(b, nh, s_len // blk),
        in_specs=[smem, _head_spec(blk, d), _full_spec(s_len, d), _full_spec(s_len, LANES),
                  pl.BlockSpec((1, s_len), lambda bb, h, i: (0, 0)),
                  vec(DF_DIM), vec(DF_DIM), vec(DF_DIM), vec(DF_DIM), vec(DF_VDIM)],
        out_specs=_head_spec(blk, DF_VDIM),
        out_shape=jax.ShapeDtypeStruct((b, nh, s_len, DF_VDIM), BF16),
        compiler_params=_cparams(("arbitrary",) * 3),
        name="diff",
    )(slopes, q, k, v, pos_row, lq1, lk1, lq2, lk2, subln)


def _layer_norm(v, g, b):
    mu = jnp.mean(v, axis=-1, keepdims=True)
    c = v - mu
    var = jnp.mean(c * c, axis=-1, keepdims=True)
    return c * lax.rsqrt(var + LN_EPS) * g + b


def _sigmoid(z):
    return 1.0 / (1.0 + jnp.exp(-z))


def _route(logits_t, bias_col):
    scores = _sigmoid(logits_t)
    biased = scores + bias_col
    ng = N_GROUPS
    sc = [scores[j * ng:(j + 1) * ng] for j in range(EXPERTS_PER_GROUP)]
    bs = [biased[j * ng:(j + 1) * ng] for j in range(EXPERTS_PER_GROUP)]
    gscore = None
    for a in range(EXPERTS_PER_GROUP):
        for c in range(a + 1, EXPERTS_PER_GROUP):
            pair = bs[a] + bs[c]
            gscore = pair if gscore is None else jnp.maximum(gscore, pair)
    gid = lax.broadcasted_iota(jnp.int32, gscore.shape, 0)
    best = jnp.max(gscore, axis=0, keepdims=True)
    gsel = jnp.min(jnp.where(gscore == best, gid, ng), axis=0, keepdims=True)
    onehot = gid == gsel
    pick = lambda a: jnp.sum(jnp.where(onehot, a, 0.0), axis=0, keepdims=True)
    b_in = [pick(a) for a in bs]
    s_in = [pick(a) for a in sc]
    chosen = []
    for j in range(EXPERTS_PER_GROUP):
        ahead = jnp.zeros_like(b_in[j])
        for c in range(EXPERTS_PER_GROUP):
            if c == j:
                continue
            beats = (b_in[c] > b_in[j]) | ((b_in[c] == b_in[j]) & (c < j))
            ahead = ahead + beats.astype(F32)
        chosen.append(jnp.where(ahead < 2.0, s_in[j], 0.0))
    total = chosen[0] + chosen[1] + chosen[2] + chosen[3]
    gates = [c / total for c in chosen]
    return gates, onehot


def _merge_kernel(x_ref, oa_ref, ob_ref, oc_ref, od_ref, wg_ref, bg_ref, wa_ref, wb_ref, wc_ref, wd_ref,
                  wo_ref, g_ref, b_ref, wr_ref, rb_ref, x1_ref, gate_ref, route_ref, count_sc, *, alpha):
    tm = x_ref.shape[1]
    x = x_ref[0]
    xb = x.astype(BF16)
    merged = None
    for n, (o_ref, w_ref) in enumerate(((oa_ref, wa_ref), (ob_ref, wb_ref), (oc_ref, wc_ref), (od_ref, wd_ref))):
        cols = slice(n * D_MODEL, (n + 1) * D_MODEL)
        gate = _sigmoid(jnp.dot(xb, wg_ref[:, cols], preferred_element_type=F32) + bg_ref[:, cols])
        heads = jnp.concatenate([o_ref[0, hd] for hd in range(o_ref.shape[1])], axis=1)
        term = gate * jnp.dot(heads, w_ref[...], preferred_element_type=F32)
        merged = term if merged is None else merged + term
    mix = jnp.dot(merged.astype(BF16), wo_ref[...], preferred_element_type=F32)
    y = _layer_norm(alpha * x + mix, g_ref[...], b_ref[...])
    x1_ref[0] = y

    logits_t = _nt_dot(wr_ref[...], y, precision=lax.Precision.HIGHEST)
    gates, onehot = _route(logits_t, rb_ref[...])

    @pl.when(pl.program_id(1) % (MOE_ROWS // tm) == 0)
    def _():
        count_sc[...] = jnp.zeros_like(count_sc)

    onef = onehot.astype(F32)
    before = (lax.broadcasted_iota(jnp.int32, (tm, tm), 0) < lax.broadcasted_iota(jnp.int32, (tm, tm), 1))
    prefix = jnp.dot(onehot.astype(BF16), before.astype(BF16), preferred_element_type=F32)
    seen = count_sc[...]
    rank = jnp.sum(onef * (prefix + seen[:, 0:1]), axis=0, keepdims=True)
    count_sc[...] = seen + jnp.sum(onef, axis=1, keepdims=True)
    group = jnp.sum(onef * lax.broadcasted_iota(jnp.int32, onef.shape, 0).astype(F32), axis=0, keepdims=True)

    rows = [jnp.where(onehot, gate, 0.0) for gate in gates] + [group, rank]
    rows.append(jnp.zeros((LANES - N_EXPERTS - 2, tm), F32))
    gate_ref[0] = jnp.concatenate(rows, axis=0).T
    route_ref[0] = jnp.concatenate([group, rank, jnp.zeros((6, tm), F32)], axis=0)


def _merge(x, oa, ob, oc, od, wg, bg, wa, wb, wc, wd, wo, g, b, wr_t, rb_col, alpha):
    bsz, s_len, _ = x.shape
    tm = TOKEN_BLOCK
    row = lambda w: pl.BlockSpec((1, tm, w), lambda bb, i: (bb, i, 0))
    heads = lambda o: pl.BlockSpec((1, o.shape[1], tm, o.shape[3]), lambda bb, i: (bb, 0, i, 0))
    consts = (wg, bg, wa, wb, wc, wd, wo, g, b, wr_t, rb_col)
    return pl.pallas_call(
        functools.partial(_merge_kernel, alpha=alpha),
        grid=(bsz, s_len // tm),
        in_specs=[row(D_MODEL), heads(oa), heads(ob), heads(oc), heads(od)]
                 + [_const_spec(c.shape) for c in consts],
        out_specs=[row(D_MODEL), row(LANES),
                   pl.BlockSpec((1, 8, tm), lambda bb, i: (bb * (s_len // tm) + i, 0, 0))],
        out_shape=[jax.ShapeDtypeStruct((bsz, s_len, D_MODEL), F32),
                   jax.ShapeDtypeStruct((bsz, s_len, LANES), F32),
                   jax.ShapeDtypeStruct((bsz * s_len // tm, 8, tm), F32)],
        scratch_shapes=[pltpu.VMEM((N_GROUPS, LANES), F32)],
        compiler_params=_cparams(("arbitrary", "arbitrary")),
        name="merge",
    )(x, oa, ob, oc, od, *consts)


def _two_bf16(a):
    hi = a.astype(BF16)
    return hi, (a - hi.astype(F32)).astype(BF16)


def _moe_kernel(x_ref, gate_ref, route_ref, w1_ref, w3_ref, w2_ref, g_ref, b_ref, o_ref,
                acc_ref, xb_ref, ghi_ref, glo_ref, *, alpha):
    grp = pl.program_id(1)

    @pl.when(grp == 0)
    def _():
        acc_ref[...] = jnp.zeros_like(acc_ref)
        xb_ref[...] = x_ref[...].astype(BF16)
        ghi_ref[...], glo_ref[...] = _two_bf16(gate_ref[...])

    tm = x_ref.shape[0]
    n_sub = route_ref.shape[0]
    sub = tm // n_sub
    cap = MOE_CAP
    grp_f = grp.astype(F32)
    in_group_col = gate_ref[:, N_EXPERTS:N_EXPERTS + 1] == grp_f
    rank_col = gate_ref[:, N_EXPERTS + 1:N_EXPERTS + 2]
    in_group_row = [route_ref[j, 0:1, :] == grp_f for j in range(n_sub)]
    rank_row = [route_ref[j, 1:2, :] for j in range(n_sub)]
    count = sum(jnp.sum(m.astype(F32)) for m in in_group_row)
    n_chunks = jnp.ceil(count / cap).astype(jnp.int32)
    w2 = w2_ref[...].reshape(EXPERTS_PER_GROUP * D_EXPERT, D_MODEL)
    lane = lax.broadcasted_iota(jnp.int32, (cap, LANES), 1)

    def chunk(c, carry):
        first = c * cap
        slot_col = (lax.broadcasted_iota(jnp.int32, (cap, 1), 0) + first).astype(F32)
        slot_row = (lax.broadcasted_iota(jnp.int32, (1, cap), 1) + first).astype(F32)
        xg = None
        gg = None
        for j in range(n_sub):
            sel = ((rank_row[j] == slot_col) & in_group_row[j]).astype(BF16)
            tok = slice(j * sub, (j + 1) * sub)
            part = jnp.dot(sel, xb_ref[tok, :], preferred_element_type=F32)
            gpart = (jnp.dot(sel, ghi_ref[tok, :], preferred_element_type=F32)
                     + jnp.dot(sel, glo_ref[tok, :], preferred_element_type=F32))
            xg = part if xg is None else xg + part
            gg = gpart if gg is None else gg + gpart
        xg = xg.astype(BF16)
        hid = []
        for e in range(EXPERTS_PER_GROUP):
            h1 = jnp.dot(xg, w1_ref[e], preferred_element_type=F32)
            h3 = jnp.dot(xg, w3_ref[e], preferred_element_type=F32)
            gate = jnp.sum(jnp.where(lane == e * N_GROUPS + grp, gg, 0.0), axis=1, keepdims=True)
            hid.append((h1 * _sigmoid(h1) * h3 * gate).astype(BF16))
        y = jnp.dot(jnp.concatenate(hid, axis=1), w2, preferred_element_type=F32)
        y_hi, y_lo = _two_bf16(y)
        back = ((rank_col == slot_row) & in_group_col).astype(BF16)
        acc_ref[...] += (jnp.dot(back, y_hi, preferred_element_type=F32)
                         + jnp.dot(back, y_lo, preferred_element_type=F32))
        return carry

    lax.fori_loop(0, n_chunks, chunk, 0)

    @pl.when(grp == N_GROUPS - 1)
    def _():
        o_ref[...] = _layer_norm(alpha * x_ref[...] + acc_ref[...], g_ref[...], b_ref[...])


def _moe(x1, gates, route, w1, w3, w2, g, b, alpha):
    t = x1.shape[0]
    tm = MOE_ROWS
    epg = EXPERTS_PER_GROUP
    n_sub = tm // route.shape[2]
    return pl.pallas_call(
        functools.partial(_moe_kernel, alpha=alpha),
        grid=(t // tm, N_GROUPS),
        in_specs=[pl.BlockSpec((tm, D_MODEL), lambda i, e: (i, 0)),
                  pl.BlockSpec((tm, LANES), lambda i, e: (i, 0)),
                  pl.BlockSpec((n_sub,) + route.shape[1:], lambda i, e: (i, 0, 0)),
                  pl.BlockSpec((epg, D_MODEL, D_EXPERT), lambda i, e: (e, 0, 0)),
                  pl.BlockSpec((epg, D_MODEL, D_EXPERT), lambda i, e: (e, 0, 0)),
                  pl.BlockSpec((epg, D_EXPERT, D_MODEL), lambda i, e: (e, 0, 0)),
                  pl.BlockSpec((1, D_MODEL), lambda i, e: (0, 0)),
                  pl.BlockSpec((1, D_MODEL), lambda i, e: (0, 0))],
        out_specs=pl.BlockSpec((tm, D_MODEL), lambda i, e: (i, 0)),
        out_shape=jax.ShapeDtypeStruct((t, D_MODEL), F32),
        scratch_shapes=[pltpu.VMEM((tm, D_MODEL), F32), pltpu.VMEM((tm, D_MODEL), BF16),
                        pltpu.VMEM((tm, LANES), BF16), pltpu.VMEM((tm, LANES), BF16)],
        compiler_params=_cparams(("arbitrary", "arbitrary")),
        name="moe",
    )(x1, gates, route, w1, w3, w2, g, b)


def _alibi_slopes(n_heads):
    return 2.0 ** (-8.0 * jnp.arange(1, n_heads + 1, dtype=F32) / n_heads)


def _rope_tables(positions):
    half = MLA_ROPE // 2
    inv_freq = ROPE_THETA ** (-jnp.arange(half, dtype=F32) / half)
    ang = positions.astype(F32)[:, None] * inv_freq[None, :]
    cos = jnp.cos(ang)
    sin = jnp.sin(ang)
    return jnp.concatenate([cos, cos], axis=1), jnp.concatenate([sin, sin], axis=1)


def _rotate_half_columns(w):
    half = w.shape[-1] // 2
    return jnp.concatenate([-w[..., half:], w[..., :half]], axis=-1)


def _prep_w_in(w_in):
    split_at = [int(c) for c in np.cumsum(IN_SPLITS)[:-1]]
    parts = jnp.split(w_in, split_at, axis=-1)
    c_q, c_kv, k_rope = parts[0], parts[1], parts[2]
    before = jnp.zeros((w_in.shape[0], MLA_NOPE), w_in.dtype)
    after = jnp.zeros((w_in.shape[0], LANES - MLA_QK), w_in.dtype)
    return jnp.concatenate([c_q, c_kv, before, k_rope, after, before, _rotate_half_columns(k_rope), after]
                           + list(parts[3:]), axis=-1).astype(BF16)


def _pad_heads(w, n_heads, width):
    w3 = w.reshape(w.shape[0], n_heads, width)
    return jnp.pad(w3, ((0, 0), (0, 0), (0, LANES - width))).reshape(w.shape[0], n_heads * LANES)


def _prep_w_uq(w_uq):
    w = w_uq.reshape(MLA_Q_RANK, MLA_HEADS, MLA_QK)
    rot = jnp.concatenate([jnp.zeros_like(w[..., :MLA_NOPE]), _rotate_half_columns(w[..., MLA_NOPE:])], axis=-1)
    return (_pad_heads(w_uq, MLA_HEADS, MLA_QK).astype(BF16),
            _pad_heads(rot.reshape(w_uq.shape), MLA_HEADS, MLA_QK).astype(BF16))


def _prep_w_ukv(w_ukv):
    w = w_ukv.reshape(MLA_KV_RANK, MLA_HEADS, MLA_NOPE + MLA_V)
    wk = w[..., :MLA_NOPE].reshape(MLA_KV_RANK, MLA_HEADS * MLA_NOPE)
    wv = w[..., MLA_NOPE:].reshape(MLA_KV_RANK, MLA_HEADS * MLA_V)
    return _pad_heads(wk, MLA_HEADS, MLA_NOPE).astype(BF16), _pad_heads(wv, MLA_HEADS, MLA_V).astype(BF16)


def kernel(x, positions, w_in, mla_q_norm, mla_kv_norm, w_uq, w_ukv, sw_sinks, df_lq1, df_lk1, df_lq2, df_lk2,
           df_subln, w_gate, b_gate, w_br_a, w_br_b, w_br_c, w_br_d, w_out, ln1_g, ln1_b, w_router, router_bias,
           moe_w1, moe_w3, moe_w2, ln2_g, ln2_b):
    b, s_len, d = x.shape
    depth = w_in.shape[0]
    t = b * s_len
    alpha = (2 * depth) ** 0.25
    assert d == D_MODEL and s_len % TOKEN_BLOCK == 0 and s_len % ATT_BLOCK == 0 and s_len % MOE_ROWS == 0
    assert MOE_ROWS % TOKEN_BLOCK == 0 and s_len % MLA_BLOCK == 0
    assert SB_DIM == SW_DIM == DF_VDIM == 2 * DF_DIM == HEAD_WIDTH == LANES // 2
    assert ATT_BLOCK % SW_WINDOW == 0

    posf = positions.astype(F32)
    pos_col = posf[:, None]
    pos_row = posf[None, :]
    cos, sin = _rope_tables(positions)
    q_scale = MLA_QK ** -0.5
    ones_n = jnp.ones((s_len, MLA_NOPE), F32)
    tail = jnp.zeros((s_len, LANES - MLA_QK), F32)
    cq = jnp.tile(jnp.concatenate([ones_n * q_scale, cos * q_scale, tail], axis=1), (1, MLA_HEADS))
    sq = jnp.tile(jnp.concatenate([0.0 * ones_n, sin * q_scale, tail], axis=1), (1, MLA_HEADS))
    ck = jnp.concatenate([0.0 * ones_n, cos, tail], axis=1)
    sk = jnp.concatenate([0.0 * ones_n, sin, tail], axis=1)
    rscale = jnp.concatenate([
        jnp.full((B_WIDTH,), SB_DIM ** -0.5, F32), jnp.ones((2 * B_WIDTH,), F32),
        jnp.full((C_WIDTH,), SW_DIM ** -0.5, F32), jnp.ones((2 * SW_KV_HEADS * SW_DIM,), F32),
        jnp.full((2 * DF_HEADS * DF_DIM,), DF_DIM ** -0.5, F32), jnp.ones((2 * DF_HEADS * DF_DIM + D_WIDTH,), F32),
    ])[None, :]
    sw_slopes = _alibi_slopes(SW_Q_HEADS)
    df_slopes = _alibi_slopes(DF_HEADS)

    perm = np.arange(N_EXPERTS).reshape(N_GROUPS, EXPERTS_PER_GROUP).T.reshape(-1)
    wr_t = w_router.astype(F32).T[perm]
    rb_col = router_bias.astype(F32)[perm][:, None]

    x3 = x
    for l in range(depth):
        lam_init = 0.8 - 0.6 * math.exp(-0.3 * l)
        wuq, wuqs = _prep_w_uq(w_uq[l])
        wuk, wuv = _prep_w_ukv(w_ukv[l])
        (mla_q, mla_k, mla_v, sb_q, sb_k, sb_v, sw_q, sw_k, sw_v, df_q, df_k, df_v) = _proj(
            x3, _prep_w_in(w_in[l]), mla_q_norm[l][None, :], mla_kv_norm[l][None, :],
            wuq, wuqs, wuk, wuv, cq, sq, ck, sk, rscale)

        o_a = _mla_attention(mla_q, mla_k, mla_v)
        o_b = _sb_attention(sb_q, sb_k, sb_v)
        o_c = _swa_attention(sw_q, sw_k, sw_v, sw_slopes, sw_sinks[l].astype(F32), pos_col, pos_row)
        o_d = _diff_attention(df_q, df_k, df_v, df_slopes, pos_row,
                              df_lq1[l][None, :], df_lk1[l][None, :], df_lq2[l][None, :],
                              df_lk2[l][None, :], df_subln[l][None, :], lam_init)

        x1, gates, route = _merge(x3, o_a, o_b, o_c, o_d, w_gate[l].astype(BF16), b_gate[l][None, :],
                           w_br_a[l].astype(BF16), w_br_b[l].astype(BF16), w_br_c[l].astype(BF16),
                           w_br_d[l].astype(BF16), w_out[l].astype(BF16), ln1_g[l][None, :], ln1_b[l][None, :],
                           wr_t, rb_col, alpha)
        x2 = _moe(x1.reshape(t, d), gates.reshape(t, LANES), route, moe_w1[l].astype(BF16), moe_w3[l].astype(BF16),
                  moe_w2[l].astype(BF16), ln2_g[l][None, :], ln2_b[l][None, :], alpha)
        x3 = x2.reshape(b, s_len, d)
    return x3
```

```python
import functools
import math

import numpy as np
import jax
import jax.numpy as jnp
from jax import lax
from jax.experimental import pallas as pl
from jax.experimental.pallas import tpu as pltpu

F32 = jnp.float32
BF16 = jnp.bfloat16

D_MODEL = 1024
NEG_INF = -1e30
LN_EPS = 1e-5
RMS_EPS = 1e-6

MLA_HEADS = 4
MLA_Q_RANK = 256
MLA_KV_RANK = 128
MLA_NOPE = 64
MLA_ROPE = 32
MLA_V = 64
MLA_QK = MLA_NOPE + MLA_ROPE
ROPE_THETA = 10000.0

SB_HEADS = 4
SB_DIM = 64

SW_Q_HEADS = 8
SW_KV_HEADS = 2
SW_DIM = 64
SW_WINDOW = 128
SW_GROUP = SW_Q_HEADS // SW_KV_HEADS

DF_HEADS = 4
DF_DIM = 32
DF_VDIM = 2 * DF_DIM

A_WIDTH = MLA_HEADS * MLA_V
B_WIDTH = SB_HEADS * SB_DIM
C_WIDTH = SW_Q_HEADS * SW_DIM
D_WIDTH = DF_HEADS * DF_VDIM
N_BRANCH = 4

IN_SPLITS = (MLA_Q_RANK, MLA_KV_RANK, MLA_ROPE, 3 * B_WIDTH, C_WIDTH, 2 * SW_KV_HEADS * SW_DIM,
             2 * DF_HEADS * DF_DIM, 2 * DF_HEADS * DF_DIM, D_WIDTH)

N_EXPERTS = 32
N_GROUPS = 8
EXPERTS_PER_GROUP = N_EXPERTS // N_GROUPS
D_EXPERT = 256

LANES = 128
HEAD_WIDTH = 64

OFF_CQ = 0
OFF_CKV = OFF_CQ + MLA_Q_RANK
OFF_KR = OFF_CKV + MLA_KV_RANK
OFF_KRS = OFF_KR + LANES
OFF_REST = OFF_KRS + LANES
REST_WIDTH = sum(IN_SPLITS[3:])
IN_PAD_WIDTH = OFF_REST + REST_WIDTH

TOKEN_BLOCK = 512
ATT_BLOCK = 512
MLA_BLOCK = 1024
DIFF_BLOCK = 1024
SUFFIX_BLOCK = 256
SB_WALK = 256
EXP_UNDERFLOW = -104.0
MOE_ROWS = 1024
MOE_CAP = 160
VMEM_LIMIT = 56 * 1024 * 1024


def _cparams(sem):
    return pltpu.CompilerParams(dimension_semantics=sem, vmem_limit_bytes=VMEM_LIMIT)


def _nt_dot(a, b, **kw):
    return lax.dot_general(a, b, (((1,), (1,)), ((), ())), preferred_element_type=F32, **kw)


def _const_spec(shape):
    nd = len(shape)
    return pl.BlockSpec(shape, lambda *_: (0,) * nd)


def _split_heads(val, out_ref, n_heads, width, ones_lane=None):
    for h in range(n_heads):
        tile = val[:, (h // 2) * LANES:(h // 2 + 1) * LANES]
        if h % 2:
            tile = pltpu.roll(tile, LANES - width, axis=1)
        if ones_lane is None:
            out_ref[0, h] = tile[:, :width].astype(out_ref.dtype)
        else:
            keep = lax.broadcasted_iota(jnp.int32, tile.shape, 1) < width
            out_ref[0, h] = (jnp.where(keep, tile, 0.0) + ones_lane).astype(out_ref.dtype)


def _proj_kernel(x_ref, win_ref, qn_ref, kvn_ref, wuq_ref, wuqs_ref, wuk_ref, wuv_ref,
                 cq_ref, sq_ref, ck_ref, sk_ref, rscale_ref,
                 mq_ref, mk_ref, mv_ref, sbq_ref, sbk_ref, sbv_ref, swq_ref, swk_ref, swv_ref,
                 dfq_ref, dfk_ref, dfv_ref):
    xb = x_ref[0].astype(BF16)
    h = jnp.dot(xb, win_ref[...], preferred_element_type=F32)
    c_q = h[:, OFF_CQ:OFF_CQ + MLA_Q_RANK]
    c_kv = h[:, OFF_CKV:OFF_CKV + MLA_KV_RANK]
    nq = c_q * lax.rsqrt(jnp.mean(c_q * c_q, axis=-1, keepdims=True) + RMS_EPS) * qn_ref[...]
    nqb = nq.astype(BF16)
    q = jnp.dot(nqb, wuq_ref[...], preferred_element_type=F32)
    qs = jnp.dot(nqb, wuqs_ref[...], preferred_element_type=F32)
    q = q * cq_ref[...] + qs * sq_ref[...]
    nkv = c_kv * lax.rsqrt(jnp.mean(c_kv * c_kv, axis=-1, keepdims=True) + RMS_EPS) * kvn_ref[...]
    nkvb = nkv.astype(BF16)
    kn = jnp.dot(nkvb, wuk_ref[...], preferred_element_type=F32)
    vv = jnp.dot(nkvb, wuv_ref[...], preferred_element_type=F32)
    kr = h[:, OFF_KR:OFF_KR + LANES] * ck_ref[...] + h[:, OFF_KRS:OFF_KRS + LANES] * sk_ref[...]
    ones_lane = (lax.broadcasted_iota(jnp.int32, (1, LANES), 1) == HEAD_WIDTH).astype(F32)
    for hd in range(MLA_HEADS):
        lanes = slice(hd * LANES, (hd + 1) * LANES)
        mq_ref[0, hd] = q[:, lanes][:, :MLA_QK].astype(mq_ref.dtype)
        mk_ref[0, hd] = (kn[:, lanes] + kr)[:, :MLA_QK].astype(mk_ref.dtype)
        mv_ref[0, hd] = (vv[:, lanes] + ones_lane).astype(mv_ref.dtype)
    rest = h[:, OFF_REST:] * rscale_ref[...]
    o = 0
    for ref, nh in ((sbq_ref, SB_HEADS), (sbk_ref, SB_HEADS), (sbv_ref, SB_HEADS),
                    (swq_ref, SW_Q_HEADS), (swk_ref, SW_KV_HEADS), (swv_ref, SW_KV_HEADS),
                    (dfq_ref, DF_HEADS), (dfk_ref, DF_HEADS), (dfv_ref, DF_HEADS)):
        _split_heads(rest[:, o:o + nh * HEAD_WIDTH], ref, nh, HEAD_WIDTH,
                     ones_lane if ref is dfv_ref else None)
        o += nh * HEAD_WIDTH


def _proj(x, win, qn, kvn, wuq, wuqs, wuk, wuv, cq, sq, ck, sk, rscale):
    b, s_len, _ = x.shape
    tm = TOKEN_BLOCK
    tab = lambda w: pl.BlockSpec((tm, w), lambda bb, i: (i, 0))
    heads = lambda nh, d: pl.BlockSpec((1, nh, tm, d), lambda bb, i: (bb, 0, i, 0))
    shape = lambda nh, d: jax.ShapeDtypeStruct((b, nh, s_len, d), BF16)
    outs = [(MLA_HEADS, MLA_QK), (MLA_HEADS, MLA_QK), (MLA_HEADS, LANES)] + [(SB_HEADS, SB_DIM)] * 3 + [
        (SW_Q_HEADS, SW_DIM), (SW_KV_HEADS, SW_DIM), (SW_KV_HEADS, SW_DIM)] + [
        (DF_HEADS, HEAD_WIDTH), (DF_HEADS, HEAD_WIDTH), (DF_HEADS, LANES)]
    consts = (win, qn, kvn, wuq, wuqs, wuk, wuv)
    return pl.pallas_call(
        _proj_kernel,
        grid=(b, s_len // tm),
        in_specs=[pl.BlockSpec((1, tm, D_MODEL), lambda bb, i: (bb, i, 0))]
                 + [_const_spec(c.shape) for c in consts]
                 + [tab(cq.shape[1]), tab(sq.shape[1]), tab(LANES), tab(LANES), _const_spec(rscale.shape)],
        out_specs=[heads(nh, d) for nh, d in outs],
        out_shape=[shape(nh, d) for nh, d in outs],
        compiler_params=_cparams(("arbitrary", "arbitrary")),
        name="proj",
    )(x, *consts, cq, sq, ck, sk, rscale)


def _causal_mask(n):
    row = lax.broadcasted_iota(jnp.int32, (n, n), 0)
    col = lax.broadcasted_iota(jnp.int32, (n, n), 1)
    return row, col


def _softmax_step(s, v_ones, m, acc):
    m_new = jnp.maximum(m, jnp.max(s, axis=-1, keepdims=True))
    p = jnp.exp((s - m_new).astype(BF16))
    alpha = jnp.exp(m - m_new)
    return m_new, alpha * acc + jnp.dot(p, v_ones, preferred_element_type=F32)


def _normalised(acc, width):
    return acc[:, :width] / acc[:, width:width + 1]


def _walk_full_blocks(n_blocks, blk, block, carry, use_quads=True):
    quads = n_blocks // 4 if use_quads else 0
    carry = lax.fori_loop(0, quads, lambda j, c: block(4 * j, 4 * blk, c), carry)
    done = 4 * quads
    pair = (n_blocks - done) // 2
    carry = lax.fori_loop(0, pair, lambda j, c: block(done + 2 * j, 2 * blk, c), carry)
    done = done + 2 * pair
    return lax.fori_loop(done, n_blocks, lambda ki, c: block(ki, blk, c), carry)


def _mla_kernel(q_ref, k_ref, v_ref, o_ref, *, blk):
    qi = pl.program_id(2)
    q = q_ref[0, 0]

    def kv_block(ki, width=blk):
        start = pl.multiple_of(ki * blk, blk)
        return k_ref[0, 0, pl.ds(start, width), :], v_ref[0, 0, pl.ds(start, width), :]

    def block(ki, width, carry):
        k, v = kv_block(ki, width)
        return _softmax_step(_nt_dot(q, k), v, *carry)

    init = (jnp.full((blk, 1), NEG_INF, F32), jnp.zeros((blk, LANES), F32))
    carry = _walk_full_blocks(qi, blk, block, init)
    k, v = kv_block(qi)
    row, col = _causal_mask(blk)
    s = jnp.where(col <= row, _nt_dot(q, k), NEG_INF)
    _, acc = _softmax_step(s, v, *carry)
    o_ref[0, 0] = _normalised(acc, MLA_V).astype(o_ref.dtype)


def _head_spec(blk, d):
    return pl.BlockSpec((1, 1, blk, d), lambda b, h, i: (b, h, i, 0))


def _full_spec(s_len, d):
    return pl.BlockSpec((1, 1, s_len, d), lambda b, h, i: (b, h, 0, 0))


def _mla_attention(q, k, v):
    b, nh, s_len, _ = q.shape
    blk = MLA_BLOCK
    return pl.pallas_call(
        functools.partial(_mla_kernel, blk=blk),
        grid=(b, nh, s_len // blk),
        in_specs=[_head_spec(blk, MLA_QK), _full_spec(s_len, MLA_QK), _full_spec(s_len, LANES)],
        out_specs=_head_spec(blk, MLA_V),
        out_shape=jax.ShapeDtypeStruct((b, nh, s_len, MLA_V), BF16),
        compiler_params=_cparams(("arbitrary",) * 3),
        name="mla",
    )(q, k, v)


def _log_sigmoid(z):
    return jnp.minimum(z, 0.0) - jnp.log(1.0 + jnp.exp(-jnp.abs(z)))


def _suffix_sum(l1m, upper):
    sub = upper.shape[0]
    n_sub = l1m.shape[1] // sub
    pieces = [None] * n_sub
    right = None
    for c in reversed(range(n_sub)):
        chunk = l1m[:, c * sub:(c + 1) * sub]
        hi = chunk.astype(BF16)
        lo = (chunk - hi.astype(F32)).astype(BF16)
        inner = jnp.dot(hi, upper, preferred_element_type=F32) + jnp.dot(lo, upper, preferred_element_type=F32)
        total = inner[:, 0:1] + chunk[:, 0:1]
        if right is None:
            pieces[c], right = inner, total
        else:
            pieces[c], right = inner + right, right + total
    return jnp.concatenate(pieces, axis=1), right


def _sb_kernel(q_ref, k_ref, v_ref, o_ref, *, blk):
    qi = pl.program_id(2)
    q = q_ref[0, 0]
    srow, scol = _causal_mask(SUFFIX_BLOCK)
    upper = (srow > scol).astype(BF16)

    def kv_rows(start, width):
        start = pl.multiple_of(start, width)
        return k_ref[0, 0, pl.ds(start, width), :], v_ref[0, 0, pl.ds(start, width), :]

    row, col = _causal_mask(blk)
    k, v = kv_rows(qi * blk, blk)
    z = _nt_dot(q, k)
    mask = col < row
    lb = _log_sigmoid(z)
    l1m = jnp.where(mask, lb - z, 0.0)
    suffix, passed = _suffix_sum(l1m, upper)
    a = jnp.where(mask, jnp.exp(lb + suffix), 0.0)
    acc = jnp.dot(a.astype(BF16), v, preferred_element_type=F32)

    n_walk = qi * (blk // SB_WALK)

    def live(carry):
        j, passed, _ = carry
        return (j < n_walk) & (jnp.max(passed) >= EXP_UNDERFLOW)

    def body(carry):
        j, passed, acc = carry
        k, v = kv_rows((n_walk - 1 - j) * SB_WALK, SB_WALK)
        z = _nt_dot(q, k)
        lb = _log_sigmoid(z)
        l1m = lb - z
        suffix, total = _suffix_sum(l1m, upper)
        a = jnp.exp(lb + suffix + passed)
        acc = acc + jnp.dot(a.astype(BF16), v, preferred_element_type=F32)
        return j + 1, passed + total, acc

    _, _, acc = lax.while_loop(live, body, (jnp.int32(0), passed, acc))
    o_ref[0, 0] = acc.astype(o_ref.dtype)


def _sb_attention(q, k, v):
    b, nh, s_len, d = q.shape
    blk = ATT_BLOCK
    return pl.pallas_call(
        functools.partial(_sb_kernel, blk=blk),
        grid=(b, nh, s_len // blk),
        in_specs=[_head_spec(blk, d), _full_spec(s_len, d), _full_spec(s_len, d)],
        out_specs=_head_spec(blk, d),
        out_shape=jax.ShapeDtypeStruct((b, nh, s_len, d), BF16),
        compiler_params=_cparams(("arbitrary",) * 3),
        name="sb",
    )(q, k, v)


def _swa_kernel(slopes_ref, sinks_ref, q_ref, kp_ref, kc_ref, vp_ref, vc_ref,
                pq_ref, pkp_ref, pkc_ref, o_ref):
    i = pl.program_id(1)
    w = SW_WINDOW
    pk = jnp.concatenate([pkp_ref[...], pkc_ref[...]], axis=1)
    dist = pq_ref[...] - pk
    qi = lax.broadcasted_iota(jnp.int32, (w, 2 * w), 0)
    kj = lax.broadcasted_iota(jnp.int32, (w, 2 * w), 1)
    mask = (kj > qi) & (kj <= qi + w) & ((kj >= w) | (i > 0))
    for g in range(SW_KV_HEADS):
        q = q_ref[0, g * SW_GROUP:(g + 1) * SW_GROUP].reshape(SW_GROUP * w, SW_DIM)
        k = jnp.concatenate([kp_ref[0, g], kc_ref[0, g]], axis=0)
        v = jnp.concatenate([vp_ref[0, g], vc_ref[0, g]], axis=0)
        s = _nt_dot(q, k)
        for j in range(SW_GROUP):
            head = g * SW_GROUP + j
            sj = jnp.where(mask, s[j * w:(j + 1) * w] - slopes_ref[head] * dist, NEG_INF)
            sink = sinks_ref[head]
            m = jnp.maximum(jnp.max(sj, axis=-1, keepdims=True), sink)
            p = jnp.exp(sj - m)
            denom = jnp.sum(p, axis=-1, keepdims=True) + jnp.exp(sink - m)
            o = jnp.dot(p.astype(BF16), v, preferred_element_type=F32) / denom
            o_ref[0, head] = o.astype(o_ref.dtype)


def _swa_attention(q, k, v, slopes, sinks, pos_col, pos_row):
    b, _, s_len, d = q.shape
    w = SW_WINDOW
    prev = lambda i: jnp.maximum(i - 1, 0)
    smem = pl.BlockSpec(memory_space=pltpu.SMEM)
    kv_prev = pl.BlockSpec((1, SW_KV_HEADS, w, d), lambda bb, i: (bb, 0, prev(i), 0))
    kv_cur = pl.BlockSpec((1, SW_KV_HEADS, w, d), lambda bb, i: (bb, 0, i, 0))
    return pl.pallas_call(
        _swa_kernel,
        grid=(b, s_len // w),
        in_specs=[smem, smem,
                  pl.BlockSpec((1, SW_Q_HEADS, w, d), lambda bb, i: (bb, 0, i, 0)),
                  kv_prev, kv_cur, kv_prev, kv_cur,
                  pl.BlockSpec((w, 1), lambda bb, i: (i, 0)),
                  pl.BlockSpec((1, w), lambda bb, i: (0, prev(i))),
                  pl.BlockSpec((1, w), lambda bb, i: (0, i))],
        out_specs=pl.BlockSpec((1, SW_Q_HEADS, w, d), lambda bb, i: (bb, 0, i, 0)),
        out_shape=jax.ShapeDtypeStruct((b, SW_Q_HEADS, s_len, d), BF16),
        compiler_params=_cparams(("arbitrary",) * 2),
        name="swa",
    )(slopes, sinks, q, k, k, v, v, pos_col, pos_row, pos_row)


def _diff_kernel(slopes_ref, q_ref, k_ref, v_ref, pk_ref, lq1_ref, lk1_ref, lq2_ref, lk2_ref,
                 subln_ref, o_ref, *, blk, lam_init):
    h = pl.program_id(1)
    qi = pl.program_id(2)
    q = q_ref[0, 0]
    lane = lax.broadcasted_iota(jnp.int32, q.shape, 1)
    zero = jnp.zeros_like(q)
    qq = jnp.concatenate([jnp.where(lane < DF_DIM, q, zero), jnp.where(lane >= DF_DIM, q, zero)], axis=0)
    slope = slopes_ref[h]
    base = pk_ref[:, pl.ds(pl.multiple_of(qi * blk, blk), LANES)][:, 0:1]

    def block(ki, width=blk):
        start = pl.multiple_of(ki * blk, blk)
        k = k_ref[0, 0, pl.ds(start, width), :]
        v = v_ref[0, 0, pl.ds(start, width), :]
        return _nt_dot(qq, k) + slope * (pk_ref[:, pl.ds(start, width)] - base), v

    def step(ki, width, carry):
        s, v = block(ki, width)
        return _softmax_step(s, v, *carry)

    init = (jnp.full((2 * blk, 1), NEG_INF, F32), jnp.zeros((2 * blk, LANES), F32))
    carry = _walk_full_blocks(qi, blk, step, init, use_quads=False)
    s, v = block(qi)
    row = lax.broadcasted_iota(jnp.int32, (2 * blk, blk), 0)
    col = lax.broadcasted_iota(jnp.int32, (2 * blk, blk), 1)
    causal = (col <= row) & ((row < blk) | (col <= row - blk))
    _, acc = _softmax_step(jnp.where(causal, s, NEG_INF), v, *carry)
    lam = (jnp.exp(jnp.sum(lq1_ref[...] * lk1_ref[...], keepdims=True))
           - jnp.exp(jnp.sum(lq2_ref[...] * lk2_ref[...], keepdims=True)) + lam_init)
    o = _normalised(acc, DF_VDIM)
    o = o[:blk] - lam * o[blk:]
    o = o * lax.rsqrt(jnp.mean(o * o, axis=-1, keepdims=True) + RMS_EPS) * subln_ref[...] * (1.0 - lam_init)
    o_ref[0, 0] = o.astype(o_ref.dtype)


def _diff_attention(q, k, v, slopes, pos_row, lq1, lk1, lq2, lk2, subln, lam_init):
    b, nh, s_len, d = q.shape
    blk = DIFF_BLOCK
    smem = pl.BlockSpec(memory_space=pltpu.SMEM)
    vec = lambda n: pl.BlockSpec((1, n), lambda bb, h, i: (0, 0))
    return pl.pallas_call(
        functools.partial(_diff_kernel, blk=blk, lam_init=lam_init),
        grid=(b, nh, s_len // blk),
        in_specs=[smem, _head_spec(blk, d), _full_spec(s_len, d), _full_spec(s_len, LANES),
                  pl.BlockSpec((1, s_len), lambda bb, h, i: (0, 0)),
                  vec(DF_DIM), vec(DF_DIM), vec(DF_DIM), vec(DF_DIM), vec(DF_VDIM)],
        out_specs=_head_spec(blk, DF_VDIM),
        out_shape=jax.ShapeDtypeStruct((b, nh, s_len, DF_VDIM), BF16),
        compiler_params=_cparams(("arbitrary",) * 3),
        name="diff",
    )(slopes, q, k, v, pos_row, lq1, lk1, lq2, lk2, subln)


def _layer_norm(v, g, b):
    mu = jnp.mean(v, axis=-1, keepdims=True)
    c = v - mu
    var = jnp.mean(c * c, axis=-1, keepdims=True)
    return c * lax.rsqrt(var + LN_EPS) * g + b


def _sigmoid(z):
    return 1.0 / (1.0 + jnp.exp(-z))


def _route(logits_t, bias_col):
    scores = _sigmoid(logits_t)
    biased = scores + bias_col
    ng = N_GROUPS
    sc = [scores[j * ng:(j + 1) * ng] for j in range(EXPERTS_PER_GROUP)]
    bs = [biased[j * ng:(j + 1) * ng] for j in range(EXPERTS_PER_GROUP)]
    gscore = None
    for a in range(EXPERTS_PER_GROUP):
        for c in range(a + 1, EXPERTS_PER_GROUP):
            pair = bs[a] + bs[c]
            gscore = pair if gscore is None else jnp.maximum(gscore, pair)
    gid = lax.broadcasted_iota(jnp.int32, gscore.shape, 0)
    best = jnp.max(gscore, axis=0, keepdims=True)
    gsel = jnp.min(jnp.where(gscore == best, gid, ng), axis=0, keepdims=True)
    onehot = gid == gsel
    pick = lambda a: jnp.sum(jnp.where(onehot, a, 0.0), axis=0, keepdims=True)
    b_in = [pick(a) for a in bs]
    s_in = [pick(a) for a in sc]
    chosen = []
    for j in range(EXPERTS_PER_GROUP):
        ahead = jnp.zeros_like(b_in[j])
        for c in range(EXPERTS_PER_GROUP):
            if c == j:
                continue
            beats = (b_in[c] > b_in[j]) | ((b_in[c] == b_in[j]) & (c < j))
            ahead = ahead + beats.astype(F32)
        chosen.append(jnp.where(ahead < 2.0, s_in[j], 0.0))
    total = chosen[0] + chosen[1] + chosen[2] + chosen[3]
    gates = [c / total for c in chosen]
    return gates, onehot


def _merge_kernel(x_ref, oa_ref, ob_ref, oc_ref, od_ref, wg_ref, bg_ref, wa_ref, wb_ref, wc_ref, wd_ref,
                  wo_ref, g_ref, b_ref, wr_ref, rb_ref, x1_ref, gate_ref, route_ref, count_sc, *, alpha):
    tm = x_ref.shape[1]
    x = x_ref[0]
    xb = x.astype(BF16)
    merged = None
    for n, (o_ref, w_ref) in enumerate(((oa_ref, wa_ref), (ob_ref, wb_ref), (oc_ref, wc_ref), (od_ref, wd_ref))):
        cols = slice(n * D_MODEL, (n + 1) * D_MODEL)
        gate = _sigmoid(jnp.dot(xb, wg_ref[:, cols], preferred_element_type=F32) + bg_ref[:, cols])
        heads = jnp.concatenate([o_ref[0, hd] for hd in range(o_ref.shape[1])], axis=1)
        term = gate * jnp.dot(heads, w_ref[...], preferred_element_type=F32)
        merged = term if merged is None else merged + term
    mix = jnp.dot(merged.astype(BF16), wo_ref[...], preferred_element_type=F32)
    y = _layer_norm(alpha * x + mix, g_ref[...], b_ref[...])
    x1_ref[0] = y

    logits_t = _nt_dot(wr_ref[...], y, precision=lax.Precision.HIGHEST)
    gates, onehot = _route(logits_t, rb_ref[...])

    @pl.when(pl.program_id(1) % (MOE_ROWS // tm) == 0)
    def _():
        count_sc[...] = jnp.zeros_like(count_sc)

    onef = onehot.astype(F32)
    before = (lax.broadcasted_iota(jnp.int32, (tm, tm), 0) < lax.broadcasted_iota(jnp.int32, (tm, tm), 1))
    prefix = jnp.dot(onehot.astype(BF16), before.astype(BF16), preferred_element_type=F32)
    seen = count_sc[...]
    rank = jnp.sum(onef * (prefix + seen[:, 0:1]), axis=0, keepdims=True)
    count_sc[...] = seen + jnp.sum(onef, axis=1, keepdims=True)
    group = jnp.sum(onef * lax.broadcasted_iota(jnp.int32, onef.shape, 0).astype(F32), axis=0, keepdims=True)

    rows = [jnp.where(onehot, gate, 0.0) for gate in gates] + [group, rank]
    rows.append(jnp.zeros((LANES - N_EXPERTS - 2, tm), F32))
    gate_ref[0] = jnp.concatenate(rows, axis=0).T
    route_ref[0] = jnp.concatenate([group, rank, jnp.zeros((6, tm), F32)], axis=0)


def _merge(x, oa, ob, oc, od, wg, bg, wa, wb, wc, wd, wo, g, b, wr_t, rb_col, alpha):
    bsz, s_len, _ = x.shape
    tm = TOKEN_BLOCK
    row = lambda w: pl.BlockSpec((1, tm, w), lambda bb, i: (bb, i, 0))
    heads = lambda o: pl.BlockSpec((1, o.shape[1], tm, o.shape[3]), lambda bb, i: (bb, 0, i, 0))
    consts = (wg, bg, wa, wb, wc, wd, wo, g, b, wr_t, rb_col)
    return pl.pallas_call(
        functools.partial(_merge_kernel, alpha=alpha),
        grid=(bsz, s_len // tm),
        in_specs=[row(D_MODEL), heads(oa), heads(ob), heads(oc), heads(od)]
                 + [_const_spec(c.shape) for c in consts],
        out_specs=[row(D_MODEL), row(LANES),
                   pl.BlockSpec((1, 8, tm), lambda bb, i: (bb * (s_len // tm) + i, 0, 0))],
        out_shape=[jax.ShapeDtypeStruct((bsz, s_len, D_MODEL), F32),
                   jax.ShapeDtypeStruct((bsz, s_len, LANES), F32),
                   jax.ShapeDtypeStruct((bsz * s_len // tm, 8, tm), F32)],
        scratch_shapes=[pltpu.VMEM((N_GROUPS, LANES), F32)],
        compiler_params=_cparams(("arbitrary", "arbitrary")),
        name="merge",
    )(x, oa, ob, oc, od, *consts)


def _two_bf16(a):
    hi = a.astype(BF16)
    return hi, (a - hi.astype(F32)).astype(BF16)


def _moe_kernel(x_ref, gate_ref, route_ref, w1_ref, w3_ref, w2_ref, g_ref, b_ref, o_ref,
                acc_ref, xb_ref, ghi_ref, glo_ref, *, alpha):
    grp = pl.program_id(1)

    @pl.when(grp == 0)
    def _():
        acc_ref[...] = jnp.zeros_like(acc_ref)
        xb_ref[...] = x_ref[...].astype(BF16)
        ghi_ref[...], glo_ref[...] = _two_bf16(gate_ref[...])

    tm = x_ref.shape[0]
    n_sub = route_ref.shape[0]
    sub = tm // n_sub
    cap = MOE_CAP
    grp_f = grp.astype(F32)
    in_group_col = gate_ref[:, N_EXPERTS:N_EXPERTS + 1] == grp_f
    rank_col = gate_ref[:, N_EXPERTS + 1:N_EXPERTS + 2]
    in_group_row = [route_ref[j, 0:1, :] == grp_f for j in range(n_sub)]
    rank_row = [route_ref[j, 1:2, :] for j in range(n_sub)]
    count = sum(jnp.sum(m.astype(F32)) for m in in_group_row)
    n_chunks = jnp.ceil(count / cap).astype(jnp.int32)
    w2 = w2_ref[...].reshape(EXPERTS_PER_GROUP * D_EXPERT, D_MODEL)
    lane = lax.broadcasted_iota(jnp.int32, (cap, LANES), 1)

    def chunk(c, carry):
        first = c * cap
        slot_col = (lax.broadcasted_iota(jnp.int32, (cap, 1), 0) + first).astype(F32)
        slot_row = (lax.broadcasted_iota(jnp.int32, (1, cap), 1) + first).astype(F32)
        xg = None
        gg = None
        for j in range(n_sub):
            sel = ((rank_row[j] == slot_col) & in_group_row[j]).astype(BF16)
            tok = slice(j * sub, (j + 1) * sub)
            part = jnp.dot(sel, xb_ref[tok, :], preferred_element_type=F32)
            gpart = (jnp.dot(sel, ghi_ref[tok, :], preferred_element_type=F32)
                     + jnp.dot(sel, glo_ref[tok, :], preferred_element_type=F32))
            xg = part if xg is None else xg + part
            gg = gpart if gg is None else gg + gpart
        xg = xg.astype(BF16)
        hid = []
        for e in range(EXPERTS_PER_GROUP):
            h1 = jnp.dot(xg, w1_ref[e], preferred_element_type=F32)
            h3 = jnp.dot(xg, w3_ref[e], preferred_element_type=F32)
            gate = jnp.sum(jnp.where(lane == e * N_GROUPS + grp, gg, 0.0), axis=1, keepdims=True)
            hid.append((h1 * _sigmoid(h1) * h3 * gate).astype(BF16))
        y = jnp.dot(jnp.concatenate(hid, axis=1), w2, preferred_element_type=F32)
        y_hi, y_lo = _two_bf16(y)
        back = ((rank_col == slot_row) & in_group_col).astype(BF16)
        acc_ref[...] += (jnp.dot(back, y_hi, preferred_element_type=F32)
                         + jnp.dot(back, y_lo, preferred_element_type=F32))
        return carry

    lax.fori_loop(0, n_chunks, chunk, 0)

    @pl.when(grp == N_GROUPS - 1)
    def _():
        o_ref[...] = _layer_norm(alpha * x_ref[...] + acc_ref[...], g_ref[...], b_ref[...])


def _moe(x1, gates, route, w1, w3, w2, g, b, alpha):
    t = x1.shape[0]
    tm = MOE_ROWS
    epg = EXPERTS_PER_GROUP
    n_sub = tm // route.shape[2]
    return pl.pallas_call(
        functools.partial(_moe_kernel, alpha=alpha),
        grid=(t // tm, N_GROUPS),
        in_specs=[pl.BlockSpec((tm, D_MODEL), lambda i, e: (i, 0)),
                  pl.BlockSpec((tm, LANES), lambda i, e: (i, 0)),
                  pl.BlockSpec((n_sub,) + route.shape[1:], lambda i, e: (i, 0, 0)),
                  pl.BlockSpec((epg, D_MODEL, D_EXPERT), lambda i, e: (e, 0, 0)),
                  pl.BlockSpec((epg, D_MODEL, D_EXPERT), lambda i, e: (e, 0, 0)),
                  pl.BlockSpec((epg, D_EXPERT, D_MODEL), lambda i, e: (e, 0, 0)),
                  pl.BlockSpec((1, D_MODEL), lambda i, e: (0, 0)),
                  pl.BlockSpec((1, D_MODEL), lambda i, e: (0, 0))],
        out_specs=pl.BlockSpec((tm, D_MODEL), lambda i, e: (i, 0)),
        out_shape=jax.ShapeDtypeStruct((t, D_MODEL), F32),
        scratch_shapes=[pltpu.VMEM((tm, D_MODEL), F32), pltpu.VMEM((tm, D_MODEL), BF16),
                        pltpu.VMEM((tm, LANES), BF16), pltpu.VMEM((tm, LANES), BF16)],
        compiler_params=_cparams(("arbitrary", "arbitrary")),
        name="moe",
    )(x1, gates, route, w1, w3, w2, g, b)


def _alibi_slopes(n_heads):
    return 2.0 ** (-8.0 * jnp.arange(1, n_heads + 1, dtype=F32) / n_heads)


def _rope_tables(positions):
    half = MLA_ROPE // 2
    inv_freq = ROPE_THETA ** (-jnp.arange(half, dtype=F32) / half)
    ang = positions.astype(F32)[:, None] * inv_freq[None, :]
    cos = jnp.cos(ang)
    sin = jnp.sin(ang)
    return jnp.concatenate([cos, cos], axis=1), jnp.concatenate([sin, sin], axis=1)


def _rotate_half_columns(w):
    half = w.shape[-1] // 2
    return jnp.concatenate([-w[..., half:], w[..., :half]], axis=-1)


def _prep_w_in(w_in):
    split_at = [int(c) for c in np.cumsum(IN_SPLITS)[:-1]]
    parts = jnp.split(w_in, split_at, axis=-1)
    c_q, c_kv, k_rope = parts[0], parts[1], parts[2]
    before = jnp.zeros((w_in.shape[0], MLA_NOPE), w_in.dtype)
    after = jnp.zeros((w_in.shape[0], LANES - MLA_QK), w_in.dtype)
    return jnp.concatenate([c_q, c_kv, before, k_rope, after, before, _rotate_half_columns(k_rope), after]
                           + list(parts[3:]), axis=-1).astype(BF16)


def _pad_heads(w, n_heads, width):
    w3 = w.reshape(w.shape[0], n_heads, width)
    return jnp.pad(w3, ((0, 0), (0, 0), (0, LANES - width))).reshape(w.shape[0], n_heads * LANES)


def _prep_w_uq(w_uq):
    w = w_uq.reshape(MLA_Q_RANK, MLA_HEADS, MLA_QK)
    rot = jnp.concatenate([jnp.zeros_like(w[..., :MLA_NOPE]), _rotate_half_columns(w[..., MLA_NOPE:])], axis=-1)
    return (_pad_heads(w_uq, MLA_HEADS, MLA_QK).astype(BF16),
            _pad_heads(rot.reshape(w_uq.shape), MLA_HEADS, MLA_QK).astype(BF16))


def _prep_w_ukv(w_ukv):
    w = w_ukv.reshape(MLA_KV_RANK, MLA_HEADS, MLA_NOPE + MLA_V)
    wk = w[..., :MLA_NOPE].reshape(MLA_KV_RANK, MLA_HEADS * MLA_NOPE)
    wv = w[..., MLA_NOPE:].reshape(MLA_KV_RANK, MLA_HEADS * MLA_V)
    return _pad_heads(wk, MLA_HEADS, MLA_NOPE).astype(BF16), _pad_heads(wv, MLA_HEADS, MLA_V).astype(BF16)


def kernel(x, positions, w_in, mla_q_norm, mla_kv_norm, w_uq, w_ukv, sw_sinks, df_lq1, df_lk1, df_lq2, df_lk2,
           df_subln, w_gate, b_gate, w_br_a, w_br_b, w_br_c, w_br_d, w_out, ln1_g, ln1_b, w_router, router_bias,
           moe_w1, moe_w3, moe_w2, ln2_g, ln2_b):
    b, s_len, d = x.shape
    depth = w_in.shape[0]
    t = b * s_len
    alpha = (2 * depth) ** 0.25
    assert d == D_MODEL and s_len % TOKEN_BLOCK == 0 and s_len % ATT_BLOCK == 0 and s_len % MOE_ROWS == 0
    assert MOE_ROWS % TOKEN_BLOCK == 0 and s_len % MLA_BLOCK == 0 and s_len % DIFF_BLOCK == 0
    assert SB_DIM == SW_DIM == DF_VDIM == 2 * DF_DIM == HEAD_WIDTH == LANES // 2
    assert ATT_BLOCK % SW_WINDOW == 0

    posf = positions.astype(F32)
    pos_col = posf[:, None]
    pos_row = posf[None, :]
    cos, sin = _rope_tables(positions)
    q_scale = MLA_QK ** -0.5
    ones_n = jnp.ones((s_len, MLA_NOPE), F32)
    tail = jnp.zeros((s_len, LANES - MLA_QK), F32)
    cq = jnp.tile(jnp.concatenate([ones_n * q_scale, cos * q_scale, tail], axis=1), (1, MLA_HEADS))
    sq = jnp.tile(jnp.concatenate([0.0 * ones_n, sin * q_scale, tail], axis=1), (1, MLA_HEADS))
    ck = jnp.concatenate([0.0 * ones_n, cos, tail], axis=1)
    sk = jnp.concatenate([0.0 * ones_n, sin, tail], axis=1)
    rscale = jnp.concatenate([
        jnp.full((B_WIDTH,), SB_DIM ** -0.5, F32), jnp.ones((2 * B_WIDTH,), F32),
        jnp.full((C_WIDTH,), SW_DIM ** -0.5, F32), jnp.ones((2 * SW_KV_HEADS * SW_DIM,), F32),
        jnp.full((2 * DF_HEADS * DF_DIM,), DF_DIM ** -0.5, F32), jnp.ones((2 * DF_HEADS * DF_DIM + D_WIDTH,), F32),
    ])[None, :]
    sw_slopes = _alibi_slopes(SW_Q_HEADS)
    df_slopes = _alibi_slopes(DF_HEADS)

    perm = np.arange(N_EXPERTS).reshape(N_GROUPS, EXPERTS_PER_GROUP).T.reshape(-1)
    wr_t = w_router.astype(F32).T[perm]
    rb_col = router_bias.astype(F32)[perm][:, None]

    x3 = x
    for l in range(depth):
        lam_init = 0.8 - 0.6 * math.exp(-0.3 * l)
        wuq, wuqs = _prep_w_uq(w_uq[l])
        wuk, wuv = _prep_w_ukv(w_ukv[l])
        (mla_q, mla_k, mla_v, sb_q, sb_k, sb_v, sw_q, sw_k, sw_v, df_q, df_k, df_v) = _proj(
            x3, _prep_w_in(w_in[l]), mla_q_norm[l][None, :], mla_kv_norm[l][None, :],
            wuq, wuqs, wuk, wuv, cq, sq, ck, sk, rscale)

        o_a = _mla_attention(mla_q, mla_k, mla_v)
        o_b = _sb_attention(sb_q, sb_k, sb_v)
        o_c = _swa_attention(sw_q, sw_k, sw_v, sw_slopes, sw_sinks[l].astype(F32), pos_col, pos_row)
        o_d = _diff_attention(df_q, df_k, df_v, df_slopes, pos_row,
                              df_lq1[l][None, :], df_lk1[l][None, :], df_lq2[l][None, :],
                              df_lk2[l][None, :], df_subln[l][None, :], lam_init)

        x1, gates, route = _merge(x3, o_a, o_b, o_c, o_d, w_gate[l].astype(BF16), b_gate[l][None, :],
                           w_br_a[l].astype(BF16), w_br_b[l].astype(BF16), w_br_c[l].astype(BF16),
                           w_br_d[l].astype(BF16), w_out[l].astype(BF16), ln1_g[l][None, :], ln1_b[l][None, :],
                           wr_t, rb_col, alpha)
        x2 = _moe(x1.reshape(t, d), gates.reshape(t, LANES), route, moe_w1[l].astype(BF16), moe_w3[l].astype(BF16),
                  moe_w2[l].astype(BF16), ln2_g[l][None, :], ln2_b[l][None, :], alpha)
        x3 = x2.reshape(b, s_len, d)
    return x3
```

```python
import functools
import math

import numpy as np
import jax
import jax.numpy as jnp
from jax import lax
from jax.experimental import pallas as pl
from jax.experimental.pallas import tpu as pltpu

F32 = jnp.float32
BF16 = jnp.bfloat16

D_MODEL = 1024
NEG_INF = -1e30
LN_EPS = 1e-5
RMS_EPS = 1e-6

MLA_HEADS = 4
MLA_Q_RANK = 256
MLA_KV_RANK = 128
MLA_NOPE = 64
MLA_ROPE = 32
MLA_V = 64
MLA_QK = MLA_NOPE + MLA_ROPE
ROPE_THETA = 10000.0

SB_HEADS = 4
SB_DIM = 64

SW_Q_HEADS = 8
SW_KV_HEADS = 2
SW_DIM = 64
SW_WINDOW = 128
SW_GROUP = SW_Q_HEADS // SW_KV_HEADS

DF_HEADS = 4
DF_DIM = 32
DF_VDIM = 2 * DF_DIM

A_WIDTH = MLA_HEADS * MLA_V
B_WIDTH = SB_HEADS * SB_DIM
C_WIDTH = SW_Q_HEADS * SW_DIM
D_WIDTH = DF_HEADS * DF_VDIM
N_BRANCH = 4

IN_SPLITS = (MLA_Q_RANK, MLA_KV_RANK, MLA_ROPE, 3 * B_WIDTH, C_WIDTH, 2 * SW_KV_HEADS * SW_DIM,
             2 * DF_HEADS * DF_DIM, 2 * DF_HEADS * DF_DIM, D_WIDTH)

N_EXPERTS = 32
N_GROUPS = 8
EXPERTS_PER_GROUP = N_EXPERTS // N_GROUPS
D_EXPERT = 256

LANES = 128
HEAD_WIDTH = 64

OFF_CQ = 0
OFF_CKV = OFF_CQ + MLA_Q_RANK
OFF_KR = OFF_CKV + MLA_KV_RANK
OFF_KRS = OFF_KR + LANES
OFF_REST = OFF_KRS + LANES
REST_WIDTH = sum(IN_SPLITS[3:])
IN_PAD_WIDTH = OFF_REST + REST_WIDTH

TOKEN_BLOCK = 512
ATT_BLOCK = 512
MLA_BLOCK = 1024
DIFF_BLOCK = 1024
SUFFIX_BLOCK = 256
SB_WALK = 256
EXP_UNDERFLOW = -104.0
MOE_ROWS = 1024
MOE_CAP = 160
VMEM_LIMIT = 56 * 1024 * 1024


def _cparams(sem):
    return pltpu.CompilerParams(dimension_semantics=sem, vmem_limit_bytes=VMEM_LIMIT)


def _nt_dot(a, b, **kw):
    return lax.dot_general(a, b, (((1,), (1,)), ((), ())), preferred_element_type=F32, **kw)


def _const_spec(shape):
    nd = len(shape)
    return pl.BlockSpec(shape, lambda *_: (0,) * nd)


def _split_heads(val, out_ref, n_heads, width, ones_lane=None):
    for h in range(n_heads):
        tile = val[:, (h // 2) * LANES:(h // 2 + 1) * LANES]
        if h % 2:
            tile = pltpu.roll(tile, LANES - width, axis=1)
        if ones_lane is None:
            out_ref[0, h] = tile[:, :width].astype(out_ref.dtype)
        else:
            keep = lax.broadcasted_iota(jnp.int32, tile.shape, 1) < width
            out_ref[0, h] = (jnp.where(keep, tile, 0.0) + ones_lane).astype(out_ref.dtype)


def _proj_kernel(x_ref, win_ref, qn_ref, kvn_ref, wuq_ref, wuqs_ref, wuk_ref, wuv_ref,
                 cq_ref, sq_ref, ck_ref, sk_ref, rscale_ref,
                 mq_ref, mk_ref, mv_ref, sbq_ref, sbk_ref, sbv_ref, swq_ref, swk_ref, swv_ref,
                 dfq_ref, dfk_ref, dfv_ref):
    xb = x_ref[0].astype(BF16)
    h = jnp.dot(xb, win_ref[...], preferred_element_type=F32)
    c_q = h[:, OFF_CQ:OFF_CQ + MLA_Q_RANK]
    c_kv = h[:, OFF_CKV:OFF_CKV + MLA_KV_RANK]
    nq = c_q * lax.rsqrt(jnp.mean(c_q * c_q, axis=-1, keepdims=True) + RMS_EPS) * qn_ref[...]
    nqb = nq.astype(BF16)
    q = jnp.dot(nqb, wuq_ref[...], preferred_element_type=F32)
    qs = jnp.dot(nqb, wuqs_ref[...], preferred_element_type=F32)
    q = q * cq_ref[...] + qs * sq_ref[...]
    nkv = c_kv * lax.rsqrt(jnp.mean(c_kv * c_kv, axis=-1, keepdims=True) + RMS_EPS) * kvn_ref[...]
    nkvb = nkv.astype(BF16)
    kn = jnp.dot(nkvb, wuk_ref[...], preferred_element_type=F32)
    vv = jnp.dot(nkvb, wuv_ref[...], preferred_element_type=F32)
    kr = h[:, OFF_KR:OFF_KR + LANES] * ck_ref[...] + h[:, OFF_KRS:OFF_KRS + LANES] * sk_ref[...]
    ones_lane = (lax.broadcasted_iota(jnp.int32, (1, LANES), 1) == HEAD_WIDTH).astype(F32)
    for hd in range(MLA_HEADS):
        lanes = slice(hd * LANES, (hd + 1) * LANES)
        mq_ref[0, hd] = q[:, lanes][:, :MLA_QK].astype(mq_ref.dtype)
        mk_ref[0, hd] = (kn[:, lanes] + kr)[:, :MLA_QK].astype(mk_ref.dtype)
        mv_ref[0, hd] = (vv[:, lanes] + ones_lane).astype(mv_ref.dtype)
    rest = h[:, OFF_REST:] * rscale_ref[...]
    o = 0
    for ref, nh in ((sbq_ref, SB_HEADS), (sbk_ref, SB_HEADS), (sbv_ref, SB_HEADS),
                    (swq_ref, SW_Q_HEADS), (swk_ref, SW_KV_HEADS), (swv_ref, SW_KV_HEADS),
                    (dfq_ref, DF_HEADS), (dfk_ref, DF_HEADS), (dfv_ref, DF_HEADS)):
        _split_heads(rest[:, o:o + nh * HEAD_WIDTH], ref, nh, HEAD_WIDTH,
                     ones_lane if ref is dfv_ref else None)
        o += nh * HEAD_WIDTH


def _proj(x, win, qn, kvn, wuq, wuqs, wuk, wuv, cq, sq, ck, sk, rscale):
    b, s_len, _ = x.shape
    tm = TOKEN_BLOCK
    tab = lambda w: pl.BlockSpec((tm, w), lambda bb, i: (i, 0))
    heads = lambda nh, d: pl.BlockSpec((1, nh, tm, d), lambda bb, i: (bb, 0, i, 0))
    shape = lambda nh, d: jax.ShapeDtypeStruct((b, nh, s_len, d), BF16)
    outs = [(MLA_HEADS, MLA_QK), (MLA_HEADS, MLA_QK), (MLA_HEADS, LANES)] + [(SB_HEADS, SB_DIM)] * 3 + [
        (SW_Q_HEADS, SW_DIM), (SW_KV_HEADS, SW_DIM), (SW_KV_HEADS, SW_DIM)] + [
        (DF_HEADS, HEAD_WIDTH), (DF_HEADS, HEAD_WIDTH), (DF_HEADS, LANES)]
    consts = (win, qn, kvn, wuq, wuqs, wuk, wuv)
    return pl.pallas_call(
        _proj_kernel,
        grid=(b, s_len // tm),
        in_specs=[pl.BlockSpec((1, tm, D_MODEL), lambda bb, i: (bb, i, 0))]
                 + [_const_spec(c.shape) for c in consts]
                 + [tab(cq.shape[1]), tab(sq.shape[1]), tab(LANES), tab(LANES), _const_spec(rscale.shape)],
        out_specs=[heads(nh, d) for nh, d in outs],
        out_shape=[shape(nh, d) for nh, d in outs],
        compiler_params=_cparams(("arbitrary", "arbitrary")),
        name="proj",
    )(x, *consts, cq, sq, ck, sk, rscale)


def _causal_mask(n):
    row = lax.broadcasted_iota(jnp.int32, (n, n), 0)
    col = lax.broadcasted_iota(jnp.int32, (n, n), 1)
    return row, col


def _softmax_step(s, v_ones, m, acc):
    m_new = jnp.maximum(m, jnp.max(s, axis=-1, keepdims=True))
    p = jnp.exp((s - m_new).astype(BF16))
    alpha = jnp.exp(m - m_new)
    return m_new, alpha * acc + jnp.dot(p, v_ones, preferred_element_type=F32)


def _normalised(acc, width):
    return acc[:, :width] / acc[:, width:width + 1]


def _walk_full_blocks(n_blocks, blk, block, carry, use_quads=True):
    quads = n_blocks // 4 if use_quads else 0
    carry = lax.fori_loop(0, quads, lambda j, c: block(4 * j, 4 * blk, c), carry)
    done = 4 * quads
    pair = (n_blocks - done) // 2
    carry = lax.fori_loop(0, pair, lambda j, c: block(done + 2 * j, 2 * blk, c), carry)
    done = done + 2 * pair
    return lax.fori_loop(done, n_blocks, lambda ki, c: block(ki, blk, c), carry)


def _mla_kernel(q_ref, k_ref, v_ref, o_ref, *, blk):
    qi = pl.program_id(2)
    q = q_ref[0, 0]

    def kv_block(ki, width=blk):
        start = pl.multiple_of(ki * blk, blk)
        return k_ref[0, 0, pl.ds(start, width), :], v_ref[0, 0, pl.ds(start, width), :]

    def block(ki, width, carry):
        k, v = kv_block(ki, width)
        return _softmax_step(_nt_dot(q, k), v, *carry)

    init = (jnp.full((blk, 1), NEG_INF, F32), jnp.zeros((blk, LANES), F32))
    carry = _walk_full_blocks(qi, blk, block, init)
    k, v = kv_block(qi)
    row, col = _causal_mask(blk)
    s = jnp.where(col <= row, _nt_dot(q, k), NEG_INF)
    _, acc = _softmax_step(s, v, *carry)
    o_ref[0, 0] = _normalised(acc, MLA_V).astype(o_ref.dtype)


def _head_spec(blk, d):
    return pl.BlockSpec((1, 1, blk, d), lambda b, h, i: (b, h, i, 0))


def _full_spec(s_len, d):
    return pl.BlockSpec((1, 1, s_len, d), lambda b, h, i: (b, h, 0, 0))


def _mla_attention(q, k, v):
    b, nh, s_len, _ = q.shape
    blk = MLA_BLOCK
    return pl.pallas_call(
        functools.partial(_mla_kernel, blk=blk),
        grid=(b, nh, s_len // blk),
        in_specs=[_head_spec(blk, MLA_QK), _full_spec(s_len, MLA_QK), _full_spec(s_len, LANES)],
        out_specs=_head_spec(blk, MLA_V),
        out_shape=jax.ShapeDtypeStruct((b, nh, s_len, MLA_V), BF16),
        compiler_params=_cparams(("arbitrary",) * 3),
        name="mla",
    )(q, k, v)


def _log_sigmoid(z):
    return jnp.minimum(z, 0.0) - jnp.log(1.0 + jnp.exp(-jnp.abs(z)))


def _suffix_sum(l1m, upper):
    sub = upper.shape[0]
    n_sub = l1m.shape[1] // sub
    pieces = [None] * n_sub
    right = None
    for c in reversed(range(n_sub)):
        chunk = l1m[:, c * sub:(c + 1) * sub]
        hi = chunk.astype(BF16)
        lo = (chunk - hi.astype(F32)).astype(BF16)
        inner = jnp.dot(hi, upper, preferred_element_type=F32) + jnp.dot(lo, upper, preferred_element_type=F32)
        total = inner[:, 0:1] + chunk[:, 0:1]
        if right is None:
            pieces[c], right = inner, total
        else:
            pieces[c], right = inner + right, right + total
    return jnp.concatenate(pieces, axis=1), right


def _sb_kernel(q_ref, k_ref, v_ref, o_ref, *, blk):
    qi = pl.program_id(2)
    q = q_ref[0, 0]
    srow, scol = _causal_mask(SUFFIX_BLOCK)
    upper = (srow > scol).astype(BF16)

    def kv_rows(start, width):
        start = pl.multiple_of(start, width)
        return k_ref[0, 0, pl.ds(start, width), :], v_ref[0, 0, pl.ds(start, width), :]

    row, col = _causal_mask(blk)
    k, v = kv_rows(qi * blk, blk)
    z = _nt_dot(q, k)
    mask = col < row
    lb = _log_sigmoid(z)
    l1m = jnp.where(mask, lb - z, 0.0)
    suffix, passed = _suffix_sum(l1m, upper)
    a = jnp.where(mask, jnp.exp(lb + suffix), 0.0)
    acc = jnp.dot(a.astype(BF16), v, preferred_element_type=F32)

    n_walk = qi * (blk // SB_WALK)

    def live(carry):
        j, passed, _ = carry
        return (j < n_walk) & (jnp.max(passed) >= EXP_UNDERFLOW)

    def body(carry):
        j, passed, acc = carry
        k, v = kv_rows((n_walk - 1 - j) * SB_WALK, SB_WALK)
        z = _nt_dot(q, k)
        lb = _log_sigmoid(z)
        l1m = lb - z
        suffix, total = _suffix_sum(l1m, upper)
        a = jnp.exp(lb + suffix + passed)
        acc = acc + jnp.dot(a.astype(BF16), v, preferred_element_type=F32)
        return j + 1, passed + total, acc

    _, _, acc = lax.while_loop(live, body, (jnp.int32(0), passed, acc))
    o_ref[0, 0] = acc.astype(o_ref.dtype)


def _sb_attention(q, k, v):
    b, nh, s_len, d = q.shape
    blk = ATT_BLOCK
    return pl.pallas_call(
        functools.partial(_sb_kernel, blk=blk),
        grid=(b, nh, s_len // blk),
        in_specs=[_head_spec(blk, d), _full_spec(s_len, d), _full_spec(s_len, d)],
        out_specs=_head_spec(blk, d),
        out_shape=jax.ShapeDtypeStruct((b, nh, s_len, d), BF16),
        compiler_params=_cparams(("arbitrary",) * 3),
        name="sb",
    )(q, k, v)


def _swa_kernel(slopes_ref, sinks_ref, q_ref, kp_ref, kc_ref, vp_ref, vc_ref,
                pq_ref, pkp_ref, pkc_ref, o_ref):
    i = pl.program_id(1)
    w = SW_WINDOW
    pk = jnp.concatenate([pkp_ref[...], pkc_ref[...]], axis=1)
    dist = pq_ref[...] - pk
    qi = lax.broadcasted_iota(jnp.int32, (w, 2 * w), 0)
    kj = lax.broadcasted_iota(jnp.int32, (w, 2 * w), 1)
    mask = (kj > qi) & (kj <= qi + w) & ((kj >= w) | (i > 0))
    for g in range(SW_KV_HEADS):
        q = q_ref[0, g * SW_GROUP:(g + 1) * SW_GROUP].reshape(SW_GROUP * w, SW_DIM)
        k = jnp.concatenate([kp_ref[0, g], kc_ref[0, g]], axis=0)
        v = jnp.concatenate([vp_ref[0, g], vc_ref[0, g]], axis=0)
        s = _nt_dot(q, k)
        for j in range(SW_GROUP):
            head = g * SW_GROUP + j
            sj = jnp.where(mask, s[j * w:(j + 1) * w] - slopes_ref[head] * dist, NEG_INF)
            sink = sinks_ref[head]
            m = jnp.maximum(jnp.max(sj, axis=-1, keepdims=True), sink)
            p = jnp.exp(sj - m)
            denom = jnp.sum(p, axis=-1, keepdims=True) + jnp.exp(sink - m)
            o = jnp.dot(p.astype(BF16), v, preferred_element_type=F32) / denom
            o_ref[0, head] = o.astype(o_ref.dtype)


def _swa_attention(q, k, v, slopes, sinks, pos_col, pos_row):
    b, _, s_len, d = q.shape
    w = SW_WINDOW
    prev = lambda i: jnp.maximum(i - 1, 0)
    smem = pl.BlockSpec(memory_space=pltpu.SMEM)
    kv_prev = pl.BlockSpec((1, SW_KV_HEADS, w, d), lambda bb, i: (bb, 0, prev(i), 0))
    kv_cur = pl.BlockSpec((1, SW_KV_HEADS, w, d), lambda bb, i: (bb, 0, i, 0))
    return pl.pallas_call(
        _swa_kernel,
        grid=(b, s_len // w),
        in_specs=[smem, smem,
                  pl.BlockSpec((1, SW_Q_HEADS, w, d), lambda bb, i: (bb, 0, i, 0)),
                  kv_prev, kv_cur, kv_prev, kv_cur,
                  pl.BlockSpec((w, 1), lambda bb, i: (i, 0)),
                  pl.BlockSpec((1, w), lambda bb, i: (0, prev(i))),
                  pl.BlockSpec((1, w), lambda bb, i: (0, i))],
        out_specs=pl.BlockSpec((1, SW_Q_HEADS, w, d), lambda bb, i: (bb, 0, i, 0)),
        out_shape=jax.ShapeDtypeStruct((b, SW_Q_HEADS, s_len, d), BF16),
        compiler_params=_cparams(("arbitrary",) * 2),
        name="swa",
    )(slopes, sinks, q, k, k, v, v, pos_col, pos_row, pos_row)


def _diff_kernel(slopes_ref, q_ref, k_ref, v_ref, pk_ref, lq1_ref, lk1_ref, lq2_ref, lk2_ref,
                 subln_ref, o_ref, *, blk, lam_init):
    h = pl.program_id(1)
    qi = pl.program_id(2)
    q = q_ref[0, 0]
    lane = lax.broadcasted_iota(jnp.int32, q.shape, 1)
    zero = jnp.zeros_like(q)
    qq = jnp.concatenate([jnp.where(lane < DF_DIM, q, zero), jnp.where(lane >= DF_DIM, q, zero)], axis=0)
    slope = slopes_ref[h]
    base = pk_ref[:, pl.ds(pl.multiple_of(qi * blk, blk), LANES)][:, 0:1]

    def block(ki, width=blk):
        start = pl.multiple_of(ki * blk, blk)
        k = k_ref[0, 0, pl.ds(start, width), :]
        v = v_ref[0, 0, pl.ds(start, width), :]
        return _nt_dot(qq, k) + slope * (pk_ref[:, pl.ds(start, width)] - base), v

    def step(ki, width, carry):
        s, v = block(ki, width)
        return _softmax_step(s, v, *carry)

    init = (jnp.full((2 * blk, 1), NEG_INF, F32), jnp.zeros((2 * blk, LANES), F32))
    carry = _walk_full_blocks(qi, blk, step, init, use_quads=False)
    s, v = block(qi)
    row = lax.broadcasted_iota(jnp.int32, (2 * blk, blk), 0)
    col = lax.broadcasted_iota(jnp.int32, (2 * blk, blk), 1)
    causal = (col <= row) & ((row < blk) | (col <= row - blk))
    _, acc = _softmax_step(jnp.where(causal, s, NEG_INF), v, *carry)
    lam = (jnp.exp(jnp.sum(lq1_ref[...] * lk1_ref[...], keepdims=True))
           - jnp.exp(jnp.sum(lq2_ref[...] * lk2_ref[...], keepdims=True)) + lam_init)
    o = _normalised(acc, DF_VDIM)
    o = o[:blk] - lam * o[blk:]
    o = o * lax.rsqrt(jnp.mean(o * o, axis=-1, keepdims=True) + RMS_EPS) * subln_ref[...] * (1.0 - lam_init)
    o_ref[0, 0] = o.astype(o_ref.dtype)


def _diff_attention(q, k, v, slopes, pos_row, lq1, lk1, lq2, lk2, subln, lam_init):
    b, nh, s_len, d = q.shape
    blk = DIFF_BLOCK
    smem = pl.BlockSpec(memory_space=pltpu.SMEM)
    vec = lambda n: pl.BlockSpec((1, n), lambda bb, h, i: (0, 0))
    return pl.pallas_call(
        functools.partial(_diff_kernel, blk=blk, lam_init=lam_init),
        grid=(b, nh, s_len // blk),
        in_specs=[smem, _head_spec(blk, d), _full_spec(s_len, d), _full_spec(s_len, LANES),
                  pl.BlockSpec((1, s_len), lambda bb, h, i: (0, 0)),
                  vec(DF_DIM), vec(DF_DIM), vec(DF_DIM), vec(DF_DIM), vec(DF_VDIM)],
        out_specs=_head_spec(blk, DF_VDIM),
        out_shape=jax.ShapeDtypeStruct((b, nh, s_len, DF_VDIM), BF16),
        compiler_params=_cparams(("arbitrary",) * 3),
        name="diff",
    )(slopes, q, k, v, pos_row, lq1, lk1, lq2, lk2, subln)


def _layer_norm(v, g, b):
    mu = jnp.mean(v, axis=-1, keepdims=True)
    c = v - mu
    var = jnp.mean(c * c, axis=-1, keepdims=True)
    return c * lax.rsqrt(var + LN_EPS) * g + b


def _sigmoid(z):
    return 1.0 / (1.0 + jnp.exp(-z))


def _route(logits_t, bias_col):
    scores = _sigmoid(logits_t)
    biased = scores + bias_col
    ng = N_GROUPS
    sc = [scores[j * ng:(j + 1) * ng] for j in range(EXPERTS_PER_GROUP)]
    bs = [biased[j * ng:(j + 1) * ng] for j in range(EXPERTS_PER_GROUP)]
    gscore = None
    for a in range(EXPERTS_PER_GROUP):
        for c in range(a + 1, EXPERTS_PER_GROUP):
            pair = bs[a] + bs[c]
            gscore = pair if gscore is None else jnp.maximum(gscore, pair)
    gid = lax.broadcasted_iota(jnp.int32, gscore.shape, 0)
    best = jnp.max(gscore, axis=0, keepdims=True)
    gsel = jnp.min(jnp.where(gscore == best, gid, ng), axis=0, keepdims=True)
    onehot = gid == gsel
    pick = lambda a: jnp.sum(jnp.where(onehot, a, 0.0), axis=0, keepdims=True)
    b_in = [pick(a) for a in bs]
    s_in = [pick(a) for a in sc]
    chosen = []
    for j in range(EXPERTS_PER_GROUP):
        ahead = jnp.zeros_like(b_in[j])
        for c in range(EXPERTS_PER_GROUP):
            if c == j:
                continue
            beats = (b_in[c] > b_in[j]) | ((b_in[c] == b_in[j]) & (c < j))
            ahead = ahead + beats.astype(F32)
        chosen.append(jnp.where(ahead < 2.0, s_in[j], 0.0))
    total = chosen[0] + chosen[1] + chosen[2] + chosen[3]
    gates = [c / total for c in chosen]
    return gates, onehot


def _merge_kernel(x_ref, oa_ref, ob_ref, oc_ref, od_ref, wg_ref, bg_ref, wa_ref, wb_ref, wc_ref, wd_ref,
                  wo_ref, g_ref, b_ref, wr_ref, rb_ref, x1_ref, gate_ref, route_ref, count_sc, *, alpha):
    tm = x_ref.shape[1]
    x = x_ref[0]
    xb = x.astype(BF16)
    merged = None
    for n, (o_ref, w_ref) in enumerate(((oa_ref, wa_ref), (ob_ref, wb_ref), (oc_ref, wc_ref), (od_ref, wd_ref))):
        cols = slice(n * D_MODEL, (n + 1) * D_MODEL)
        gate = _sigmoid(jnp.dot(xb, wg_ref[:, cols], preferred_element_type=F32) + bg_ref[:, cols])
        heads = jnp.concatenate([o_ref[0, hd] for hd in range(o_ref.shape[1])], axis=1)
        term = gate * jnp.dot(heads, w_ref[...], preferred_element_type=F32)
        merged = term if merged is None else merged + term
    mix = jnp.dot(merged.astype(BF16), wo_ref[...], preferred_element_type=F32)
    y = _layer_norm(alpha * x + mix, g_ref[...], b_ref[...])
    x1_ref[0] = y

    logits_t = _nt_dot(wr_ref[...], y, precision=lax.Precision.HIGHEST)
    gates, onehot = _route(logits_t, rb_ref[...])

    @pl.when(pl.program_id(1) % (MOE_ROWS // tm) == 0)
    def _():
        count_sc[...] = jnp.zeros_like(count_sc)

    onef = onehot.astype(F32)
    before = (lax.broadcasted_iota(jnp.int32, (tm, tm), 0) < lax.broadcasted_iota(jnp.int32, (tm, tm), 1))
    prefix = jnp.dot(onehot.astype(BF16), before.astype(BF16), preferred_element_type=F32)
    seen = count_sc[...]
    rank = jnp.sum(onef * (prefix + seen[:, 0:1]), axis=0, keepdims=True)
    count_sc[...] = seen + jnp.sum(onef, axis=1, keepdims=True)
    group = jnp.sum(onef * lax.broadcasted_iota(jnp.int32, onef.shape, 0).astype(F32), axis=0, keepdims=True)

    rows = [jnp.where(onehot, gate, 0.0) for gate in gates] + [group, rank]
    rows.append(jnp.zeros((LANES - N_EXPERTS - 2, tm), F32))
    gate_ref[0] = jnp.concatenate(rows, axis=0).T
    route_ref[0] = jnp.concatenate([group, rank, jnp.zeros((6, tm), F32)], axis=0)


def _merge(x, oa, ob, oc, od, wg, bg, wa, wb, wc, wd, wo, g, b, wr_t, rb_col, alpha):
    bsz, s_len, _ = x.shape
    tm = TOKEN_BLOCK
    row = lambda w: pl.BlockSpec((1, tm, w), lambda bb, i: (bb, i, 0))
    heads = lambda o: pl.BlockSpec((1, o.shape[1], tm, o.shape[3]), lambda bb, i: (bb, 0, i, 0))
    consts = (wg, bg, wa, wb, wc, wd, wo, g, b, wr_t, rb_col)
    return pl.pallas_call(
        functools.partial(_merge_kernel, alpha=alpha),
        grid=(bsz, s_len // tm),
        in_specs=[row(D_MODEL), heads(oa), heads(ob), heads(oc), heads(od)]
                 + [_const_spec(c.shape) for c in consts],
        out_specs=[row(D_MODEL), row(LANES),
                   pl.BlockSpec((1, 8, tm), lambda bb, i: (bb * (s_len // tm) + i, 0, 0))],
        out_shape=[jax.ShapeDtypeStruct((bsz, s_len, D_MODEL), F32),
                   jax.ShapeDtypeStruct((bsz, s_len, LANES), F32),
                   jax.ShapeDtypeStruct((bsz * s_len // tm, 8, tm), F32)],
        scratch_shapes=[pltpu.VMEM((N_GROUPS, LANES), F32)],
        compiler_params=_cparams(("arbitrary", "arbitrary")),
        name="merge",
    )(x, oa, ob, oc, od, *consts)


def _two_bf16(a):
    hi = a.astype(BF16)
    return hi, (a - hi.astype(F32)).astype(BF16)


def _moe_kernel(x_ref, gate_ref, route_ref, w1_ref, w3_ref, w2_ref, g_ref, b_ref, o_ref,
                acc_ref, xb_ref, ghi_ref, glo_ref, *, alpha):
    grp = pl.program_id(1)

    @pl.when(grp == 0)
    def _():
        acc_ref[...] = jnp.zeros_like(acc_ref)
        xb_ref[...] = x_ref[...].astype(BF16)
        ghi_ref[...], glo_ref[...] = _two_bf16(gate_ref[...])

    tm = x_ref.shape[0]
    n_sub = route_ref.shape[0]
    sub = tm // n_sub
    cap = MOE_CAP
    grp_f = grp.astype(F32)
    in_group_col = gate_ref[:, N_EXPERTS:N_EXPERTS + 1] == grp_f
    rank_col = gate_ref[:, N_EXPERTS + 1:N_EXPERTS + 2]
    in_group_row = [route_ref[j, 0:1, :] == grp_f for j in range(n_sub)]
    rank_row = [route_ref[j, 1:2, :] for j in range(n_sub)]
    count = sum(jnp.sum(m.astype(F32)) for m in in_group_row)
    n_chunks = jnp.ceil(count / cap).astype(jnp.int32)
    w2 = w2_ref[...].reshape(EXPERTS_PER_GROUP * D_EXPERT, D_MODEL)
    lane = lax.broadcasted_iota(jnp.int32, (cap, LANES), 1)

    def chunk(c, carry):
        first = c * cap
        slot_col = (lax.broadcasted_iota(jnp.int32, (cap, 1), 0) + first).astype(F32)
        slot_row = (lax.broadcasted_iota(jnp.int32, (1, cap), 1) + first).astype(F32)
        xg = None
        gg = None
        for j in range(n_sub):
            sel = ((rank_row[j] == slot_col) & in_group_row[j]).astype(BF16)
            tok = slice(j * sub, (j + 1) * sub)
            part = jnp.dot(sel, xb_ref[tok, :], preferred_element_type=F32)
            gpart = (jnp.dot(sel, ghi_ref[tok, :], preferred_element_type=F32)
                     + jnp.dot(sel, glo_ref[tok, :], preferred_element_type=F32))
            xg = part if xg is None else xg + part
            gg = gpart if gg is None else gg + gpart
        xg = xg.astype(BF16)
        hid = []
        for e in range(EXPERTS_PER_GROUP):
            h1 = jnp.dot(xg, w1_ref[e], preferred_element_type=F32)
            h3 = jnp.dot(xg, w3_ref[e], preferred_element_type=F32)
            gate = jnp.sum(jnp.where(lane == e * N_GROUPS + grp, gg, 0.0), axis=1, keepdims=True)
            hid.append((h1 * _sigmoid(h1) * h3 * gate).astype(BF16))
        y = jnp.dot(jnp.concatenate(hid, axis=1), w2, preferred_element_type=F32)
        back = ((rank_col == slot_row) & in_group_col).astype(BF16)
        acc_ref[...] += jnp.dot(back, y.astype(BF16), preferred_element_type=F32)
        return carry

    lax.fori_loop(0, n_chunks, chunk, 0)

    @pl.when(grp == N_GROUPS - 1)
    def _():
        o_ref[...] = _layer_norm(alpha * x_ref[...] + acc_ref[...], g_ref[...], b_ref[...])


def _moe(x1, gates, route, w1, w3, w2, g, b, alpha):
    t = x1.shape[0]
    tm = MOE_ROWS
    epg = EXPERTS_PER_GROUP
    n_sub = tm // route.shape[2]
    return pl.pallas_call(
        functools.partial(_moe_kernel, alpha=alpha),
        grid=(t // tm, N_GROUPS),
        in_specs=[pl.BlockSpec((tm, D_MODEL), lambda i, e: (i, 0)),
                  pl.BlockSpec((tm, LANES), lambda i, e: (i, 0)),
                  pl.BlockSpec((n_sub,) + route.shape[1:], lambda i, e: (i, 0, 0)),
                  pl.BlockSpec((epg, D_MODEL, D_EXPERT), lambda i, e: (e, 0, 0)),
                  pl.BlockSpec((epg, D_MODEL, D_EXPERT), lambda i, e: (e, 0, 0)),
                  pl.BlockSpec((epg, D_EXPERT, D_MODEL), lambda i, e: (e, 0, 0)),
                  pl.BlockSpec((1, D_MODEL), lambda i, e: (0, 0)),
                  pl.BlockSpec((1, D_MODEL), lambda i, e: (0, 0))],
        out_specs=pl.BlockSpec((tm, D_MODEL), lambda i, e: (i, 0)),
        out_shape=jax.ShapeDtypeStruct((t, D_MODEL), F32),
        scratch_shapes=[pltpu.VMEM((tm, D_MODEL), F32), pltpu.VMEM((tm, D_MODEL), BF16),
                        pltpu.VMEM((tm, LANES), BF16), pltpu.VMEM((tm, LANES), BF16)],
        compiler_params=_cparams(("arbitrary", "arbitrary")),
        name="moe",
    )(x1, gates, route, w1, w3, w2, g, b)


def _alibi_slopes(n_heads):
    return 2.0 ** (-8.0 * jnp.arange(1, n_heads + 1, dtype=F32) / n_heads)


def _rope_tables(positions):
    half = MLA_ROPE // 2
    inv_freq = ROPE_THETA ** (-jnp.arange(half, dtype=F32) / half)
    ang = positions.astype(F32)[:, None] * inv_freq[None, :]
    cos = jnp.cos(ang)
    sin = jnp.sin(ang)
    return jnp.concatenate([cos, cos], axis=1), jnp.concatenate([sin, sin], axis=1)


def _rotate_half_columns(w):
    half = w.shape[-1] // 2
    return jnp.concatenate([-w[..., half:], w[..., :half]], axis=-1)


def _prep_w_in(w_in):
    split_at = [int(c) for c in np.cumsum(IN_SPLITS)[:-1]]
    parts = jnp.split(w_in, split_at, axis=-1)
    c_q, c_kv, k_rope = parts[0], parts[1], parts[2]
    before = jnp.zeros((w_in.shape[0], MLA_NOPE), w_in.dtype)
    after = jnp.zeros((w_in.shape[0], LANES - MLA_QK), w_in.dtype)
    return jnp.concatenate([c_q, c_kv, before, k_rope, after, before, _rotate_half_columns(k_rope), after]
                           + list(parts[3:]), axis=-1).astype(BF16)


def _pad_heads(w, n_heads, width):
    w3 = w.reshape(w.shape[0], n_heads, width)
    return jnp.pad(w3, ((0, 0), (0, 0), (0, LANES - width))).reshape(w.shape[0], n_heads * LANES)


def _prep_w_uq(w_uq):
    w = w_uq.reshape(MLA_Q_RANK, MLA_HEADS, MLA_QK)
    rot = jnp.concatenate([jnp.zeros_like(w[..., :MLA_NOPE]), _rotate_half_columns(w[..., MLA_NOPE:])], axis=-1)
    return (_pad_heads(w_uq, MLA_HEADS, MLA_QK).astype(BF16),
            _pad_heads(rot.reshape(w_uq.shape), MLA_HEADS, MLA_QK).astype(BF16))


def _prep_w_ukv(w_ukv):
    w = w_ukv.reshape(MLA_KV_RANK, MLA_HEADS, MLA_NOPE + MLA_V)
    wk = w[..., :MLA_NOPE].reshape(MLA_KV_RANK, MLA_HEADS * MLA_NOPE)
    wv = w[..., MLA_NOPE:].reshape(MLA_KV_RANK, MLA_HEADS * MLA_V)
    return _pad_heads(wk, MLA_HEADS, MLA_NOPE).astype(BF16), _pad_heads(wv, MLA_HEADS, MLA_V).astype(BF16)


def kernel(x, positions, w_in, mla_q_norm, mla_kv_norm, w_uq, w_ukv, sw_sinks, df_lq1, df_lk1, df_lq2, df_lk2,
           df_subln, w_gate, b_gate, w_br_a, w_br_b, w_br_c, w_br_d, w_out, ln1_g, ln1_b, w_router, router_bias,
           moe_w1, moe_w3, moe_w2, ln2_g, ln2_b):
    b, s_len, d = x.shape
    depth = w_in.shape[0]
    t = b * s_len
    alpha = (2 * depth) ** 0.25
    assert d == D_MODEL and s_len % TOKEN_BLOCK == 0 and s_len % ATT_BLOCK == 0 and s_len % MOE_ROWS == 0
    assert MOE_ROWS % TOKEN_BLOCK == 0 and s_len % MLA_BLOCK == 0 and s_len % DIFF_BLOCK == 0
    assert SB_DIM == SW_DIM == DF_VDIM == 2 * DF_DIM == HEAD_WIDTH == LANES // 2
    assert ATT_BLOCK % SW_WINDOW == 0

    posf = positions.astype(F32)
    pos_col = posf[:, None]
    pos_row = posf[None, :]
    cos, sin = _rope_tables(positions)
    q_scale = MLA_QK ** -0.5
    ones_n = jnp.ones((s_len, MLA_NOPE), F32)
    tail = jnp.zeros((s_len, LANES - MLA_QK), F32)
    cq = jnp.tile(jnp.concatenate([ones_n * q_scale, cos * q_scale, tail], axis=1), (1, MLA_HEADS))
    sq = jnp.tile(jnp.concatenate([0.0 * ones_n, sin * q_scale, tail], axis=1), (1, MLA_HEADS))
    ck = jnp.concatenate([0.0 * ones_n, cos, tail], axis=1)
    sk = jnp.concatenate([0.0 * ones_n, sin, tail], axis=1)
    rscale = jnp.concatenate([
        jnp.full((B_WIDTH,), SB_DIM ** -0.5, F32), jnp.ones((2 * B_WIDTH,), F32),
        jnp.full((C_WIDTH,), SW_DIM ** -0.5, F32), jnp.ones((2 * SW_KV_HEADS * SW_DIM,), F32),
        jnp.full((2 * DF_HEADS * DF_DIM,), DF_DIM ** -0.5, F32), jnp.ones((2 * DF_HEADS * DF_DIM + D_WIDTH,), F32),
    ])[None, :]
    sw_slopes = _alibi_slopes(SW_Q_HEADS)
    df_slopes = _alibi_slopes(DF_HEADS)

    perm = np.arange(N_EXPERTS).reshape(N_GROUPS, EXPERTS_PER_GROUP).T.reshape(-1)
    wr_t = w_router.astype(F32).T[perm]
    rb_col = router_bias.astype(F32)[perm][:, None]

    x3 = x
    for l in range(depth):
        lam_init = 0.8 - 0.6 * math.exp(-0.3 * l)
        wuq, wuqs = _prep_w_uq(w_uq[l])
        wuk, wuv = _prep_w_ukv(w_ukv[l])
        (mla_q, mla_k, mla_v, sb_q, sb_k, sb_v, sw_q, sw_k, sw_v, df_q, df_k, df_v) = _proj(
            x3, _prep_w_in(w_in[l]), mla_q_norm[l][None, :], mla_kv_norm[l][None, :],
            wuq, wuqs, wuk, wuv, cq, sq, ck, sk, rscale)

        o_a = _mla_attention(mla_q, mla_k, mla_v)
        o_b = _sb_attention(sb_q, sb_k, sb_v)
        o_c = _swa_attention(sw_q, sw_k, sw_v, sw_slopes, sw_sinks[l].astype(F32), pos_col, pos_row)
        o_d = _diff_attention(df_q, df_k, df_v, df_slopes, pos_row,
                              df_lq1[l][None, :], df_lk1[l][None, :], df_lq2[l][None, :],
                              df_lk2[l][None, :], df_subln[l][None, :], lam_init)

        x1, gates, route = _merge(x3, o_a, o_b, o_c, o_d, w_gate[l].astype(BF16), b_gate[l][None, :],
                           w_br_a[l].astype(BF16), w_br_b[l].astype(BF16), w_br_c[l].astype(BF16),
                           w_br_d[l].astype(BF16), w_out[l].astype(BF16), ln1_g[l][None, :], ln1_b[l][None, :],
                           wr_t, rb_col, alpha)
        x2 = _moe(x1.reshape(t, d), gates.reshape(t, LANES), route, moe_w1[l].astype(BF16), moe_w3[l].astype(BF16),
                  moe_w2[l].astype(BF16), ln2_g[l][None, :], ln2_b[l][None, :], alpha)
        x3 = x2.reshape(b, s_len, d)
    return x3
```
